```python
import math
import jax, jax.numpy as jnp
from jax import lax
import numpy as np

D_MODEL = 4096
BATCH = 2
SEQ = 4096
DEPTH = 2

D_SSM = D_MODEL // 2
SSM_GROUP = 16
N_GROUPS = D_SSM // SSM_GROUP
STATE = 64
DT_MIN = 1e-3
DT_MAX = 1e-1
HEAD_DIM = 128
D_ATTN = D_MODEL // 2
N_HEADS = D_ATTN // (2 * HEAD_DIM)
Q_BLOCK = 128
SUBLN_EPS = 1e-5
N_BUCKETS = 32
MAX_DISTANCE = 128
EPS = 1e-6
N_IN = 2 * D_SSM + 4 * D_ATTN + 2 * D_MODEL
NEG_INF = -1e30

kernel_name = 'hybrid_s5_diffattn_adaln_block'


def rms_norm(x, g, eps):
    xf = x.astype(jnp.float32)
    return xf * lax.rsqrt(jnp.mean(xf * xf, axis=-1, keepdims=True) + eps) * g.astype(jnp.float32)


def _complex_affine_combine(e1, e2):
    a1r, a1i, b1r, b1i = e1
    a2r, a2i, b2r, b2i = e2
    ar = a2r * a1r - a2i * a1i
    ai = a2r * a1i + a2i * a1r
    br = a2r * b1r - a2i * b1i + b2r
    bi = a2r * b1i + a2i * b1r + b2i
    return (ar, ai, br, bi)


def s5_mixer(u, lam_re, lam_im, log_dt, b_re, b_im, c_re, c_im, d_skip, w_glu):
    bsz, seq, _ = u.shape
    uf = u.astype(jnp.float32).reshape(bsz, seq, N_GROUPS, SSM_GROUP)
    lr = lam_re.astype(jnp.float32)
    li = lam_im.astype(jnp.float32)
    dt = jnp.exp(log_dt.astype(jnp.float32))[:, None]
    mag = jnp.exp(lr * dt)
    ab_re = mag * jnp.cos(li * dt)
    ab_im = mag * jnp.sin(li * dt)
    den = lr * lr + li * li
    num_re = ab_re - 1.0
    coef_re = (num_re * lr + ab_im * li) / den
    coef_im = (ab_im * lr - num_re * li) / den
    br = b_re.astype(jnp.float32)
    bi = b_im.astype(jnp.float32)
    bb_re = coef_re[..., None] * br - coef_im[..., None] * bi
    bb_im = coef_re[..., None] * bi + coef_im[..., None] * br
    x_re = jnp.einsum('gpc,bsgc->bsgp', bb_re, uf)
    x_im = jnp.einsum('gpc,bsgc->bsgp', bb_im, uf)
    a_re = jnp.broadcast_to(ab_re, x_re.shape)
    a_im = jnp.broadcast_to(ab_im, x_im.shape)
    _, _, h_re, h_im = lax.associative_scan(_complex_affine_combine, (a_re, a_im, x_re, x_im), axis=1)
    y = (jnp.einsum('gcp,bsgp->bsgc', c_re.astype(jnp.float32), h_re)
         - jnp.einsum('gcp,bsgp->bsgc', c_im.astype(jnp.float32), h_im)
         + d_skip.astype(jnp.float32) * uf)
    y = y.reshape(bsz, seq, D_SSM)
    g = jax.nn.gelu(y)
    return g * jax.nn.sigmoid(g @ w_glu.astype(jnp.float32))


def t5_bucket(rel):
    n = jnp.maximum(rel, 0)
    max_exact = N_BUCKETS // 2
    nf = jnp.maximum(n, 1).astype(jnp.float32)
    large = max_exact + (jnp.log(nf / max_exact) / math.log(MAX_DISTANCE / max_exact)
                         * (N_BUCKETS - max_exact)).astype(jnp.int32)
    large = jnp.minimum(large, N_BUCKETS - 1)
    return jnp.where(n < max_exact, n, large)


def diff_attention(q, k, v, rel_bias, lam, lam_init, subln_g):
    bsz, seq = q.shape[0], q.shape[1]
    q = q.astype(jnp.float32) * (HEAD_DIM ** -0.5)
    k = k.astype(jnp.float32)
    v = v.astype(jnp.float32)
    table = rel_bias.astype(jnp.float32)
    k_pos = jnp.arange(seq)

    def block(i):
        start = i * Q_BLOCK
        qb = lax.dynamic_slice_in_dim(q, start, Q_BLOCK, axis=1)
        s = jnp.einsum('bqhmd,bkhmd->bhmqk', qb, k)
        rel = (start + jnp.arange(Q_BLOCK))[:, None] - k_pos[None, :]
        bias = jnp.transpose(table[t5_bucket(rel)], (2, 0, 1))
        s = jnp.where(rel[None, None, None] >= 0, s + bias[None, :, None], NEG_INF)
        p = jax.nn.softmax(s, axis=-1)
        attn = p[:, :, 0] - lam * p[:, :, 1]
        return jnp.einsum('bhqk,bkhd->bqhd', attn, v)

    o = lax.map(block, jnp.arange(seq // Q_BLOCK))
    o = jnp.moveaxis(o, 0, 1).reshape(bsz, seq, N_HEADS, 2 * HEAD_DIM)
    o = rms_norm(o, subln_g, SUBLN_EPS) * (1.0 - lam_init)
    return o.reshape(bsz, seq, D_ATTN)


def setup_inputs(seed: int = 0) -> dict:
    key = jax.random.key(seed)
    ks = jax.random.split(key, 32)
    f32 = jnp.float32

    def nrm(k, shape, scale):
        return jax.random.normal(k, shape, f32) * scale

    L = DEPTH
    lam_re = -0.5 + nrm(ks[6], (L, N_GROUPS, STATE), 0.01)
    lam_im = (math.pi * jnp.arange(STATE, dtype=f32))[None, None, :] + nrm(ks[7], (L, N_GROUPS, STATE), 0.01)
    log_dt = jax.random.uniform(ks[8], (L, N_GROUPS), f32, math.log(DT_MIN), math.log(DT_MAX))
    return {
        'x': nrm(ks[0], (BATCH, SEQ, D_MODEL), 1.0),
        'c': nrm(ks[1], (BATCH, D_MODEL), 1.0),
        'norm_g': 1.0 + nrm(ks[2], (L, D_MODEL), 0.02),
        'w_ada': nrm(ks[3], (L, D_MODEL, 3 * D_MODEL), 0.5 * D_MODEL ** -0.5),
        'b_ada': nrm(ks[4], (L, 3 * D_MODEL), 0.02),
        'w_in': nrm(ks[5], (L, D_MODEL, N_IN), D_MODEL ** -0.5),
        'ssm_lambda_re': lam_re,
        'ssm_lambda_im': lam_im,
        'ssm_log_dt': log_dt,
        'ssm_b_re': nrm(ks[9], (L, N_GROUPS, STATE, SSM_GROUP), (0.5 / SSM_GROUP) ** 0.5),
        'ssm_b_im': nrm(ks[10], (L, N_GROUPS, STATE, SSM_GROUP), (0.5 / SSM_GROUP) ** 0.5),
        'ssm_c_re': nrm(ks[11], (L, N_GROUPS, SSM_GROUP, STATE), (0.5 / STATE) ** 0.5),
        'ssm_c_im': nrm(ks[12], (L, N_GROUPS, SSM_GROUP, STATE), (0.5 / STATE) ** 0.5),
        'ssm_d': nrm(ks[13], (L, N_GROUPS, SSM_GROUP), 1.0),
        'w_glu': nrm(ks[14], (L, D_SSM, D_SSM), D_SSM ** -0.5),
        'lambda_q1': nrm(ks[15], (L, HEAD_DIM), 0.1),
        'lambda_k1': nrm(ks[16], (L, HEAD_DIM), 0.1),
        'lambda_q2': nrm(ks[17], (L, HEAD_DIM), 0.1),
        'lambda_k2': nrm(ks[18], (L, HEAD_DIM), 0.1),
        'subln_g': 1.0 + nrm(ks[19], (L, 2 * HEAD_DIM), 0.02),
        'w_out_ssm': nrm(ks[20], (L, D_SSM, D_MODEL), D_SSM ** -0.5),
        'w_out_attn': nrm(ks[21], (L, D_ATTN, D_MODEL), D_ATTN ** -0.5),
        'w_o': nrm(ks[22], (L, D_MODEL, D_MODEL), D_MODEL ** -0.5),
        'rel_bias': nrm(ks[23], (N_BUCKETS, N_HEADS), 0.5),
        'final_g': 1.0 + nrm(ks[24], (D_MODEL,), 0.02),
    }


def reference(x, c, norm_g, w_ada, b_ada, w_in, ssm_lambda_re, ssm_lambda_im, ssm_log_dt,
              ssm_b_re, ssm_b_im, ssm_c_re, ssm_c_im, ssm_d, w_glu,
              lambda_q1, lambda_k1, lambda_q2, lambda_k2, subln_g,
              w_out_ssm, w_out_attn, w_o, rel_bias, final_g):
    out_dtype = x.dtype
    bsz, seq, _ = x.shape
    widths = [D_SSM, D_SSM, D_ATTN, D_ATTN, D_ATTN, D_ATTN, D_MODEL, D_MODEL]
    offsets = [sum(widths[:i + 1]) for i in range(len(widths) - 1)]
    c_act = jax.nn.silu(c.astype(jnp.float32))
    h_res = x.astype(jnp.float32)
    for l in range(DEPTH):
        mod = c_act @ w_ada[l].astype(jnp.float32) + b_ada[l].astype(jnp.float32)
        shift, scale, gate = jnp.split(mod[:, None, :], 3, axis=-1)
        h = rms_norm(h_res, norm_g[l], EPS) * (1.0 + scale) + shift
        proj = h @ w_in[l].astype(jnp.float32)
        u_s, z_s, q, k, v, z_a, g_s, g_a = jnp.split(proj, offsets, axis=-1)
        y_s = s5_mixer(u_s, ssm_lambda_re[l], ssm_lambda_im[l], ssm_log_dt[l], ssm_b_re[l], ssm_b_im[l],
                       ssm_c_re[l], ssm_c_im[l], ssm_d[l], w_glu[l]) * jax.nn.silu(z_s)
        p_s = y_s @ w_out_ssm[l].astype(jnp.float32)
        lam_init = 0.8 - 0.6 * math.exp(-0.3 * l)
        lam = (jnp.exp(jnp.sum(lambda_q1[l].astype(jnp.float32) * lambda_k1[l].astype(jnp.float32)))
               - jnp.exp(jnp.sum(lambda_q2[l].astype(jnp.float32) * lambda_k2[l].astype(jnp.float32)))
               + lam_init)
        qh = q.reshape(bsz, seq, N_HEADS, 2, HEAD_DIM)
        kh = k.reshape(bsz, seq, N_HEADS, 2, HEAD_DIM)
        vh = v.reshape(bsz, seq, N_HEADS, 2 * HEAD_DIM)
        y_a = diff_attention(qh, kh, vh, rel_bias, lam, lam_init, subln_g[l]) * jax.nn.silu(z_a)
        p_a = y_a @ w_out_attn[l].astype(jnp.float32)
        merged = jax.nn.sigmoid(g_s) * p_s + jax.nn.sigmoid(g_a) * p_a
        h_res = h_res + gate * (merged @ w_o[l].astype(jnp.float32))
    return rms_norm(h_res, final_g, EPS).astype(out_dtype)
```

```python
import functools
import math

import jax
import jax.numpy as jnp
from jax import lax
from jax.experimental import pallas as pl
from jax.experimental.pallas import tpu as pltpu

F32 = jnp.float32
BF16 = jnp.bfloat16

SSM_GROUP = 16
STATE = 64
HEAD_DIM = 128
N_BUCKETS = 32
MAX_DISTANCE = 128
EPS = 1e-6
SUBLN_EPS = 1e-5
NEG_INF = -1e30

CHUNK = 16
CW = CHUNK * SSM_GROUP
SW = 2 * STATE
ATT_BLOCK = 256
V7X_VMEM_LIMIT = 56 * 1024 * 1024


def _sigmoid(x):
    return 1.0 / (1.0 + jnp.exp(-x))


def _silu(x):
    return x * _sigmoid(x)


def _gelu_tanh(x):
    return 0.5 * x * (1.0 + jnp.tanh(math.sqrt(2.0 / math.pi) * (x + 0.044715 * (x * x * x))))


def _tile(dim, pref):
    t = min(dim, pref)
    assert dim % t == 0, (dim, pref)
    return t


def _params(sem):
    return pltpu.CompilerParams(dimension_semantics=sem, vmem_limit_bytes=V7X_VMEM_LIMIT)


def _ada_kernel(c_ref, w_ref, b_ref, o_ref):
    c = c_ref[...]
    ca = _silu(c).astype(BF16)
    o_ref[0] = jnp.dot(ca, w_ref[0].astype(BF16), preferred_element_type=F32) + b_ref[0]


def _ada_mod(c_pad, w_ada, b_ada):
    n_layers, d, n3 = w_ada.shape
    rows = c_pad.shape[0]
    tn = _tile(n3, 1024)
    return pl.pallas_call(
        _ada_kernel,
        grid=(n_layers, n3 // tn),
        in_specs=[
            pl.BlockSpec((rows, d), lambda l, j: (0, 0)),
            pl.BlockSpec((1, d, tn), lambda l, j: (l, 0, j)),
            pl.BlockSpec((1, 1, tn), lambda l, j: (l, 0, j)),
        ],
        out_specs=pl.BlockSpec((1, rows, tn), lambda l, j: (l, 0, j)),
        out_shape=jax.ShapeDtypeStruct((n_layers, rows, n3), F32),
        compiler_params=_params(("arbitrary", "arbitrary")),
        name="ada_mod",
    )(c_pad, w_ada, b_ada.reshape(n_layers, 1, n3))


def _prenorm_kernel(x_ref, g_ref, m_ref, o_ref):
    x = x_ref[0]
    ms = jnp.mean(x * x, axis=-1, keepdims=True)
    shift = m_ref[0, 0:1, :]
    scale = m_ref[0, 1:2, :]
    h = x * lax.rsqrt(ms + EPS) * g_ref[...]
    o_ref[0] = (h * (1.0 + scale) + shift).astype(o_ref.dtype)


def _prenorm(h_res, g, mod3):
    b, s, d = h_res.shape
    ts = _tile(s, 512)
    return pl.pallas_call(
        _prenorm_kernel,
        grid=(b, s // ts),
        in_specs=[
            pl.BlockSpec((1, ts, d), lambda i, j: (i, j, 0)),
            pl.BlockSpec((1, d), lambda i, j: (0, 0)),
            pl.BlockSpec((1, 3, d), lambda i, j: (i, 0, 0)),
        ],
        out_specs=pl.BlockSpec((1, ts, d), lambda i, j: (i, j, 0)),
        out_shape=jax.ShapeDtypeStruct((b, s, d), BF16),
        compiler_params=_params(("arbitrary", "arbitrary")),
        name="prenorm",
    )(h_res, g.reshape(1, d), mod3)


def _inproj_kernel(a_ref, w_ref, o_ref, *, seg_tiles):
    j = pl.program_id(1)
    acc = jnp.dot(a_ref[...], w_ref[...], preferred_element_type=F32)
    st = seg_tiles
    is_silu = jnp.logical_or(jnp.logical_and(j >= st, j < 2 * st), jnp.logical_and(j >= 5 * st, j < 6 * st))
    is_q = jnp.logical_and(j >= 2 * st, j < 3 * st)
    is_sig = j >= 6 * st
    is_id = jnp.logical_or(j < st, jnp.logical_and(j >= 3 * st, j < 5 * st))

    @pl.when(is_id)
    def _():
        o_ref[...] = acc.astype(o_ref.dtype)

    @pl.when(is_q)
    def _():
        o_ref[...] = (acc * (HEAD_DIM ** -0.5)).astype(o_ref.dtype)

    @pl.when(is_silu)
    def _():
        o_ref[...] = _silu(acc).astype(o_ref.dtype)

    @pl.when(is_sig)
    def _():
        o_ref[...] = _sigmoid(acc).astype(o_ref.dtype)


def _in_proj(a, w, d_ssm):
    m, k = a.shape
    n = w.shape[1]
    tm = _tile(m, 1024)
    tn = _tile(d_ssm, 1024)
    return pl.pallas_call(
        functools.partial(_inproj_kernel, seg_tiles=d_ssm // tn),
        grid=(m // tm, n // tn),
        in_specs=[
            pl.BlockSpec((tm, k), lambda i, j: (i, 0)),
            pl.BlockSpec((k, tn), lambda i, j: (0, j)),
        ],
        out_specs=pl.BlockSpec((tm, tn), lambda i, j: (i, j)),
        out_shape=jax.ShapeDtypeStruct((m, n), BF16),
        compiler_params=_params(("arbitrary", "arbitrary")),
        name="in_proj",
    )(a, w)


def _s5_prep_kernel(lr_ref, li_ref, ldt_ref, ba_ref, bb_ref, ca_ref, cb_ref,
                    r_ref, ot_ref, t_ref, aux_ref, z_scr, *, nst):
    gb = lr_ref.shape[0]
    lr = lr_ref[...]
    li = li_ref[...]
    dt = jnp.exp(ldt_ref[...])
    mag = jnp.exp(lr * dt)
    are = mag * jnp.cos(li * dt)
    aim = mag * jnp.sin(li * dt)
    den = lr * lr + li * li
    nre = are - 1.0
    cre = (nre * lr + aim * li) / den
    cim = (aim * lr - nre * li) / den
    b_a = ba_ref[...]
    b_b = bb_ref[...]
    c_a = ca_ref[...]
    c_b = cb_ref[...]
    cre3 = cre[:, None, :]
    cim3 = cim[:, None, :]
    bbv = cre3 * b_a + cim3 * b_b
    bbs = cre3 * b_b - cim3 * b_a

    pre = jnp.ones_like(are)
    pim = jnp.zeros_like(are)
    pows = [(pre, pim)]
    for _ in range(CHUNK):
        pre, pim = are * pre - aim * pim, are * pim + aim * pre
        pows.append((pre, pim))

    for k in range(CHUNK + 1):
        pr, pi = pows[k]
        z_k = pr[:, None, :] * c_a + pi[:, None, :] * c_b
        if k < CHUNK:
            z_scr[:, k * SSM_GROUP:(k + 1) * SSM_GROUP, :] = z_k
        if k >= 1:
            ot_ref[:, (k - 1) * SSM_GROUP:k * SSM_GROUP, :] = z_k.astype(ot_ref.dtype)
    for s in range(CHUNK):
        pr, pi = pows[CHUNK - 1 - s]
        r_s = pr[:, None, :] * bbv + pi[:, None, :] * bbs
        r_ref[:, s * SSM_GROUP:(s + 1) * SSM_GROUP, :] = r_s.astype(r_ref.dtype)

    lane = lax.broadcasted_iota(jnp.int32, (SSM_GROUP, CW), 1)
    for g in range(gb):
        km = lax.dot_general(bbv[g], z_scr[g], (((1,), (1,)), ((), ())),
                             precision=lax.Precision.HIGHEST, preferred_element_type=F32)
        for s in range(CHUNK):
            if s == 0:
                blk = km
            else:
                blk = jnp.where(lane >= s * SSM_GROUP, pltpu.roll(km, s * SSM_GROUP, axis=1), 0.0)
            t_ref[g, s * SSM_GROUP:(s + 1) * SSM_GROUP, :] = blk.astype(t_ref.dtype)

    half = lax.broadcasted_iota(jnp.int32, are.shape, 1) < STATE
    sre, sim = pows[CHUNK]
    for k in range(nst):
        aux_ref[k] = sre
        aux_ref[nst + k] = jnp.where(half, -sim, sim)
        sre, sim = sre * sre - sim * sim, 2.0 * sre * sim


def _s5_prep(lam_re, lam_im, log_dt, b_re, b_im, c_re, c_im, nst):
    g = lam_re.shape[0]
    gb = _tile(g, 8)
    lr2 = jnp.concatenate([lam_re, lam_re], axis=-1)
    li2 = jnp.concatenate([lam_im, lam_im], axis=-1)
    ldt2 = jnp.broadcast_to(log_dt[:, None], (g, SW))
    brt = jnp.swapaxes(b_re, 1, 2)
    bit = jnp.swapaxes(b_im, 1, 2)
    b_a = jnp.concatenate([brt, bit], axis=-1)
    b_b = jnp.concatenate([-bit, brt], axis=-1)
    c_a = jnp.concatenate([c_re, -c_im], axis=-1)
    c_b = jnp.concatenate([-c_im, -c_re], axis=-1)
    vec = pl.BlockSpec((gb, SW), lambda i: (i, 0))
    mat = pl.BlockSpec((gb, SSM_GROUP, SW), lambda i: (i, 0, 0))
    return pl.pallas_call(
        functools.partial(_s5_prep_kernel, nst=nst),
        grid=(g // gb,),
        in_specs=[vec, vec, vec, mat, mat, mat, mat],
        out_specs=[
            pl.BlockSpec((gb, CW, SW), lambda i: (i, 0, 0)),
            pl.BlockSpec((gb, CW, SW), lambda i: (i, 0, 0)),
            pl.BlockSpec((gb, CW, CW), lambda i: (i, 0, 0)),
            pl.BlockSpec((2 * nst, gb, SW), lambda i: (0, i, 0)),
        ],
        out_shape=[
            jax.ShapeDtypeStruct((g, CW, SW), BF16),
            jax.ShapeDtypeStruct((g, CW, SW), BF16),
            jax.ShapeDtypeStruct((g, CW, CW), BF16),
            jax.ShapeDtypeStruct((2 * nst, g, SW), F32),
        ],
        scratch_shapes=[pltpu.VMEM((gb, CW, SW), F32)],
        compiler_params=_params(("arbitrary",)),
        name="s5_prep",
    )(lr2, li2, ldt2, b_a, b_b, c_a, c_b)


def _s5_main_kernel(u_ref, t_ref, r_ref, ot_ref, aux_ref, d_ref, o_ref, *, nst, npb):
    gb, nc, _ = u_ref.shape
    row = lax.broadcasted_iota(jnp.int32, (nc, SW), 0)
    row_in_batch = jnp.bitwise_and(row, npb - 1)
    for g in range(gb):
        u = u_ref[g]
        h = jnp.dot(u, r_ref[g], preferred_element_type=F32)
        for k in range(nst):
            d = 1 << k
            hs = jnp.where(row_in_batch >= d, pltpu.roll(h, d, axis=0), 0.0)
            ar = aux_ref[k, g:g + 1, :]
            ai = aux_ref[nst + k, g:g + 1, :]
            h = h + ar * hs + ai * pltpu.roll(hs, STATE, axis=1)
        h_in = jnp.where(row_in_batch >= 1, pltpu.roll(h, 1, axis=0), 0.0)
        y = jnp.dot(u, t_ref[g], preferred_element_type=F32)
        y = y + lax.dot_general(h_in.astype(BF16), ot_ref[g], (((1,), (1,)), ((), ())),
                                preferred_element_type=F32)
        y = y + d_ref[g] * u.astype(F32)
        o_ref[g] = _gelu_tanh(y).astype(o_ref.dtype)


def _s5_main(u_t, t_op, r_op, ot_op, aux, d_tiled, nst, npb):
    g, nc, _ = u_t.shape
    gb = _tile(g, 8)
    return pl.pallas_call(
        functools.partial(_s5_main_kernel, nst=nst, npb=npb),
        grid=(g // gb,),
        in_specs=[
            pl.BlockSpec((gb, nc, CW), lambda i: (i, 0, 0)),
            pl.BlockSpec((gb, CW, CW), lambda i: (i, 0, 0)),
            pl.BlockSpec((gb, CW, SW), lambda i: (i, 0, 0)),
            pl.BlockSpec((gb, CW, SW), lambda i: (i, 0, 0)),
            pl.BlockSpec((2 * nst, gb, SW), lambda i: (0, i, 0)),
            pl.BlockSpec((gb, 1, CW), lambda i: (i, 0, 0)),
        ],
        out_specs=pl.BlockSpec((gb, nc, CW), lambda i: (i, 0, 0)),
        out_shape=jax.ShapeDtypeStruct((g, nc, CW), BF16),
        compiler_params=_params(("arbitrary",)),
        name="s5_main",
    )(u_t, t_op, r_op, ot_op, aux, d_tiled)


def _glu_kernel(a_ref, w_ref, g_ref, z_ref, o_ref):
    acc = jnp.dot(a_ref[...], w_ref[...], preferred_element_type=F32)
    g = g_ref[...].astype(F32)
    o_ref[...] = (g * _sigmoid(acc) * z_ref[...].astype(F32)).astype(o_ref.dtype)


def _glu(g_act, w_glu, proj, d_ssm):
    m, k = g_act.shape
    tm = _tile(m, 1024)
    tn = _tile(d_ssm, 1024)
    zoff = d_ssm // tn
    return pl.pallas_call(
        _glu_kernel,
        grid=(m // tm, d_ssm // tn),
        in_specs=[
            pl.BlockSpec((tm, k), lambda i, j: (i, 0)),
            pl.BlockSpec((k, tn), lambda i, j: (0, j)),
            pl.BlockSpec((tm, tn), lambda i, j: (i, j)),
            pl.BlockSpec((tm, tn), lambda i, j: (i, zoff + j)),
        ],
        out_specs=pl.BlockSpec((tm, tn), lambda i, j: (i, j)),
        out_shape=jax.ShapeDtypeStruct((m, d_ssm), BF16),
        compiler_params=_params(("arbitrary", "arbitrary")),
        name="glu",
    )(g_act, w_glu, g_act, proj)


def _bias_kernel(tab_ref, o_ref):
    h = pl.program_id(0)
    blk = o_ref.shape[-1]
    r = lax.broadcasted_iota(jnp.int32, (blk, blk), 0)
    c = lax.broadcasted_iota(jnp.int32, (blk, blk), 1)
    max_exact = N_BUCKETS // 2
    far = tab_ref[h * N_BUCKETS + N_BUCKETS - 1]
    for t in range(2):
        rel = r - c + t * blk
        n = jnp.maximum(rel, 0)
        nf = jnp.maximum(n, 1).astype(F32)
        large = max_exact + (jnp.log(nf / max_exact) / math.log(MAX_DISTANCE / max_exact)
                             * (N_BUCKETS - max_exact)).astype(jnp.int32)
        large = jnp.minimum(large, N_BUCKETS - 1)
        bucket = jnp.where(n < max_exact, n, large)
        val = jnp.zeros((blk, blk), F32)
        for b in range(N_BUCKETS):
            val = jnp.where(bucket == b, tab_ref[h * N_BUCKETS + b], val)
        o_ref[0, t] = jnp.where(rel >= 0, val - far, NEG_INF)


def _bias_tiles(rel_bias, n_heads, blk):
    tab = jnp.transpose(rel_bias.astype(F32)).reshape(-1)
    return pl.pallas_call(
        _bias_kernel,
        grid=(n_heads,),
        in_specs=[pl.BlockSpec(memory_space=pltpu.SMEM)],
        out_specs=pl.BlockSpec((1, 2, blk, blk), lambda h: (h, 0, 0, 0)),
        out_shape=jax.ShapeDtypeStruct((n_heads, 2, blk, blk), F32),
        compiler_params=_params(("arbitrary",)),
        name="t5_bias_tiles",
    )(tab)


def _attn_kernel(q_ref, k_ref, v_ref, z_ref, bias_ref, lamv_ref, sg_ref, o_ref,
                 m_scr, l_scr, acc_scr, *, lam_init):
    qi = pl.program_id(2)
    blk = q_ref.shape[0]
    q = q_ref[...]
    qs = (q[:, :HEAD_DIM], q[:, HEAD_DIM:])

    m_scr[...] = jnp.full(m_scr.shape, -jnp.inf, F32)
    l_scr[...] = jnp.zeros(l_scr.shape, F32)
    acc_scr[...] = jnp.zeros(acc_scr.shape, F32)

    def step(j, bias):
        off = pl.multiple_of(j * blk, blk)
        k = k_ref[pl.ds(off, blk), :]
        v = v_ref[pl.ds(off, blk), :]
        for mp in range(2):
            s = lax.dot_general(qs[mp], k[:, mp * HEAD_DIM:(mp + 1) * HEAD_DIM],
                                (((1,), (1,)), ((), ())), preferred_element_type=F32)
            if bias is not None:
                s = s + bias
            m_prev = m_scr[mp]
            m_new = jnp.maximum(m_prev, jnp.max(s, axis=-1, keepdims=True))
            alpha = jnp.exp(m_prev - m_new)
            p = jnp.exp(s - m_new)
            l_scr[mp] = alpha * l_scr[mp] + jnp.sum(p, axis=-1, keepdims=True)
            acc_scr[mp] = alpha * acc_scr[mp] + jnp.dot(p.astype(BF16), v, preferred_element_type=F32)
            m_scr[mp] = m_new

    def far_body(j, carry):
        step(j, None)
        return carry

    lax.fori_loop(0, jnp.maximum(qi - 1, 0), far_body, 0)

    @pl.when(qi >= 1)
    def _():
        step(qi - 1, bias_ref[0, 1])

    step(qi, bias_ref[0, 0])

    lv = lamv_ref[...]
    lam = (jnp.exp(jnp.sum(lv[0:1] * lv[1:2], axis=-1, keepdims=True))
           - jnp.exp(jnp.sum(lv[2:3] * lv[3:4], axis=-1, keepdims=True)) + lam_init)
    o = acc_scr[0] / l_scr[0] - lam * (acc_scr[1] / l_scr[1])
    ms = jnp.mean(o * o, axis=-1, keepdims=True)
    o = o * lax.rsqrt(ms + SUBLN_EPS) * sg_ref[...] * (1.0 - lam_init)
    o_ref[...] = (o * z_ref[...].astype(F32)).astype(o_ref.dtype)


def _attention(proj, bias_tiles, lamv, subln_g, bsz, seq, d_attn, lam_init):
    hw = 2 * HEAD_DIM
    n_heads = d_attn // hw
    blk = _tile(seq, ATT_BLOCK)
    nq = seq // blk
    qoff, koff, voff, zoff = (2 * d_attn // hw, 3 * d_attn // hw, 4 * d_attn // hw, 5 * d_attn // hw)
    return pl.pallas_call(
        functools.partial(_attn_kernel, lam_init=lam_init),
        grid=(bsz, n_heads, nq),
        in_specs=[
            pl.BlockSpec((blk, hw), lambda b, h, i: (b * nq + i, qoff + h)),
            pl.BlockSpec((seq, hw), lambda b, h, i: (b, koff + h)),
            pl.BlockSpec((seq, hw), lambda b, h, i: (b, voff + h)),
            pl.BlockSpec((blk, hw), lambda b, h, i: (b * nq + i, zoff + h)),
            pl.BlockSpec((1, 2, blk, blk), lambda b, h, i: (h, 0, 0, 0)),
            pl.BlockSpec((4, HEAD_DIM), lambda b, h, i: (0, 0)),
            pl.BlockSpec((1, hw), lambda b, h, i: (0, 0)),
        ],
        out_specs=pl.BlockSpec((blk, hw), lambda b, h, i: (b * nq + i, h)),
        out_shape=jax.ShapeDtypeStruct((bsz * seq, d_attn), BF16),
        scratch_shapes=[
            pltpu.VMEM((2, blk, 1), F32),
            pltpu.VMEM((2, blk, 1), F32),
            pltpu.VMEM((2, blk, hw), F32),
        ],
        compiler_params=_params(("arbitrary", "arbitrary", "arbitrary")),
        name="diff_attention",
    )(proj, proj, proj, proj, bias_tiles, lamv, subln_g.reshape(1, hw))


def _merge_kernel(ys_ref, ya_ref, ws_ref, wa_ref, gs_ref, ga_ref, o_ref):
    ps = jnp.dot(ys_ref[...], ws_ref[...], preferred_element_type=F32)
    pa = jnp.dot(ya_ref[...], wa_ref[...], preferred_element_type=F32)
    o_ref[...] = (gs_ref[...].astype(F32) * ps + ga_ref[...].astype(F32) * pa).astype(o_ref.dtype)


def _out_merge(y_s, y_a, w_s, w_a, proj, d_ssm, d_attn):
    m, ks = y_s.shape
    ka = y_a.shape[1]
    d = w_s.shape[1]
    tm = _tile(m, 1024)
    tn = _tile(d, 1024)
    gs_off = (2 * d_ssm + 4 * d_attn) // tn
    ga_off = gs_off + d // tn
    return pl.pallas_call(
        _merge_kernel,
        grid=(m // tm, d // tn),
        in_specs=[
            pl.BlockSpec((tm, ks), lambda i, j: (i, 0)),
            pl.BlockSpec((tm, ka), lambda i, j: (i, 0)),
            pl.BlockSpec((ks, tn), lambda i, j: (0, j)),
            pl.BlockSpec((ka, tn), lambda i, j: (0, j)),
            pl.BlockSpec((tm, tn), lambda i, j: (i, gs_off + j)),
            pl.BlockSpec((tm, tn), lambda i, j: (i, ga_off + j)),
        ],
        out_specs=pl.BlockSpec((tm, tn), lambda i, j: (i, j)),
        out_shape=jax.ShapeDtypeStruct((m, d), BF16),
        compiler_params=_params(("arbitrary", "arbitrary")),
        name="out_merge",
    )(y_s, y_a, w_s, w_a, proj, proj)


def _resid_kernel(a_ref, w_ref, h_ref, m_ref, o_ref):
    acc = jnp.dot(a_ref[...], w_ref[...], preferred_element_type=F32)
    o_ref[...] = h_ref[...] + m_ref[0, 2:3, :] * acc


def _resid(merged, w_o, h_res2, mod3, bsz):
    m, k = merged.shape
    d = w_o.shape[1]
    seq = m // bsz
    tm = _tile(seq, 1024)
    tn = _tile(d, 1024)
    ns = seq // tm
    return pl.pallas_call(
        _resid_kernel,
        grid=(bsz, ns, d // tn),
        in_specs=[
            pl.BlockSpec((tm, k), lambda b, i, j: (b * ns + i, 0)),
            pl.BlockSpec((k, tn), lambda b, i, j: (0, j)),
            pl.BlockSpec((tm, tn), lambda b, i, j: (b * ns + i, j)),
            pl.BlockSpec((1, 3, tn), lambda b, i, j: (b, 0, j)),
        ],
        out_specs=pl.BlockSpec((tm, tn), lambda b, i, j: (b * ns + i, j)),
        out_shape=jax.ShapeDtypeStruct((m, d), F32),
        input_output_aliases={2: 0},
        compiler_params=_params(("arbitrary", "arbitrary", "arbitrary")),
        name="resid",
    )(merged, w_o, h_res2, mod3)


def _final_norm_kernel(x_ref, g_ref, o_ref):
    x = x_ref[...]
    ms = jnp.mean(x * x, axis=-1, keepdims=True)
    o_ref[...] = (x * lax.rsqrt(ms + EPS) * g_ref[...]).astype(o_ref.dtype)


def _final_norm(h2, g, out_dtype):
    m, d = h2.shape
    tm = _tile(m, 512)
    return pl.pallas_call(
        _final_norm_kernel,
        grid=(m // tm,),
        in_specs=[pl.BlockSpec((tm, d), lambda i: (i, 0)), pl.BlockSpec((1, d), lambda i: (0, 0))],
        out_specs=pl.BlockSpec((tm, d), lambda i: (i, 0)),
        out_shape=jax.ShapeDtypeStruct((m, d), out_dtype),
        compiler_params=_params(("arbitrary",)),
        name="final_norm",
    )(h2, g.reshape(1, d))


def kernel(x, c, norm_g, w_ada, b_ada, w_in, ssm_lambda_re, ssm_lambda_im, ssm_log_dt, ssm_b_re, ssm_b_im, ssm_c_re, ssm_c_im, ssm_d, w_glu, lambda_q1, lambda_k1, lambda_q2, lambda_k2, subln_g, w_out_ssm, w_out_attn, w_o, rel_bias, final_g):
    out_dtype = x.dtype
    bsz, seq, d = x.shape
    depth = w_in.shape[0]
    d_ssm = w_glu.shape[1]
    d_attn = w_out_attn.shape[1]
    n_groups = d_ssm // SSM_GROUP
    n_heads = d_attn // (2 * HEAD_DIM)
    npb = seq // CHUNK
    nst = npb.bit_length() - 1
    assert seq % CHUNK == 0 and (1 << nst) == npb, "sequence must be CHUNK * 2^k"
    nc = bsz * npb

    c_pad = jnp.zeros((8, d), F32).at[:bsz].set(c.astype(F32))
    mod = _ada_mod(c_pad, w_ada.astype(F32), b_ada.astype(F32))
    bias_tiles = _bias_tiles(rel_bias, n_heads, _tile(seq, ATT_BLOCK))

    h_res = x.astype(F32)
    for l in range(depth):
        mod3 = mod[l, :bsz].reshape(bsz, 3, d)
        hn = _prenorm(h_res, norm_g[l].astype(F32), mod3)
        proj = _in_proj(hn.reshape(bsz * seq, d), w_in[l].astype(BF16), d_ssm)

        r_op, ot_op, t_op, aux = _s5_prep(
            ssm_lambda_re[l].astype(F32), ssm_lambda_im[l].astype(F32), ssm_log_dt[l].astype(F32),
            ssm_b_re[l].astype(F32), ssm_b_im[l].astype(F32), ssm_c_re[l].astype(F32), ssm_c_im[l].astype(F32), nst)
        u_t = proj[:, :d_ssm].reshape(nc, CHUNK, n_groups, SSM_GROUP).transpose(2, 0, 1, 3).reshape(n_groups, nc, CW)
        d_tiled = jnp.tile(ssm_d[l].astype(F32), (1, CHUNK)).reshape(n_groups, 1, CW)
        g_t = _s5_main(u_t, t_op, r_op, ot_op, aux, d_tiled, nst, npb)
        g_act = g_t.reshape(n_groups, nc, CHUNK, SSM_GROUP).transpose(1, 2, 0, 3).reshape(bsz * seq, d_ssm)
        y_s = _glu(g_act, w_glu[l].astype(BF16), proj, d_ssm)

        lam_init = 0.8 - 0.6 * math.exp(-0.3 * l)
        lamv = jnp.stack([lambda_q1[l], lambda_k1[l], lambda_q2[l], lambda_k2[l]]).astype(F32)
        y_a = _attention(proj, bias_tiles, lamv, subln_g[l].astype(F32), bsz, seq, d_attn, lam_init)

        merged = _out_merge(y_s, y_a, w_out_ssm[l].astype(BF16), w_out_attn[l].astype(BF16), proj, d_ssm, d_attn)
        h_res = _resid(merged, w_o[l].astype(BF16), h_res.reshape(bsz * seq, d), mod3, bsz).reshape(bsz, seq, d)

    out = _final_norm(h_res.reshape(bsz * seq, d), final_g.astype(F32), out_dtype)
    return out.reshape(bsz, seq, d)
```

```python
import functools
import math

import jax
import jax.numpy as jnp
from jax import lax
from jax.experimental import pallas as pl
from jax.experimental.pallas import tpu as pltpu

F32 = jnp.float32
BF16 = jnp.bfloat16

SSM_GROUP = 16
STATE = 64
HEAD_DIM = 128
N_BUCKETS = 32
MAX_DISTANCE = 128
EPS = 1e-6
SUBLN_EPS = 1e-5
NEG_INF = -1e30

CHUNK = 16
CW = CHUNK * SSM_GROUP
SW = 2 * STATE
ATT_BLOCK = 512
LANES = 128
GROUPS_PER_STEP = LANES // SSM_GROUP
LOG2E = math.log2(math.e)
V7X_VMEM_LIMIT = 56 * 1024 * 1024


def _sigmoid(x):
    return 1.0 / (1.0 + jnp.exp(-x))


def _silu(x):
    return x * _sigmoid(x)


def _gelu_tanh(x):
    return 0.5 * x * (1.0 + jnp.tanh(math.sqrt(2.0 / math.pi) * (x + 0.044715 * (x * x * x))))


def _tile(dim, pref):
    t = min(dim, pref)
    assert dim % t == 0, (dim, pref)
    return t


def _params(sem):
    return pltpu.CompilerParams(dimension_semantics=sem, vmem_limit_bytes=V7X_VMEM_LIMIT)


def _ada_kernel(c_ref, w_ref, b_ref, o_ref):
    c = c_ref[...]
    ca = _silu(c).astype(BF16)
    o_ref[0] = jnp.dot(ca, w_ref[0].astype(BF16), preferred_element_type=F32) + b_ref[0]


def _ada_mod(c_pad, w_ada, b_ada):
    n_layers, d, n3 = w_ada.shape
    rows = c_pad.shape[0]
    tn = _tile(n3, 1024)
    return pl.pallas_call(
        _ada_kernel,
        grid=(n_layers, n3 // tn),
        in_specs=[
            pl.BlockSpec((rows, d), lambda l, j: (0, 0)),
            pl.BlockSpec((1, d, tn), lambda l, j: (l, 0, j)),
            pl.BlockSpec((1, 1, tn), lambda l, j: (l, 0, j)),
        ],
        out_specs=pl.BlockSpec((1, rows, tn), lambda l, j: (l, 0, j)),
        out_shape=jax.ShapeDtypeStruct((n_layers, rows, n3), F32),
        compiler_params=_params(("arbitrary", "arbitrary")),
        name="ada_mod",
    )(c_pad, w_ada, b_ada.reshape(n_layers, 1, n3))


def _prenorm_kernel(x_ref, g_ref, m_ref, o_ref):
    x = x_ref[0]
    ms = jnp.mean(x * x, axis=-1, keepdims=True)
    shift = m_ref[0, 0:1, :]
    scale = m_ref[0, 1:2, :]
    h = x * lax.rsqrt(ms + EPS) * g_ref[...]
    o_ref[0] = (h * (1.0 + scale) + shift).astype(o_ref.dtype)


def _prenorm(h_res, g, mod3):
    b, s, d = h_res.shape
    ts = _tile(s, 512)
    return pl.pallas_call(
        _prenorm_kernel,
        grid=(b, s // ts),
        in_specs=[
            pl.BlockSpec((1, ts, d), lambda i, j: (i, j, 0)),
            pl.BlockSpec((1, d), lambda i, j: (0, 0)),
            pl.BlockSpec((1, 3, d), lambda i, j: (i, 0, 0)),
        ],
        out_specs=pl.BlockSpec((1, ts, d), lambda i, j: (i, j, 0)),
        out_shape=jax.ShapeDtypeStruct((b, s, d), BF16),
        compiler_params=_params(("arbitrary", "arbitrary")),
        name="prenorm",
    )(h_res, g.reshape(1, d), mod3)


def _inproj_kernel(a_ref, w_ref, o_ref, *, seg_tiles):
    j = pl.program_id(1)
    acc = jnp.dot(a_ref[...], w_ref[...], preferred_element_type=F32)
    st = seg_tiles
    is_silu = jnp.logical_or(jnp.logical_and(j >= st, j < 2 * st), jnp.logical_and(j >= 5 * st, j < 6 * st))
    is_q = jnp.logical_and(j >= 2 * st, j < 3 * st)
    is_sig = j >= 6 * st
    is_id = jnp.logical_or(j < st, jnp.logical_and(j >= 3 * st, j < 5 * st))

    @pl.when(is_id)
    def _():
        o_ref[...] = acc.astype(o_ref.dtype)

    @pl.when(is_q)
    def _():
        o_ref[...] = (acc * (HEAD_DIM ** -0.5 * LOG2E)).astype(o_ref.dtype)

    @pl.when(is_silu)
    def _():
        o_ref[...] = _silu(acc).astype(o_ref.dtype)

    @pl.when(is_sig)
    def _():
        o_ref[...] = _sigmoid(acc).astype(o_ref.dtype)


def _in_proj(a, w, d_ssm):
    m, k = a.shape
    n = w.shape[1]
    tm = _tile(m, 1024)
    tn = _tile(d_ssm, 1024)
    return pl.pallas_call(
        functools.partial(_inproj_kernel, seg_tiles=d_ssm // tn),
        grid=(m // tm, n // tn),
        in_specs=[
            pl.BlockSpec((tm, k), lambda i, j: (i, 0)),
            pl.BlockSpec((k, tn), lambda i, j: (0, j)),
        ],
        out_specs=pl.BlockSpec((tm, tn), lambda i, j: (i, j)),
        out_shape=jax.ShapeDtypeStruct((m, n), BF16),
        compiler_params=_params(("arbitrary", "arbitrary")),
        name="in_proj",
    )(a, w)


def _s5_prep_kernel(lr_ref, li_ref, ldt_ref, ba_ref, bb_ref, ca_ref, cb_ref,
                    r_ref, ot_ref, t_ref, aux_ref, z_scr, *, nst):
    gb = lr_ref.shape[0]
    lr = lr_ref[...]
    li = li_ref[...]
    dt = jnp.exp(ldt_ref[...])
    mag = jnp.exp(lr * dt)
    are = mag * jnp.cos(li * dt)
    aim = mag * jnp.sin(li * dt)
    den = lr * lr + li * li
    nre = are - 1.0
    cre = (nre * lr + aim * li) / den
    cim = (aim * lr - nre * li) / den
    b_a = ba_ref[...]
    b_b = bb_ref[...]
    c_a = ca_ref[...]
    c_b = cb_ref[...]
    cre3 = cre[:, None, :]
    cim3 = cim[:, None, :]
    bbv = cre3 * b_a + cim3 * b_b
    bbs = cre3 * b_b - cim3 * b_a

    pre = jnp.ones_like(are)
    pim = jnp.zeros_like(are)
    pows = [(pre, pim)]
    for _ in range(CHUNK):
        pre, pim = are * pre - aim * pim, are * pim + aim * pre
        pows.append((pre, pim))

    for k in range(CHUNK + 1):
        pr, pi = pows[k]
        z_k = pr[:, None, :] * c_a + pi[:, None, :] * c_b
        if k < CHUNK:
            z_scr[:, k * SSM_GROUP:(k + 1) * SSM_GROUP, :] = z_k
        if k >= 1:
            ot_ref[:, (k - 1) * SSM_GROUP:k * SSM_GROUP, :] = z_k.astype(ot_ref.dtype)
    for s in range(CHUNK):
        pr, pi = pows[CHUNK - 1 - s]
        r_s = pr[:, None, :] * bbv + pi[:, None, :] * bbs
        r_ref[:, s * SSM_GROUP:(s + 1) * SSM_GROUP, :] = r_s.astype(r_ref.dtype)

    lane = lax.broadcasted_iota(jnp.int32, (SSM_GROUP, CW), 1)
    for g in range(gb):
        km = lax.dot_general(bbv[g], z_scr[g], (((1,), (1,)), ((), ())),
                             precision=lax.Precision.HIGHEST, preferred_element_type=F32)
        for s in range(CHUNK):
            if s == 0:
                blk = km
            else:
                blk = jnp.where(lane >= s * SSM_GROUP, pltpu.roll(km, s * SSM_GROUP, axis=1), 0.0)
            t_ref[g, s * SSM_GROUP:(s + 1) * SSM_GROUP, :] = blk.astype(t_ref.dtype)

    half = lax.broadcasted_iota(jnp.int32, are.shape, 1) < STATE
    sre, sim = pows[CHUNK]
    for k in range(nst):
        aux_ref[k] = sre
        aux_ref[nst + k] = jnp.where(half, -sim, sim)
        sre, sim = sre * sre - sim * sim, 2.0 * sre * sim


def _s5_prep(lam_re, lam_im, log_dt, b_re, b_im, c_re, c_im, nst):
    g = lam_re.shape[0]
    gb = _tile(g, 8)
    lr2 = jnp.concatenate([lam_re, lam_re], axis=-1)
    li2 = jnp.concatenate([lam_im, lam_im], axis=-1)
    ldt2 = jnp.broadcast_to(log_dt[:, None], (g, SW))
    brt = jnp.swapaxes(b_re, 1, 2)
    bit = jnp.swapaxes(b_im, 1, 2)
    b_a = jnp.concatenate([brt, bit], axis=-1)
    b_b = jnp.concatenate([-bit, brt], axis=-1)
    c_a = jnp.concatenate([c_re, -c_im], axis=-1)
    c_b = jnp.concatenate([-c_im, -c_re], axis=-1)
    vec = pl.BlockSpec((gb, SW), lambda i: (i, 0))
    mat = pl.BlockSpec((gb, SSM_GROUP, SW), lambda i: (i, 0, 0))
    return pl.pallas_call(
        functools.partial(_s5_prep_kernel, nst=nst),
        grid=(g // gb,),
        in_specs=[vec, vec, vec, mat, mat, mat, mat],
        out_specs=[
            pl.BlockSpec((gb, CW, SW), lambda i: (i, 0, 0)),
            pl.BlockSpec((gb, CW, SW), lambda i: (i, 0, 0)),
            pl.BlockSpec((gb, CW, CW), lambda i: (i, 0, 0)),
            pl.BlockSpec((2 * nst, gb, SW), lambda i: (0, i, 0)),
        ],
        out_shape=[
            jax.ShapeDtypeStruct((g, CW, SW), BF16),
            jax.ShapeDtypeStruct((g, CW, SW), BF16),
            jax.ShapeDtypeStruct((g, CW, CW), BF16),
            jax.ShapeDtypeStruct((2 * nst, g, SW), F32),
        ],
        scratch_shapes=[pltpu.VMEM((gb, CW, SW), F32)],
        compiler_params=_params(("arbitrary",)),
        name="s5_prep",
    )(lr2, li2, ldt2, b_a, b_b, c_a, c_b)


def _s5_main_kernel(*refs, nst, npb):
    x_refs = refs[:CHUNK]
    p_ref, t_ref, r_ref, ot_ref, aux_ref, d_ref, o_ref, g_scr = refs[CHUNK:]
    nc = x_refs[0].shape[0]
    row = lax.broadcasted_iota(jnp.int32, (nc, SW), 0)
    row_in_batch = jnp.bitwise_and(row, npb - 1)
    xcat = jnp.concatenate([x[...] for x in x_refs], axis=1)
    up = jnp.dot(xcat, p_ref[...], preferred_element_type=F32).astype(BF16)
    for g in range(GROUPS_PER_STEP):
        u = up[:, g * CW:(g + 1) * CW]
        h = jnp.dot(u, r_ref[g], preferred_element_type=F32)
        for k in range(nst):
            d = 1 << k
            hs = jnp.where(row_in_batch >= d, pltpu.roll(h, d, axis=0), 0.0)
            ar = aux_ref[k, g:g + 1, :]
            ai = aux_ref[nst + k, g:g + 1, :]
            h = h + ar * hs + ai * pltpu.roll(hs, STATE, axis=1)
        h_in = jnp.where(row_in_batch >= 1, pltpu.roll(h, 1, axis=0), 0.0)
        y = jnp.dot(u, t_ref[g], preferred_element_type=F32)
        y = y + lax.dot_general(h_in.astype(BF16), ot_ref[g], (((1,), (1,)), ((), ())),
                                preferred_element_type=F32)
        y = y + d_ref[g] * u.astype(F32)
        g_scr[:, g * CW:(g + 1) * CW] = _gelu_tanh(y).astype(g_scr.dtype)
    outp = lax.dot_general(g_scr[...], p_ref[...], (((1,), (1,)), ((), ())), preferred_element_type=F32)
    for s in range(CHUNK):
        o_ref[s] = outp[:, s * LANES:(s + 1) * LANES].astype(o_ref.dtype)


def _regroup_matrix():
    n = CHUNK * LANES
    i = lax.broadcasted_iota(jnp.int32, (n, n), 0)
    j = lax.broadcasted_iota(jnp.int32, (n, n), 1)
    s, g, c = i // LANES, (i % LANES) // SSM_GROUP, i % SSM_GROUP
    return (j == g * CW + s * SSM_GROUP + c).astype(BF16)


def _s5_main(proj, t_op, r_op, ot_op, aux, d_tiled, nst, npb, nc, d_ssm):
    n_in = proj.shape[1]
    g = d_ssm // SSM_GROUP
    proj_v = proj.reshape(nc, CHUNK * n_in)
    cb = n_in // LANES
    x_specs = [pl.BlockSpec((nc, LANES), functools.partial(lambda i, s: (0, s * cb + i), s=s)) for s in range(CHUNK)]
    n = CHUNK * LANES
    return pl.pallas_call(
        functools.partial(_s5_main_kernel, nst=nst, npb=npb),
        grid=(g // GROUPS_PER_STEP,),
        in_specs=x_specs + [
            pl.BlockSpec((n, n), lambda i: (0, 0)),
            pl.BlockSpec((GROUPS_PER_STEP, CW, CW), lambda i: (i, 0, 0)),
            pl.BlockSpec((GROUPS_PER_STEP, CW, SW), lambda i: (i, 0, 0)),
            pl.BlockSpec((GROUPS_PER_STEP, CW, SW), lambda i: (i, 0, 0)),
            pl.BlockSpec((2 * nst, GROUPS_PER_STEP, SW), lambda i: (0, i, 0)),
            pl.BlockSpec((GROUPS_PER_STEP, 1, CW), lambda i: (i, 0, 0)),
        ],
        out_specs=pl.BlockSpec((CHUNK, nc, LANES), lambda i: (0, 0, i)),
        out_shape=jax.ShapeDtypeStruct((CHUNK, nc, d_ssm), BF16),
        scratch_shapes=[pltpu.VMEM((nc, n), BF16)],
        compiler_params=_params(("arbitrary",)),
        name="s5_main",
    )(*([proj_v] * CHUNK), _regroup_matrix(), t_op, r_op, ot_op, aux, d_tiled)


def _glu_kernel(a_ref, w_ref, g_ref, z_ref, o_ref):
    acc = jnp.dot(a_ref[0], w_ref[...], preferred_element_type=F32)
    g = g_ref[0].astype(F32)
    o_ref[0] = (g * _sigmoid(acc) * z_ref[...].astype(F32)).astype(o_ref.dtype)


def _glu(g_sm, w_glu, proj, d_ssm):
    _, nc, k = g_sm.shape
    n_in = proj.shape[1]
    proj_v = proj.reshape(nc, CHUNK * n_in)
    tm = _tile(nc, 1024)
    tn = _tile(d_ssm, 1024)
    cb = n_in // tn
    zoff = d_ssm // tn
    return pl.pallas_call(
        _glu_kernel,
        grid=(CHUNK, nc // tm, d_ssm // tn),
        in_specs=[
            pl.BlockSpec((1, tm, k), lambda s, i, j: (s, i, 0)),
            pl.BlockSpec((k, tn), lambda s, i, j: (0, j)),
            pl.BlockSpec((1, tm, tn), lambda s, i, j: (s, i, j)),
            pl.BlockSpec((tm, tn), lambda s, i, j: (i, s * cb + zoff + j)),
        ],
        out_specs=pl.BlockSpec((1, tm, tn), lambda s, i, j: (s, i, j)),
        out_shape=jax.ShapeDtypeStruct((CHUNK, nc, d_ssm), BF16),
        compiler_params=_params(("arbitrary", "arbitrary", "arbitrary")),
        name="glu",
    )(g_sm, w_glu, g_sm, proj_v)


def _bias_kernel(tab_ref, o_ref):
    h = pl.program_id(0)
    blk = o_ref.shape[-1]
    r = lax.broadcasted_iota(jnp.int32, (blk, blk), 0)
    c = lax.broadcasted_iota(jnp.int32, (blk, blk), 1)
    max_exact = N_BUCKETS // 2
    far = tab_ref[h * N_BUCKETS + N_BUCKETS - 1]
    for t in range(2):
        rel = r - c + t * blk
        n = jnp.maximum(rel, 0)
        nf = jnp.maximum(n, 1).astype(F32)
        large = max_exact + (jnp.log(nf / max_exact) / math.log(MAX_DISTANCE / max_exact)
                             * (N_BUCKETS - max_exact)).astype(jnp.int32)
        large = jnp.minimum(large, N_BUCKETS - 1)
        bucket = jnp.where(n < max_exact, n, large)
        val = jnp.zeros((blk, blk), F32)
        for b in range(N_BUCKETS):
            val = jnp.where(bucket == b, tab_ref[h * N_BUCKETS + b], val)
        o_ref[0, t] = jnp.where(rel >= 0, (val - far) * LOG2E, NEG_INF)


def _bias_tiles(rel_bias, n_heads, blk):
    tab = jnp.transpose(rel_bias.astype(F32)).reshape(-1)
    return pl.pallas_call(
        _bias_kernel,
        grid=(n_heads,),
        in_specs=[pl.BlockSpec(memory_space=pltpu.SMEM)],
        out_specs=pl.BlockSpec((1, 2, blk, blk), lambda h: (h, 0, 0, 0)),
        out_shape=jax.ShapeDtypeStruct((n_heads, 2, blk, blk), F32),
        compiler_params=_params(("arbitrary",)),
        name="t5_bias_tiles",
    )(tab)


def _attn_kernel(q_ref, k_ref, v_ref, z_ref, bias_ref, lamv_ref, sg_ref, o_ref,
                 m_scr, l_scr, acc_scr, *, lam_init):
    qi = pl.program_id(2)
    blk = q_ref.shape[0]
    q = q_ref[...]
    qs = (q[:, :HEAD_DIM], q[:, HEAD_DIM:])

    m_scr[...] = jnp.full(m_scr.shape, -jnp.inf, F32)
    l_scr[...] = jnp.zeros(l_scr.shape, F32)
    acc_scr[...] = jnp.zeros(acc_scr.shape, F32)

    nlb = blk // LANES

    def step(j, bias):
        off = pl.multiple_of(j * blk, blk)
        k = k_ref[pl.ds(off, blk), :]
        v = v_ref[pl.ds(off, blk), :]
        for mp in range(2):
            s = lax.dot_general(qs[mp], k[:, mp * HEAD_DIM:(mp + 1) * HEAD_DIM],
                                (((1,), (1,)), ((), ())), preferred_element_type=F32)
            if bias is not None:
                s = s + bias
            m_prev = m_scr[mp]
            m_new = jnp.maximum(m_prev, jnp.max(s, axis=-1, keepdims=True))
            alpha = jnp.exp2(m_prev - m_new)
            p = jnp.exp2(s - jnp.concatenate([m_new] * nlb, axis=1))
            psum = p[:, :LANES]
            for t in range(1, nlb):
                psum = psum + p[:, t * LANES:(t + 1) * LANES]
            l_scr[mp] = alpha * l_scr[mp] + psum
            acc_scr[mp] = (jnp.concatenate([alpha] * (2 * HEAD_DIM // LANES), axis=1) * acc_scr[mp]
                           + jnp.dot(p.astype(BF16), v, preferred_element_type=F32))
            m_scr[mp] = m_new

    def far_body(j, carry):
        step(j, None)
        return carry

    lax.fori_loop(0, jnp.maximum(qi - 1, 0), far_body, 0)

    @pl.when(qi >= 1)
    def _():
        step(qi - 1, bias_ref[0, 1])

    step(qi, bias_ref[0, 0])

    lv = lamv_ref[...]
    lam = (jnp.exp(jnp.sum(lv[0:1] * lv[1:2], axis=-1, keepdims=True))
           - jnp.exp(jnp.sum(lv[2:3] * lv[3:4], axis=-1, keepdims=True)) + lam_init)
    l0 = jnp.sum(l_scr[0], axis=-1, keepdims=True)
    l1 = jnp.sum(l_scr[1], axis=-1, keepdims=True)
    o = acc_scr[0] / l0 - lam * (acc_scr[1] / l1)
    ms = jnp.mean(o * o, axis=-1, keepdims=True)
    o = o * lax.rsqrt(ms + SUBLN_EPS) * sg_ref[...] * (1.0 - lam_init)
    o_ref[...] = (o * z_ref[...].astype(F32)).astype(o_ref.dtype)


def _attention(proj, bias_tiles, lamv, subln_g, bsz, seq, d_attn, lam_init):
    hw = 2 * HEAD_DIM
    n_heads = d_attn // hw
    blk = _tile(seq, ATT_BLOCK)
    nq = seq // blk
    qoff, koff, voff, zoff = (2 * d_attn // hw, 3 * d_attn // hw, 4 * d_attn // hw, 5 * d_attn // hw)
    return pl.pallas_call(
        functools.partial(_attn_kernel, lam_init=lam_init),
        grid=(bsz, n_heads, nq),
        in_specs=[
            pl.BlockSpec((blk, hw), lambda b, h, i: (b * nq + i, qoff + h)),
            pl.BlockSpec((seq, hw), lambda b, h, i: (b, koff + h)),
            pl.BlockSpec((seq, hw), lambda b, h, i: (b, voff + h)),
            pl.BlockSpec((blk, hw), lambda b, h, i: (b * nq + i, zoff + h)),
            pl.BlockSpec((1, 2, blk, blk), lambda b, h, i: (h, 0, 0, 0)),
            pl.BlockSpec((4, HEAD_DIM), lambda b, h, i: (0, 0)),
            pl.BlockSpec((1, hw), lambda b, h, i: (0, 0)),
        ],
        out_specs=pl.BlockSpec((blk, hw), lambda b, h, i: (b * nq + i, h)),
        out_shape=jax.ShapeDtypeStruct((bsz * seq, d_attn), BF16),
        scratch_shapes=[
            pltpu.VMEM((2, blk, LANES), F32),
            pltpu.VMEM((2, blk, LANES), F32),
            pltpu.VMEM((2, blk, hw), F32),
        ],
        compiler_params=_params(("arbitrary", "arbitrary", "arbitrary")),
        name="diff_attention",
    )(proj, proj, proj, proj, bias_tiles, lamv, subln_g.reshape(1, hw))


def _merge_kernel(ys_ref, ya_ref, ws_ref, wa_ref, gs_ref, ga_ref, o_ref):
    ps = jnp.dot(ys_ref[0], ws_ref[...], preferred_element_type=F32)
    pa = jnp.dot(ya_ref[...], wa_ref[...], preferred_element_type=F32)
    o_ref[...] = (gs_ref[...].astype(F32) * ps + ga_ref[...].astype(F32) * pa).astype(o_ref.dtype)


def _out_merge(y_s, y_a, w_s, w_a, proj, d_ssm, d_attn):
    _, nc, ks = y_s.shape
    ka = y_a.shape[1]
    d = w_s.shape[1]
    n_in = proj.shape[1]
    tm = _tile(nc, 1024)
    tn = _tile(d, 1024)
    cb = n_in // tn
    gs_off = (2 * d_ssm + 4 * d_attn) // tn
    ga_off = gs_off + d // tn
    proj_v = proj.reshape(nc, CHUNK * n_in)
    out = pl.pallas_call(
        _merge_kernel,
        grid=(CHUNK, nc // tm, d // tn),
        in_specs=[
            pl.BlockSpec((1, tm, ks), lambda s, i, j: (s, i, 0)),
            pl.BlockSpec((tm, ka), lambda s, i, j: (i, s)),
            pl.BlockSpec((ks, tn), lambda s, i, j: (0, j)),
            pl.BlockSpec((ka, tn), lambda s, i, j: (0, j)),
            pl.BlockSpec((tm, tn), lambda s, i, j: (i, s * cb + gs_off + j)),
            pl.BlockSpec((tm, tn), lambda s, i, j: (i, s * cb + ga_off + j)),
        ],
        out_specs=pl.BlockSpec((tm, tn), lambda s, i, j: (i, s * (d // tn) + j)),
        out_shape=jax.ShapeDtypeStruct((nc, CHUNK * d), BF16),
        compiler_params=_params(("arbitrary", "arbitrary", "arbitrary")),
        name="out_merge",
    )(y_s, y_a.reshape(nc, CHUNK * ka), w_s, w_a, proj_v, proj_v)
    return out.reshape(nc * CHUNK, d)


def _resid_kernel(a_ref, w_ref, h_ref, m_ref, o_ref):
    acc = jnp.dot(a_ref[...], w_ref[...], preferred_element_type=F32)
    o_ref[...] = h_ref[...] + m_ref[0, 2:3, :] * acc


def _resid(merged, w_o, h_res2, mod3, bsz):
    m, k = merged.shape
    d = w_o.shape[1]
    seq = m // bsz
    tm = _tile(seq, 1024)
    tn = _tile(d, 1024)
    ns = seq // tm
    return pl.pallas_call(
        _resid_kernel,
        grid=(bsz, ns, d // tn),
        in_specs=[
            pl.BlockSpec((tm, k), lambda b, i, j: (b * ns + i, 0)),
            pl.BlockSpec((k, tn), lambda b, i, j: (0, j)),
            pl.BlockSpec((tm, tn), lambda b, i, j: (b * ns + i, j)),
            pl.BlockSpec((1, 3, tn), lambda b, i, j: (b, 0, j)),
        ],
        out_specs=pl.BlockSpec((tm, tn), lambda b, i, j: (b * ns + i, j)),
        out_shape=jax.ShapeDtypeStruct((m, d), F32),
        input_output_aliases={2: 0},
        compiler_params=_params(("arbitrary", "arbitrary", "arbitrary")),
        name="resid",
    )(merged, w_o, h_res2, mod3)


def _final_norm_kernel(x_ref, g_ref, o_ref):
    x = x_ref[...]
    ms = jnp.mean(x * x, axis=-1, keepdims=True)
    o_ref[...] = (x * lax.rsqrt(ms + EPS) * g_ref[...]).astype(o_ref.dtype)


def _final_norm(h2, g, out_dtype):
    m, d = h2.shape
    tm = _tile(m, 512)
    return pl.pallas_call(
        _final_norm_kernel,
        grid=(m // tm,),
        in_specs=[pl.BlockSpec((tm, d), lambda i: (i, 0)), pl.BlockSpec((1, d), lambda i: (0, 0))],
        out_specs=pl.BlockSpec((tm, d), lambda i: (i, 0)),
        out_shape=jax.ShapeDtypeStruct((m, d), out_dtype),
        compiler_params=_params(("arbitrary",)),
        name="final_norm",
    )(h2, g.reshape(1, d))


def kernel(x, c, norm_g, w_ada, b_ada, w_in, ssm_lambda_re, ssm_lambda_im, ssm_log_dt, ssm_b_re, ssm_b_im, ssm_c_re, ssm_c_im, ssm_d, w_glu, lambda_q1, lambda_k1, lambda_q2, lambda_k2, subln_g, w_out_ssm, w_out_attn, w_o, rel_bias, final_g):
    out_dtype = x.dtype
    bsz, seq, d = x.shape
    depth = w_in.shape[0]
    d_ssm = w_glu.shape[1]
    d_attn = w_out_attn.shape[1]
    n_groups = d_ssm // SSM_GROUP
    n_heads = d_attn // (2 * HEAD_DIM)
    npb = seq // CHUNK
    nst = npb.bit_length() - 1
    assert seq % CHUNK == 0 and (1 << nst) == npb, "sequence must be CHUNK * 2^k"
    nc = bsz * npb

    c_pad = jnp.zeros((8, d), F32).at[:bsz].set(c.astype(F32))
    mod = _ada_mod(c_pad, w_ada.astype(F32), b_ada.astype(F32))
    bias_tiles = _bias_tiles(rel_bias, n_heads, _tile(seq, ATT_BLOCK))

    h_res = x.astype(F32)
    for l in range(depth):
        mod3 = mod[l, :bsz].reshape(bsz, 3, d)
        hn = _prenorm(h_res, norm_g[l].astype(F32), mod3)
        proj = _in_proj(hn.reshape(bsz * seq, d), w_in[l].astype(BF16), d_ssm)

        r_op, ot_op, t_op, aux = _s5_prep(
            ssm_lambda_re[l].astype(F32), ssm_lambda_im[l].astype(F32), ssm_log_dt[l].astype(F32),
            ssm_b_re[l].astype(F32), ssm_b_im[l].astype(F32), ssm_c_re[l].astype(F32), ssm_c_im[l].astype(F32), nst)
        d_tiled = jnp.tile(ssm_d[l].astype(F32), (1, CHUNK)).reshape(n_groups, 1, CW)
        g_sm = _s5_main(proj, t_op, r_op, ot_op, aux, d_tiled, nst, npb, nc, d_ssm)
        y_s = _glu(g_sm, w_glu[l].astype(BF16), proj, d_ssm)

        lam_init = 0.8 - 0.6 * math.exp(-0.3 * l)
        lamv = jnp.stack([lambda_q1[l], lambda_k1[l], lambda_q2[l], lambda_k2[l]]).astype(F32)
        y_a = _attention(proj, bias_tiles, lamv, subln_g[l].astype(F32), bsz, seq, d_attn, lam_init)

        merged = _out_merge(y_s, y_a, w_out_ssm[l].astype(BF16), w_out_attn[l].astype(BF16), proj, d_ssm, d_attn)
        h_res = _resid(merged, w_o[l].astype(BF16), h_res.reshape(bsz * seq, d), mod3, bsz).reshape(bsz, seq, d)

    out = _final_norm(h_res.reshape(bsz * seq, d), final_g.astype(F32), out_dtype)
    return out.reshape(bsz, seq, d)
```

```python
import functools
import math

import jax
import jax.numpy as jnp
from jax import lax
from jax.experimental import pallas as pl
from jax.experimental.pallas import tpu as pltpu

F32 = jnp.float32
BF16 = jnp.bfloat16

SSM_GROUP = 16
STATE = 64
HEAD_DIM = 128
N_BUCKETS = 32
MAX_DISTANCE = 128
EPS = 1e-6
SUBLN_EPS = 1e-5
NEG_INF = -1e30

LANES = 128
CHUNK = 16
SW = 2 * STATE
GROUPS_PER_STEP = LANES // SSM_GROUP
XW = CHUNK * LANES
HW = GROUPS_PER_STEP * SW
ATT_BLOCK = 512
LOG2E = math.log2(math.e)
V7X_VMEM_LIMIT = 56 * 1024 * 1024


def _sigmoid(x):
    return 1.0 / (1.0 + jnp.exp(-x))


def _silu(x):
    return x * _sigmoid(x)


def _gelu_tanh(x):
    return 0.5 * x * (1.0 + jnp.tanh(math.sqrt(2.0 / math.pi) * (x + 0.044715 * (x * x * x))))


def _tile(dim, pref):
    t = min(dim, pref)
    assert dim % t == 0, (dim, pref)
    return t


def _params(sem):
    return pltpu.CompilerParams(dimension_semantics=sem, vmem_limit_bytes=V7X_VMEM_LIMIT)


def _ada_kernel(c_ref, w_ref, b_ref, o_ref):
    c = c_ref[...]
    ca = _silu(c).astype(BF16)
    o_ref[0] = jnp.dot(ca, w_ref[0].astype(BF16), preferred_element_type=F32) + b_ref[0]


def _ada_mod(c_pad, w_ada, b_ada):
    n_layers, d, n3 = w_ada.shape
    rows = c_pad.shape[0]
    tn = _tile(n3, 1024)
    return pl.pallas_call(
        _ada_kernel,
        grid=(n_layers, n3 // tn),
        in_specs=[
            pl.BlockSpec((rows, d), lambda l, j: (0, 0)),
            pl.BlockSpec((1, d, tn), lambda l, j: (l, 0, j)),
            pl.BlockSpec((1, 1, tn), lambda l, j: (l, 0, j)),
        ],
        out_specs=pl.BlockSpec((1, rows, tn), lambda l, j: (l, 0, j)),
        out_shape=jax.ShapeDtypeStruct((n_layers, rows, n3), F32),
        compiler_params=_params(("arbitrary", "arbitrary")),
        name="ada_mod",
    )(c_pad, w_ada, b_ada.reshape(n_layers, 1, n3))


def _prenorm_kernel(x_ref, g_ref, m_ref, o_ref):
    x = x_ref[0]
    ms = jnp.mean(x * x, axis=-1, keepdims=True)
    shift = m_ref[0, 0:1, :]
    scale = m_ref[0, 1:2, :]
    h = x * lax.rsqrt(ms + EPS) * g_ref[...]
    o_ref[0] = (h * (1.0 + scale) + shift).astype(o_ref.dtype)


def _prenorm(h_res, g, mod3):
    b, s, d = h_res.shape
    ts = _tile(s, 512)
    return pl.pallas_call(
        _prenorm_kernel,
        grid=(b, s // ts),
        in_specs=[
            pl.BlockSpec((1, ts, d), lambda i, j: (i, j, 0)),
            pl.BlockSpec((1, d), lambda i, j: (0, 0)),
            pl.BlockSpec((1, 3, d), lambda i, j: (i, 0, 0)),
        ],
        out_specs=pl.BlockSpec((1, ts, d), lambda i, j: (i, j, 0)),
        out_shape=jax.ShapeDtypeStruct((b, s, d), BF16),
        compiler_params=_params(("arbitrary", "arbitrary")),
        name="prenorm",
    )(h_res, g.reshape(1, d), mod3)


def _inproj_kernel(a_ref, w_ref, o_ref, wb_scr, *, seg_tiles):
    j = pl.program_id(0)

    @pl.when(pl.program_id(1) == 0)
    def _():
        wb_scr[...] = w_ref[0].astype(BF16)

    acc = jnp.dot(a_ref[...], wb_scr[...], preferred_element_type=F32)
    st = seg_tiles
    is_silu = jnp.logical_or(jnp.logical_and(j >= st, j < 2 * st), jnp.logical_and(j >= 5 * st, j < 6 * st))
    is_q = jnp.logical_and(j >= 2 * st, j < 3 * st)
    is_sig = j >= 6 * st
    is_id = jnp.logical_or(j < st, jnp.logical_and(j >= 3 * st, j < 5 * st))

    @pl.when(is_id)
    def _():
        o_ref[...] = acc.astype(o_ref.dtype)

    @pl.when(is_q)
    def _():
        o_ref[...] = (acc * (HEAD_DIM ** -0.5 * LOG2E)).astype(o_ref.dtype)

    @pl.when(is_silu)
    def _():
        o_ref[...] = _silu(acc).astype(o_ref.dtype)

    @pl.when(is_sig)
    def _():
        o_ref[...] = _sigmoid(acc).astype(o_ref.dtype)


def _in_proj(a, w_all, layer, d_ssm):
    m, k = a.shape
    n = w_all.shape[2]
    tm = _tile(m, 1024)
    tn = _tile(d_ssm, 512)
    return pl.pallas_call(
        functools.partial(_inproj_kernel, seg_tiles=d_ssm // tn),
        grid=(n // tn, m // tm),
        in_specs=[
            pl.BlockSpec((tm, k), lambda j, i: (i, 0)),
            pl.BlockSpec((1, k, tn), lambda j, i: (layer, 0, j)),
        ],
        out_specs=pl.BlockSpec((tm, tn), lambda j, i: (i, j)),
        out_shape=jax.ShapeDtypeStruct((m, n), BF16),
        scratch_shapes=[pltpu.VMEM((k, tn), BF16)],
        compiler_params=_params(("arbitrary", "arbitrary")),
        name="in_proj",
    )(a, w_all)


def _s5_kernel(x_ref, lr_ref, li_ref, ldt_ref, ba_ref, bb_ref, ca_ref, cb_ref, d_ref, o_ref,
               t_scr, r_scr, ot_scr, xs_scr, os_scr, *, nst, npb):
    nc = x_ref.shape[0] // CHUNK
    gps = GROUPS_PER_STEP

    @pl.when(pl.program_id(0) == 0)
    def _():
        t_scr[...] = jnp.zeros(t_scr.shape, t_scr.dtype)

    lr = lr_ref[...]
    li = li_ref[...]
    dt = jnp.exp(ldt_ref[...])
    mag = jnp.exp(lr * dt)
    are = mag * jnp.cos(li * dt)
    aim = mag * jnp.sin(li * dt)
    den = lr * lr + li * li
    nre = are - 1.0
    cre = (nre * lr + aim * li) / den
    cim = (aim * lr - nre * li) / den
    b_a = ba_ref[...]
    b_b = bb_ref[...]
    c_a = ca_ref[...]
    c_b = cb_ref[...]
    cre3 = cre[:, None, :]
    cim3 = cim[:, None, :]
    bbv = cre3 * b_a + cim3 * b_b
    bbs = cre3 * b_b - cim3 * b_a

    pre = jnp.ones_like(are)
    pim = jnp.zeros_like(are)
    pows = [(pre, pim)]
    for _ in range(CHUNK):
        pre, pim = are * pre - aim * pim, are * pim + aim * pre
        pows.append((pre, pim))

    same_group = (lax.broadcasted_iota(jnp.int32, (LANES, LANES), 0) // SSM_GROUP
                  == lax.broadcasted_iota(jnp.int32, (LANES, LANES), 1) // SSM_GROUP)
    own_state = (lax.broadcasted_iota(jnp.int32, (LANES, HW), 0) // SSM_GROUP
                 == lax.broadcasted_iota(jnp.int32, (LANES, HW), 1) // SW)
    bbv2 = bbv.reshape(LANES, SW)
    for k in range(CHUNK + 1):
        pr, pi = pows[k]
        z_k = (pr[:, None, :] * c_a + pi[:, None, :] * c_b).reshape(LANES, SW)
        if k < CHUNK:
            kk = lax.dot_general(bbv2, z_k, (((1,), (1,)), ((), ())),
                                 precision=lax.Precision.HIGHEST, preferred_element_type=F32)
            kk = jnp.where(same_group, kk, 0.0).astype(t_scr.dtype)
            for s in range(CHUNK - k):
                t = s + k
                t_scr[s * LANES:(s + 1) * LANES, t * LANES:(t + 1) * LANES] = kk
        if k >= 1:
            ot_scr[(k - 1) * LANES:k * LANES, :] = jnp.where(
                own_state, jnp.concatenate([z_k] * gps, axis=1), 0.0).astype(ot_scr.dtype)
    for s in range(CHUNK):
        pr, pi = pows[CHUNK - 1 - s]
        r_s = (pr[:, None, :] * bbv + pi[:, None, :] * bbs).reshape(LANES, SW)
        r_scr[s * LANES:(s + 1) * LANES, :] = jnp.where(
            own_state, jnp.concatenate([r_s] * gps, axis=1), 0.0).astype(r_scr.dtype)

    first_half = lax.broadcasted_iota(jnp.int32, are.shape, 1) < STATE
    sre, sim = pows[CHUNK]
    steps = []
    for _ in range(nst):
        steps.append((sre, jnp.where(first_half, -sim, sim)))
        sre, sim = sre * sre - sim * sim, 2.0 * sre * sim

    xs_scr[...] = x_ref[...].astype(F32)
    xcat = jnp.concatenate([xs_scr[pl.ds(s, nc, stride=CHUNK), :].astype(BF16) for s in range(CHUNK)], axis=1)

    row_in_batch = jnp.bitwise_and(lax.broadcasted_iota(jnp.int32, (nc, SW), 0), npb - 1)
    r = jnp.dot(xcat, r_scr[...], preferred_element_type=F32)
    h_ins = []
    for g in range(gps):
        h = r[:, g * SW:(g + 1) * SW]
        for k in range(nst):
            d = 1 << k
            hs = jnp.where(row_in_batch >= d, pltpu.roll(h, d, axis=0), 0.0)
            ar, ai = steps[k]
            h = h + ar[g:g + 1, :] * hs + ai[g:g + 1, :] * pltpu.roll(hs, STATE, axis=1)
        h_in = jnp.where(row_in_batch >= 1, pltpu.roll(h, 1, axis=0), 0.0)
        h_ins.append(h_in.astype(BF16))
    hcat = jnp.concatenate(h_ins, axis=1)

    y = jnp.dot(xcat, t_scr[...], preferred_element_type=F32)
    y = y + lax.dot_general(hcat, ot_scr[...], (((1,), (1,)), ((), ())), preferred_element_type=F32)
    y = y + d_ref[0] * xcat.astype(F32)
    gact = _gelu_tanh(y)
    for s in range(CHUNK):
        os_scr[pl.ds(s, nc, stride=CHUNK), :] = gact[:, s * LANES:(s + 1) * LANES]
    o_ref[...] = os_scr[...].astype(o_ref.dtype)


def _s5(proj, lam_re, lam_im, log_dt, b_re, b_im, c_re, c_im, d_skip, nst, npb, d_ssm):
    rows = proj.shape[0]
    g = d_ssm // SSM_GROUP
    gps = GROUPS_PER_STEP
    lr2 = jnp.concatenate([lam_re, lam_re], axis=-1)
    li2 = jnp.concatenate([lam_im, lam_im], axis=-1)
    ldt2 = jnp.broadcast_to(log_dt[:, None], (g, SW))
    brt = jnp.swapaxes(b_re, 1, 2)
    bit = jnp.swapaxes(b_im, 1, 2)
    b_a = jnp.concatenate([brt, bit], axis=-1)
    b_b = jnp.concatenate([-bit, brt], axis=-1)
    c_a = jnp.concatenate([c_re, -c_im], axis=-1)
    c_b = jnp.concatenate([-c_im, -c_re], axis=-1)
    d_lane = jnp.tile(d_skip.reshape(g // gps, 1, LANES), (1, 1, CHUNK))
    vec = pl.BlockSpec((gps, SW), lambda i: (i, 0))
    mat = pl.BlockSpec((gps, SSM_GROUP, SW), lambda i: (i, 0, 0))
    return pl.pallas_call(
        functools.partial(_s5_kernel, nst=nst, npb=npb),
        grid=(g // gps,),
        in_specs=[pl.BlockSpec((rows, LANES), lambda i: (0, i)), vec, vec, vec, mat, mat, mat, mat,
                  pl.BlockSpec((1, 1, XW), lambda i: (i, 0, 0))],
        out_specs=pl.BlockSpec((rows, LANES), lambda i: (0, i)),
        out_shape=jax.ShapeDtypeStruct((rows, d_ssm), BF16),
        scratch_shapes=[
            pltpu.VMEM((XW, XW), BF16),
            pltpu.VMEM((XW, HW), BF16),
            pltpu.VMEM((XW, HW), BF16),
            pltpu.VMEM((rows, LANES), F32),
            pltpu.VMEM((rows, LANES), F32),
        ],
        compiler_params=_params(("arbitrary",)),
        name="s5",
    )(proj, lr2, li2, ldt2, b_a, b_b, c_a, c_b, d_lane)


def _glu_kernel(a_ref, w_ref, g_ref, z_ref, o_ref):
    acc = jnp.dot(a_ref[...], w_ref[...], preferred_element_type=F32)
    g = g_ref[...].astype(F32)
    o_ref[...] = (g * _sigmoid(acc) * z_ref[...].astype(F32)).astype(o_ref.dtype)


def _glu(g_act, w_glu, proj, d_ssm):
    m, k = g_act.shape
    tm = _tile(m, 1024)
    tn = _tile(d_ssm, 1024)
    zoff = d_ssm // tn
    return pl.pallas_call(
        _glu_kernel,
        grid=(m // tm, d_ssm // tn),
        in_specs=[
            pl.BlockSpec((tm, k), lambda i, j: (i, 0)),
            pl.BlockSpec((k, tn), lambda i, j: (0, j)),
            pl.BlockSpec((tm, tn), lambda i, j: (i, j)),
            pl.BlockSpec((tm, tn), lambda i, j: (i, zoff + j)),
        ],
        out_specs=pl.BlockSpec((tm, tn), lambda i, j: (i, j)),
        out_shape=jax.ShapeDtypeStruct((m, d_ssm), BF16),
        compiler_params=_params(("arbitrary", "arbitrary")),
        name="glu",
    )(g_act, w_glu, g_act, proj)


def _bias_kernel(tab_ref, o_ref):
    h = pl.program_id(0)
    blk = o_ref.shape[-1]
    r = lax.broadcasted_iota(jnp.int32, (blk, blk), 0)
    c = lax.broadcasted_iota(jnp.int32, (blk, blk), 1)
    max_exact = N_BUCKETS // 2
    far = tab_ref[h * N_BUCKETS + N_BUCKETS - 1]
    for t in range(2):
        rel = r - c + t * blk
        n = jnp.maximum(rel, 0)
        nf = jnp.maximum(n, 1).astype(F32)
        large = max_exact + (jnp.log(nf / max_exact) / math.log(MAX_DISTANCE / max_exact)
                             * (N_BUCKETS - max_exact)).astype(jnp.int32)
        large = jnp.minimum(large, N_BUCKETS - 1)
        bucket = jnp.where(n < max_exact, n, large)
        val = jnp.zeros((blk, blk), F32)
        for b in range(N_BUCKETS):
            val = jnp.where(bucket == b, tab_ref[h * N_BUCKETS + b], val)
        o_ref[0, t] = jnp.where(rel >= 0, (val - far) * LOG2E, NEG_INF)


def _bias_tiles(rel_bias, n_heads, blk):
    tab = jnp.transpose(rel_bias.astype(F32)).reshape(-1)
    return pl.pallas_call(
        _bias_kernel,
        grid=(n_heads,),
        in_specs=[pl.BlockSpec(memory_space=pltpu.SMEM)],
        out_specs=pl.BlockSpec((1, 2, blk, blk), lambda h: (h, 0, 0, 0)),
        out_shape=jax.ShapeDtypeStruct((n_heads, 2, blk, blk), F32),
        compiler_params=_params(("arbitrary",)),
        name="t5_bias_tiles",
    )(tab)


def _attn_kernel(q_ref, k_ref, v_ref, z_ref, bias_ref, lamv_ref, sg_ref, o_ref,
                 m_scr, l_scr, acc_scr, *, lam_init):
    qi = pl.program_id(2)
    blk = q_ref.shape[0]
    q = q_ref[...]
    qs = (q[:, :HEAD_DIM], q[:, HEAD_DIM:])

    m_scr[...] = jnp.full(m_scr.shape, -jnp.inf, F32)
    l_scr[...] = jnp.zeros(l_scr.shape, F32)
    acc_scr[...] = jnp.zeros(acc_scr.shape, F32)

    nlb = blk // LANES

    def step(j, bias):
        off = pl.multiple_of(j * blk, blk)
        k = k_ref[pl.ds(off, blk), :]
        v = v_ref[pl.ds(off, blk), :]
        for mp in range(2):
            s = lax.dot_general(qs[mp], k[:, mp * HEAD_DIM:(mp + 1) * HEAD_DIM],
                                (((1,), (1,)), ((), ())), preferred_element_type=F32)
            if bias is not None:
                s = s + bias
            m_prev = m_scr[mp]
            m_new = jnp.maximum(m_prev, jnp.max(s, axis=-1, keepdims=True))
            alpha = jnp.exp2(m_prev - m_new)
            p = jnp.exp2(s - jnp.concatenate([m_new] * nlb, axis=1))
            psum = p[:, :LANES]
            for t in range(1, nlb):
                psum = psum + p[:, t * LANES:(t + 1) * LANES]
            l_scr[mp] = alpha * l_scr[mp] + psum
            acc_scr[mp] = (jnp.concatenate([alpha] * (2 * HEAD_DIM // LANES), axis=1) * acc_scr[mp]
                           + jnp.dot(p.astype(BF16), v, preferred_element_type=F32))
            m_scr[mp] = m_new

    def far_body(j, carry):
        step(j, None)
        return carry

    lax.fori_loop(0, jnp.maximum(qi - 1, 0), far_body, 0)

    @pl.when(qi >= 1)
    def _():
        step(qi - 1, bias_ref[0, 1])

    step(qi, bias_ref[0, 0])

    lv = lamv_ref[...]
    lam = (jnp.exp(jnp.sum(lv[0:1] * lv[1:2], axis=-1, keepdims=True))
           - jnp.exp(jnp.sum(lv[2:3] * lv[3:4], axis=-1, keepdims=True)) + lam_init)
    l0 = jnp.sum(l_scr[0], axis=-1, keepdims=True)
    l1 = jnp.sum(l_scr[1], axis=-1, keepdims=True)
    o = acc_scr[0] / l0 - lam * (acc_scr[1] / l1)
    ms = jnp.mean(o * o, axis=-1, keepdims=True)
    o = o * lax.rsqrt(ms + SUBLN_EPS) * sg_ref[...] * (1.0 - lam_init)
    o_ref[...] = (o * z_ref[...].astype(F32)).astype(o_ref.dtype)


def _attention(proj, bias_tiles, lamv, subln_g, bsz, seq, d_attn, lam_init):
    hw = 2 * HEAD_DIM
    n_heads = d_attn // hw
    blk = _tile(seq, ATT_BLOCK)
    nq = seq // blk
    qoff, koff, voff, zoff = (2 * d_attn // hw, 3 * d_attn // hw, 4 * d_attn // hw, 5 * d_attn // hw)
    return pl.pallas_call(
        functools.partial(_attn_kernel, lam_init=lam_init),
        grid=(bsz, n_heads, nq),
        in_specs=[
            pl.BlockSpec((blk, hw), lambda b, h, i: (b * nq + i, qoff + h)),
            pl.BlockSpec((seq, hw), lambda b, h, i: (b, koff + h)),
            pl.BlockSpec((seq, hw), lambda b, h, i: (b, voff + h)),
            pl.BlockSpec((blk, hw), lambda b, h, i: (b * nq + i, zoff + h)),
            pl.BlockSpec((1, 2, blk, blk), lambda b, h, i: (h, 0, 0, 0)),
            pl.BlockSpec((4, HEAD_DIM), lambda b, h, i: (0, 0)),
            pl.BlockSpec((1, hw), lambda b, h, i: (0, 0)),
        ],
        out_specs=pl.BlockSpec((blk, hw), lambda b, h, i: (b * nq + i, h)),
        out_shape=jax.ShapeDtypeStruct((bsz * seq, d_attn), BF16),
        scratch_shapes=[
            pltpu.VMEM((2, blk, LANES), F32),
            pltpu.VMEM((2, blk, LANES), F32),
            pltpu.VMEM((2, blk, hw), F32),
        ],
        compiler_params=_params(("arbitrary", "arbitrary", "arbitrary")),
        name="diff_attention",
    )(proj, proj, proj, proj, bias_tiles, lamv, subln_g.reshape(1, hw))


def _merge_kernel(ys_ref, ya_ref, ws_ref, wa_ref, gs_ref, ga_ref, o_ref):
    ps = jnp.dot(ys_ref[...], ws_ref[...], preferred_element_type=F32)
    pa = jnp.dot(ya_ref[...], wa_ref[...], preferred_element_type=F32)
    o_ref[...] = (gs_ref[...].astype(F32) * ps + ga_ref[...].astype(F32) * pa).astype(o_ref.dtype)


def _out_merge(y_s, y_a, w_s, w_a, proj, d_ssm, d_attn):
    m, ks = y_s.shape
    ka = y_a.shape[1]
    d = w_s.shape[1]
    tm = _tile(m, 1024)
    tn = _tile(d, 1024)
    gs_off = (2 * d_ssm + 4 * d_attn) // tn
    ga_off = gs_off + d // tn
    return pl.pallas_call(
        _merge_kernel,
        grid=(m // tm, d // tn),
        in_specs=[
            pl.BlockSpec((tm, ks), lambda i, j: (i, 0)),
            pl.BlockSpec((tm, ka), lambda i, j: (i, 0)),
            pl.BlockSpec((ks, tn), lambda i, j: (0, j)),
            pl.BlockSpec((ka, tn), lambda i, j: (0, j)),
            pl.BlockSpec((tm, tn), lambda i, j: (i, gs_off + j)),
            pl.BlockSpec((tm, tn), lambda i, j: (i, ga_off + j)),
        ],
        out_specs=pl.BlockSpec((tm, tn), lambda i, j: (i, j)),
        out_shape=jax.ShapeDtypeStruct((m, d), BF16),
        compiler_params=_params(("arbitrary", "arbitrary")),
        name="out_merge",
    )(y_s, y_a, w_s, w_a, proj, proj)


def _resid_kernel(a_ref, w_ref, h_ref, m_ref, o_ref):
    acc = jnp.dot(a_ref[...], w_ref[...], preferred_element_type=F32)
    o_ref[...] = h_ref[...] + m_ref[0, 2:3, :] * acc


def _resid(merged, w_o, h_res2, mod3, bsz):
    m, k = merged.shape
    d = w_o.shape[1]
    seq = m // bsz
    tm = _tile(seq, 1024)
    tn = _tile(d, 1024)
    ns = seq // tm
    return pl.pallas_call(
        _resid_kernel,
        grid=(bsz, ns, d // tn),
        in_specs=[
            pl.BlockSpec((tm, k), lambda b, i, j: (b * ns + i, 0)),
            pl.BlockSpec((k, tn), lambda b, i, j: (0, j)),
            pl.BlockSpec((tm, tn), lambda b, i, j: (b * ns + i, j)),
            pl.BlockSpec((1, 3, tn), lambda b, i, j: (b, 0, j)),
        ],
        out_specs=pl.BlockSpec((tm, tn), lambda b, i, j: (b * ns + i, j)),
        out_shape=jax.ShapeDtypeStruct((m, d), F32),
        input_output_aliases={2: 0},
        compiler_params=_params(("arbitrary", "arbitrary", "arbitrary")),
        name="resid",
    )(merged, w_o, h_res2, mod3)


def _final_norm_kernel(x_ref, g_ref, o_ref):
    x = x_ref[...]
    ms = jnp.mean(x * x, axis=-1, keepdims=True)
    o_ref[...] = (x * lax.rsqrt(ms + EPS) * g_ref[...]).astype(o_ref.dtype)


def _final_norm(h2, g, out_dtype):
    m, d = h2.shape
    tm = _tile(m, 512)
    return pl.pallas_call(
        _final_norm_kernel,
        grid=(m // tm,),
        in_specs=[pl.BlockSpec((tm, d), lambda i: (i, 0)), pl.BlockSpec((1, d), lambda i: (0, 0))],
        out_specs=pl.BlockSpec((tm, d), lambda i: (i, 0)),
        out_shape=jax.ShapeDtypeStruct((m, d), out_dtype),
        compiler_params=_params(("arbitrary",)),
        name="final_norm",
    )(h2, g.reshape(1, d))


def kernel(x, c, norm_g, w_ada, b_ada, w_in, ssm_lambda_re, ssm_lambda_im, ssm_log_dt, ssm_b_re, ssm_b_im, ssm_c_re, ssm_c_im, ssm_d, w_glu, lambda_q1, lambda_k1, lambda_q2, lambda_k2, subln_g, w_out_ssm, w_out_attn, w_o, rel_bias, final_g):
    out_dtype = x.dtype
    bsz, seq, d = x.shape
    depth = w_in.shape[0]
    d_ssm = w_glu.shape[1]
    d_attn = w_out_attn.shape[1]
    n_heads = d_attn // (2 * HEAD_DIM)
    npb = seq // CHUNK
    nst = npb.bit_length() - 1
    assert seq % CHUNK == 0 and (1 << nst) == npb, "sequence must be CHUNK * 2^k"
    assert d_ssm % LANES == 0

    c_pad = jnp.zeros((8, d), F32).at[:bsz].set(c.astype(F32))
    mod = _ada_mod(c_pad, w_ada.astype(F32), b_ada.astype(F32))
    bias_tiles = _bias_tiles(rel_bias, n_heads, _tile(seq, ATT_BLOCK))

    h_res = x.astype(F32)
    for l in range(depth):
        mod3 = mod[l, :bsz].reshape(bsz, 3, d)
        hn = _prenorm(h_res, norm_g[l].astype(F32), mod3)
        proj = _in_proj(hn.reshape(bsz * seq, d), w_in.astype(F32), l, d_ssm)

        g_act = _s5(proj, ssm_lambda_re[l].astype(F32), ssm_lambda_im[l].astype(F32), ssm_log_dt[l].astype(F32),
                    ssm_b_re[l].astype(F32), ssm_b_im[l].astype(F32), ssm_c_re[l].astype(F32),
                    ssm_c_im[l].astype(F32), ssm_d[l].astype(F32), nst, npb, d_ssm)
        y_s = _glu(g_act, w_glu[l].astype(BF16), proj, d_ssm)

        lam_init = 0.8 - 0.6 * math.exp(-0.3 * l)
        lamv = jnp.stack([lambda_q1[l], lambda_k1[l], lambda_q2[l], lambda_k2[l]]).astype(F32)
        y_a = _attention(proj, bias_tiles, lamv, subln_g[l].astype(F32), bsz, seq, d_attn, lam_init)

        merged = _out_merge(y_s, y_a, w_out_ssm[l].astype(BF16), w_out_attn[l].astype(BF16), proj, d_ssm, d_attn)
        h_res = _resid(merged, w_o[l].astype(BF16), h_res.reshape(bsz * seq, d), mod3, bsz).reshape(bsz, seq, d)

    out = _final_norm(h_res.reshape(bsz * seq, d), final_g.astype(F32), out_dtype)
    return out.reshape(bsz, seq, d)
```

```python
import functools
import math

import jax
import jax.numpy as jnp
from jax import lax
from jax.experimental import pallas as pl
from jax.experimental.pallas import tpu as pltpu

F32 = jnp.float32
BF16 = jnp.bfloat16

SSM_GROUP = 16
STATE = 64
HEAD_DIM = 128
N_BUCKETS = 32
MAX_DISTANCE = 128
EPS = 1e-6
SUBLN_EPS = 1e-5
NEG_INF = -1e30

LANES = 128
CHUNK = 16
SW = 2 * STATE
GROUPS_PER_STEP = LANES // SSM_GROUP
XW = CHUNK * LANES
HW = GROUPS_PER_STEP * SW
ATT_BLOCK = 512
LOG2E = math.log2(math.e)
V7X_VMEM_LIMIT = 56 * 1024 * 1024


def _sigmoid(x):
    return 1.0 / (1.0 + jnp.exp(-x))


def _silu(x):
    return x * _sigmoid(x)


def _gelu_tanh(x):
    return 0.5 * x * (1.0 + jnp.tanh(math.sqrt(2.0 / math.pi) * (x + 0.044715 * (x * x * x))))


def _tile(dim, pref):
    t = min(dim, pref)
    assert dim % t == 0, (dim, pref)
    return t


def _params(sem):
    return pltpu.CompilerParams(dimension_semantics=sem, vmem_limit_bytes=V7X_VMEM_LIMIT)


def _ada_kernel(c_ref, w_ref, b_ref, o_ref):
    c = c_ref[...]
    ca = _silu(c).astype(BF16)
    o_ref[0] = jnp.dot(ca, w_ref[0].astype(BF16), preferred_element_type=F32) + b_ref[0]


def _ada_mod(c_pad, w_ada, b_ada):
    n_layers, d, n3 = w_ada.shape
    rows = c_pad.shape[0]
    tn = _tile(n3, 1024)
    return pl.pallas_call(
        _ada_kernel,
        grid=(n_layers, n3 // tn),
        in_specs=[
            pl.BlockSpec((rows, d), lambda l, j: (0, 0)),
            pl.BlockSpec((1, d, tn), lambda l, j: (l, 0, j)),
            pl.BlockSpec((1, 1, tn), lambda l, j: (l, 0, j)),
        ],
        out_specs=pl.BlockSpec((1, rows, tn), lambda l, j: (l, 0, j)),
        out_shape=jax.ShapeDtypeStruct((n_layers, rows, n3), F32),
        compiler_params=_params(("arbitrary", "arbitrary")),
        name="ada_mod",
    )(c_pad, w_ada, b_ada.reshape(n_layers, 1, n3))


def _prenorm_kernel(x_ref, g_ref, m_ref, o_ref):
    x = x_ref[0]
    ms = jnp.mean(x * x, axis=-1, keepdims=True)
    shift = m_ref[0, 0:1, :]
    scale = m_ref[0, 1:2, :]
    h = x * lax.rsqrt(ms + EPS) * g_ref[...]
    o_ref[0] = (h * (1.0 + scale) + shift).astype(o_ref.dtype)


def _prenorm(h_res, g, mod3):
    b, s, d = h_res.shape
    ts = _tile(s, 512)
    return pl.pallas_call(
        _prenorm_kernel,
        grid=(b, s // ts),
        in_specs=[
            pl.BlockSpec((1, ts, d), lambda i, j: (i, j, 0)),
            pl.BlockSpec((1, d), lambda i, j: (0, 0)),
            pl.BlockSpec((1, 3, d), lambda i, j: (i, 0, 0)),
        ],
        out_specs=pl.BlockSpec((1, ts, d), lambda i, j: (i, j, 0)),
        out_shape=jax.ShapeDtypeStruct((b, s, d), BF16),
        compiler_params=_params(("arbitrary", "arbitrary")),
        name="prenorm",
    )(h_res, g.reshape(1, d), mod3)


def _inproj_kernel(a_ref, w_ref, o_ref, wb_scr, *, seg_tiles):
    j = pl.program_id(0)

    @pl.when(pl.program_id(1) == 0)
    def _():
        wb_scr[...] = w_ref[0].astype(BF16)

    acc = jnp.dot(a_ref[...], wb_scr[...], preferred_element_type=F32)
    st = seg_tiles
    is_silu = jnp.logical_or(jnp.logical_and(j >= st, j < 2 * st), jnp.logical_and(j >= 5 * st, j < 6 * st))
    is_q = jnp.logical_and(j >= 2 * st, j < 3 * st)
    is_sig = j >= 6 * st
    is_id = jnp.logical_or(j < st, jnp.logical_and(j >= 3 * st, j < 5 * st))

    @pl.when(is_id)
    def _():
        o_ref[...] = acc.astype(o_ref.dtype)

    @pl.when(is_q)
    def _():
        o_ref[...] = (acc * (HEAD_DIM ** -0.5 * LOG2E)).astype(o_ref.dtype)

    @pl.when(is_silu)
    def _():
        o_ref[...] = _silu(acc).astype(o_ref.dtype)

    @pl.when(is_sig)
    def _():
        o_ref[...] = _sigmoid(acc).astype(o_ref.dtype)


def _in_proj(a, w_all, layer, d_ssm):
    m, k = a.shape
    n = w_all.shape[2]
    tm = _tile(m, 1024)
    tn = _tile(d_ssm, 512)
    return pl.pallas_call(
        functools.partial(_inproj_kernel, seg_tiles=d_ssm // tn),
        grid=(n // tn, m // tm),
        in_specs=[
            pl.BlockSpec((tm, k), lambda j, i: (i, 0)),
            pl.BlockSpec((1, k, tn), lambda j, i: (layer, 0, j)),
        ],
        out_specs=pl.BlockSpec((tm, tn), lambda j, i: (i, j)),
        out_shape=jax.ShapeDtypeStruct((m, n), BF16),
        scratch_shapes=[pltpu.VMEM((k, tn), BF16)],
        compiler_params=_params(("arbitrary", "arbitrary")),
        name="in_proj",
    )(a, w_all)


def _s5_kernel(x_ref, lr_ref, li_ref, ldt_ref, ba_ref, bb_ref, ca_ref, cb_ref, d_ref, o_ref,
               t_scr, r_scr, ot_scr, xs_scr, os_scr, *, nst, npb):
    nc = x_ref.shape[0] // CHUNK
    gps = GROUPS_PER_STEP

    @pl.when(pl.program_id(0) == 0)
    def _():
        t_scr[...] = jnp.zeros(t_scr.shape, t_scr.dtype)

    lr = lr_ref[...]
    li = li_ref[...]
    dt = jnp.exp(ldt_ref[...])
    mag = jnp.exp(lr * dt)
    are = mag * jnp.cos(li * dt)
    aim = mag * jnp.sin(li * dt)
    den = lr * lr + li * li
    nre = are - 1.0
    cre = (nre * lr + aim * li) / den
    cim = (aim * lr - nre * li) / den
    b_a = ba_ref[...]
    b_b = bb_ref[...]
    c_a = ca_ref[...]
    c_b = cb_ref[...]
    cre3 = cre[:, None, :]
    cim3 = cim[:, None, :]
    bbv = cre3 * b_a + cim3 * b_b
    bbs = cre3 * b_b - cim3 * b_a

    pre = jnp.ones_like(are)
    pim = jnp.zeros_like(are)
    pows = [(pre, pim)]
    for _ in range(CHUNK):
        pre, pim = are * pre - aim * pim, are * pim + aim * pre
        pows.append((pre, pim))

    same_group = (lax.broadcasted_iota(jnp.int32, (LANES, LANES), 0) // SSM_GROUP
                  == lax.broadcasted_iota(jnp.int32, (LANES, LANES), 1) // SSM_GROUP)
    own_state = (lax.broadcasted_iota(jnp.int32, (LANES, HW), 0) // SSM_GROUP
                 == lax.broadcasted_iota(jnp.int32, (LANES, HW), 1) // SW)
    bbv2 = bbv.reshape(LANES, SW)
    for k in range(CHUNK + 1):
        pr, pi = pows[k]
        z_k = (pr[:, None, :] * c_a + pi[:, None, :] * c_b).reshape(LANES, SW)
        if k < CHUNK:
            kk = lax.dot_general(bbv2, z_k, (((1,), (1,)), ((), ())),
                                 precision=lax.Precision.HIGHEST, preferred_element_type=F32)
            kk = jnp.where(same_group, kk, 0.0).astype(t_scr.dtype)
            for s in range(CHUNK - k):
                t = s + k
                t_scr[s * LANES:(s + 1) * LANES, t * LANES:(t + 1) * LANES] = kk
        if k >= 1:
            ot_scr[(k - 1) * LANES:k * LANES, :] = jnp.where(
                own_state, jnp.concatenate([z_k] * gps, axis=1), 0.0).astype(ot_scr.dtype)
    for s in range(CHUNK):
        pr, pi = pows[CHUNK - 1 - s]
        r_s = (pr[:, None, :] * bbv + pi[:, None, :] * bbs).reshape(LANES, SW)
        r_scr[s * LANES:(s + 1) * LANES, :] = jnp.where(
            own_state, jnp.concatenate([r_s] * gps, axis=1), 0.0).astype(r_scr.dtype)

    first_half = lax.broadcasted_iota(jnp.int32, are.shape, 1) < STATE
    sre, sim = pows[CHUNK]
    steps = []
    for _ in range(nst):
        steps.append((sre, jnp.where(first_half, -sim, sim)))
        sre, sim = sre * sre - sim * sim, 2.0 * sre * sim

    xs_scr[...] = x_ref[...].astype(F32)
    xcat = jnp.concatenate([xs_scr[pl.ds(s, nc, stride=CHUNK), :].astype(BF16) for s in range(CHUNK)], axis=1)

    row_in_batch = jnp.bitwise_and(lax.broadcasted_iota(jnp.int32, (nc, SW), 0), npb - 1)
    r = jnp.dot(xcat, r_scr[...], preferred_element_type=F32)
    h_ins = []
    for g in range(gps):
        h = r[:, g * SW:(g + 1) * SW]
        for k in range(nst):
            d = 1 << k
            hs = jnp.where(row_in_batch >= d, pltpu.roll(h, d, axis=0), 0.0)
            ar, ai = steps[k]
            h = h + ar[g:g + 1, :] * hs + ai[g:g + 1, :] * pltpu.roll(hs, STATE, axis=1)
        h_in = jnp.where(row_in_batch >= 1, pltpu.roll(h, 1, axis=0), 0.0)
        h_ins.append(h_in.astype(BF16))
    hcat = jnp.concatenate(h_ins, axis=1)

    y = jnp.dot(xcat, t_scr[...], preferred_element_type=F32)
    y = y + lax.dot_general(hcat, ot_scr[...], (((1,), (1,)), ((), ())), preferred_element_type=F32)
    y = y + d_ref[0] * xcat.astype(F32)
    gact = _gelu_tanh(y)
    for s in range(CHUNK):
        os_scr[pl.ds(s, nc, stride=CHUNK), :] = gact[:, s * LANES:(s + 1) * LANES]
    o_ref[...] = os_scr[...].astype(o_ref.dtype)


def _s5(proj, lam_re, lam_im, log_dt, b_re, b_im, c_re, c_im, d_skip, nst, npb, d_ssm):
    rows = proj.shape[0]
    g = d_ssm // SSM_GROUP
    gps = GROUPS_PER_STEP
    lr2 = jnp.concatenate([lam_re, lam_re], axis=-1)
    li2 = jnp.concatenate([lam_im, lam_im], axis=-1)
    ldt2 = jnp.broadcast_to(log_dt[:, None], (g, SW))
    brt = jnp.swapaxes(b_re, 1, 2)
    bit = jnp.swapaxes(b_im, 1, 2)
    b_a = jnp.concatenate([brt, bit], axis=-1)
    b_b = jnp.concatenate([-bit, brt], axis=-1)
    c_a = jnp.concatenate([c_re, -c_im], axis=-1)
    c_b = jnp.concatenate([-c_im, -c_re], axis=-1)
    d_lane = jnp.tile(d_skip.reshape(g // gps, 1, LANES), (1, 1, CHUNK))
    vec = pl.BlockSpec((gps, SW), lambda i: (i, 0))
    mat = pl.BlockSpec((gps, SSM_GROUP, SW), lambda i: (i, 0, 0))
    return pl.pallas_call(
        functools.partial(_s5_kernel, nst=nst, npb=npb),
        grid=(g // gps,),
        in_specs=[pl.BlockSpec((rows, LANES), lambda i: (0, i)), vec, vec, vec, mat, mat, mat, mat,
                  pl.BlockSpec((1, 1, XW), lambda i: (i, 0, 0))],
        out_specs=pl.BlockSpec((rows, LANES), lambda i: (0, i)),
        out_shape=jax.ShapeDtypeStruct((rows, d_ssm), BF16),
        scratch_shapes=[
            pltpu.VMEM((XW, XW), BF16),
            pltpu.VMEM((XW, HW), BF16),
            pltpu.VMEM((XW, HW), BF16),
            pltpu.VMEM((rows, LANES), F32),
            pltpu.VMEM((rows, LANES), F32),
        ],
        compiler_params=_params(("arbitrary",)),
        name="s5",
    )(proj, lr2, li2, ldt2, b_a, b_b, c_a, c_b, d_lane)


def _cast_weight_once(w_ref, wb_scr, first):
    @pl.when(first)
    def _():
        wb_scr[...] = w_ref[0].astype(BF16)


def _glu_kernel(a_ref, w_ref, g_ref, z_ref, o_ref, wb_scr):
    _cast_weight_once(w_ref, wb_scr, pl.program_id(1) == 0)
    acc = jnp.dot(a_ref[...], wb_scr[...], preferred_element_type=F32)
    g = g_ref[...].astype(F32)
    o_ref[...] = (g * _sigmoid(acc) * z_ref[...].astype(F32)).astype(o_ref.dtype)


def _glu(g_act, w_all, layer, proj, d_ssm):
    m, k = g_act.shape
    tm = _tile(m, 1024)
    tn = _tile(d_ssm, 1024)
    zoff = d_ssm // tn
    return pl.pallas_call(
        _glu_kernel,
        grid=(d_ssm // tn, m // tm),
        in_specs=[
            pl.BlockSpec((tm, k), lambda j, i: (i, 0)),
            pl.BlockSpec((1, k, tn), lambda j, i: (layer, 0, j)),
            pl.BlockSpec((tm, tn), lambda j, i: (i, j)),
            pl.BlockSpec((tm, tn), lambda j, i: (i, zoff + j)),
        ],
        out_specs=pl.BlockSpec((tm, tn), lambda j, i: (i, j)),
        out_shape=jax.ShapeDtypeStruct((m, d_ssm), BF16),
        scratch_shapes=[pltpu.VMEM((k, tn), BF16)],
        compiler_params=_params(("arbitrary", "arbitrary")),
        name="glu",
    )(g_act, w_all, g_act, proj)


def _bias_kernel(tab_ref, o_ref):
    h = pl.program_id(0)
    blk = o_ref.shape[-1]
    r = lax.broadcasted_iota(jnp.int32, (blk, blk), 0)
    c = lax.broadcasted_iota(jnp.int32, (blk, blk), 1)
    max_exact = N_BUCKETS // 2
    far = tab_ref[h * N_BUCKETS + N_BUCKETS - 1]
    for t in range(2):
        rel = r - c + t * blk
        n = jnp.maximum(rel, 0)
        nf = jnp.maximum(n, 1).astype(F32)
        large = max_exact + (jnp.log(nf / max_exact) / math.log(MAX_DISTANCE / max_exact)
                             * (N_BUCKETS - max_exact)).astype(jnp.int32)
        large = jnp.minimum(large, N_BUCKETS - 1)
        bucket = jnp.where(n < max_exact, n, large)
        val = jnp.zeros((blk, blk), F32)
        for b in range(N_BUCKETS):
            val = jnp.where(bucket == b, tab_ref[h * N_BUCKETS + b], val)
        o_ref[0, t] = jnp.where(rel >= 0, (val - far) * LOG2E, NEG_INF)


def _bias_tiles(rel_bias, n_heads, blk):
    tab = jnp.transpose(rel_bias.astype(F32)).reshape(-1)
    return pl.pallas_call(
        _bias_kernel,
        grid=(n_heads,),
        in_specs=[pl.BlockSpec(memory_space=pltpu.SMEM)],
        out_specs=pl.BlockSpec((1, 2, blk, blk), lambda h: (h, 0, 0, 0)),
        out_shape=jax.ShapeDtypeStruct((n_heads, 2, blk, blk), F32),
        compiler_params=_params(("arbitrary",)),
        name="t5_bias_tiles",
    )(tab)


def _attn_kernel(q_ref, k_ref, v_ref, z_ref, bias_ref, lamv_ref, sg_ref, o_ref,
                 m_scr, l_scr, acc_scr, s_scr, mb_scr, *, lam_init):
    qi = pl.program_id(2)
    blk = q_ref.shape[0]
    q = q_ref[...]
    qs = (q[:, :HEAD_DIM], q[:, HEAD_DIM:])

    m_scr[...] = jnp.full(m_scr.shape, -jnp.inf, F32)
    l_scr[...] = jnp.zeros(l_scr.shape, F32)
    acc_scr[...] = jnp.zeros(acc_scr.shape, F32)

    nlb = blk // LANES

    def produce(j, slot, bias):
        off = pl.multiple_of(j * blk, blk)
        k = k_ref[pl.ds(off, blk), :]
        for mp in range(2):
            s = lax.dot_general(qs[mp], k[:, mp * HEAD_DIM:(mp + 1) * HEAD_DIM],
                                (((1,), (1,)), ((), ())), preferred_element_type=F32)
            if bias is not None:
                s = s + bias
            s_scr[slot, mp] = s
            mb_scr[slot, mp] = jnp.broadcast_to(jnp.max(s, axis=-1, keepdims=True), (blk, LANES))

    def consume(j, slot, late_bias):
        off = pl.multiple_of(j * blk, blk)
        v = v_ref[pl.ds(off, blk), :]
        for mp in range(2):
            s = s_scr[slot, mp]
            if late_bias is None:
                m_blk = mb_scr[slot, mp]
            else:
                s = s + late_bias
                m_blk = jnp.max(s, axis=-1, keepdims=True)
            m_prev = m_scr[mp]
            m_new = jnp.maximum(m_prev, m_blk)
            alpha = jnp.exp2(m_prev - m_new)
            p = jnp.exp2(s - jnp.concatenate([m_new] * nlb, axis=1))
            psum = p[:, :LANES]
            for t in range(1, nlb):
                psum = psum + p[:, t * LANES:(t + 1) * LANES]
            l_scr[mp] = alpha * l_scr[mp] + psum
            acc_scr[mp] = (jnp.concatenate([alpha] * (2 * HEAD_DIM // LANES), axis=1) * acc_scr[mp]
                           + jnp.dot(p.astype(BF16), v, preferred_element_type=F32))
            m_scr[mp] = m_new

    n_far = jnp.maximum(qi - 1, 0)
    n_pairs = n_far // 2
    produce(0, 0, None)

    def pair_body(jj, carry):
        j = 2 * jj
        produce(j + 1, 1, None)
        consume(j, 0, None)
        produce(j + 2, 0, None)
        consume(j + 1, 1, None)
        return carry

    lax.fori_loop(0, n_pairs, pair_body, 0)
    j0 = 2 * n_pairs

    @pl.when(qi == 0)
    def _():
        consume(0, 0, bias_ref[0, 0])

    @pl.when(jnp.logical_and(qi >= 1, n_far == j0))
    def _():
        produce(j0 + 1, 1, bias_ref[0, 0])
        consume(j0, 0, bias_ref[0, 1])
        consume(j0 + 1, 1, None)

    @pl.when(n_far > j0)
    def _():
        produce(j0 + 1, 1, bias_ref[0, 1])
        consume(j0, 0, None)
        produce(j0 + 2, 0, bias_ref[0, 0])
        consume(j0 + 1, 1, None)
        consume(j0 + 2, 0, None)

    lv = lamv_ref[...]
    lam = (jnp.exp(jnp.sum(lv[0:1] * lv[1:2], axis=-1, keepdims=True))
           - jnp.exp(jnp.sum(lv[2:3] * lv[3:4], axis=-1, keepdims=True)) + lam_init)
    l0 = jnp.sum(l_scr[0], axis=-1, keepdims=True)
    l1 = jnp.sum(l_scr[1], axis=-1, keepdims=True)
    o = acc_scr[0] / l0 - lam * (acc_scr[1] / l1)
    ms = jnp.mean(o * o, axis=-1, keepdims=True)
    o = o * lax.rsqrt(ms + SUBLN_EPS) * sg_ref[...] * (1.0 - lam_init)
    o_ref[...] = (o * z_ref[...].astype(F32)).astype(o_ref.dtype)


def _attention(proj, bias_tiles, lamv, subln_g, bsz, seq, d_attn, lam_init):
    hw = 2 * HEAD_DIM
    n_heads = d_attn // hw
    blk = _tile(seq, ATT_BLOCK)
    nq = seq // blk
    qoff, koff, voff, zoff = (2 * d_attn // hw, 3 * d_attn // hw, 4 * d_attn // hw, 5 * d_attn // hw)
    return pl.pallas_call(
        functools.partial(_attn_kernel, lam_init=lam_init),
        grid=(bsz, n_heads, nq),
        in_specs=[
            pl.BlockSpec((blk, hw), lambda b, h, i: (b * nq + i, qoff + h)),
            pl.BlockSpec((seq, hw), lambda b, h, i: (b, koff + h)),
            pl.BlockSpec((seq, hw), lambda b, h, i: (b, voff + h)),
            pl.BlockSpec((blk, hw), lambda b, h, i: (b * nq + i, zoff + h)),
            pl.BlockSpec((1, 2, blk, blk), lambda b, h, i: (h, 0, 0, 0)),
            pl.BlockSpec((4, HEAD_DIM), lambda b, h, i: (0, 0)),
            pl.BlockSpec((1, hw), lambda b, h, i: (0, 0)),
        ],
        out_specs=pl.BlockSpec((blk, hw), lambda b, h, i: (b * nq + i, h)),
        out_shape=jax.ShapeDtypeStruct((bsz * seq, d_attn), BF16),
        scratch_shapes=[
            pltpu.VMEM((2, blk, LANES), F32),
            pltpu.VMEM((2, blk, LANES), F32),
            pltpu.VMEM((2, blk, hw), F32),
            pltpu.VMEM((2, 2, blk, blk), F32),
            pltpu.VMEM((2, 2, blk, LANES), F32),
        ],
        compiler_params=_params(("arbitrary", "arbitrary", "arbitrary")),
        name="diff_attention",
    )(proj, proj, proj, proj, bias_tiles, lamv, subln_g.reshape(1, hw))


def _merge_kernel(ys_ref, ya_ref, ws_ref, wa_ref, gs_ref, ga_ref, o_ref, wsb_scr, wab_scr):
    first = pl.program_id(1) == 0
    _cast_weight_once(ws_ref, wsb_scr, first)
    _cast_weight_once(wa_ref, wab_scr, first)
    ps = jnp.dot(ys_ref[...], wsb_scr[...], preferred_element_type=F32)
    pa = jnp.dot(ya_ref[...], wab_scr[...], preferred_element_type=F32)
    o_ref[...] = (gs_ref[...].astype(F32) * ps + ga_ref[...].astype(F32) * pa).astype(o_ref.dtype)


def _out_merge(y_s, y_a, ws_all, wa_all, layer, proj, d_ssm, d_attn):
    m, ks = y_s.shape
    ka = y_a.shape[1]
    d = ws_all.shape[2]
    tm = _tile(m, 1024)
    tn = _tile(d, 512)
    gs_off = (2 * d_ssm + 4 * d_attn) // tn
    ga_off = gs_off + d // tn
    return pl.pallas_call(
        _merge_kernel,
        grid=(d // tn, m // tm),
        in_specs=[
            pl.BlockSpec((tm, ks), lambda j, i: (i, 0)),
            pl.BlockSpec((tm, ka), lambda j, i: (i, 0)),
            pl.BlockSpec((1, ks, tn), lambda j, i: (layer, 0, j)),
            pl.BlockSpec((1, ka, tn), lambda j, i: (layer, 0, j)),
            pl.BlockSpec((tm, tn), lambda j, i: (i, gs_off + j)),
            pl.BlockSpec((tm, tn), lambda j, i: (i, ga_off + j)),
        ],
        out_specs=pl.BlockSpec((tm, tn), lambda j, i: (i, j)),
        out_shape=jax.ShapeDtypeStruct((m, d), BF16),
        scratch_shapes=[pltpu.VMEM((ks, tn), BF16), pltpu.VMEM((ka, tn), BF16)],
        compiler_params=_params(("arbitrary", "arbitrary")),
        name="out_merge",
    )(y_s, y_a, ws_all, wa_all, proj, proj)


def _resid_kernel(a_ref, w_ref, h_ref, m_ref, o_ref, wb_scr):
    _cast_weight_once(w_ref, wb_scr, jnp.logical_and(pl.program_id(1) == 0, pl.program_id(2) == 0))
    acc = jnp.dot(a_ref[...], wb_scr[...], preferred_element_type=F32)
    o_ref[...] = h_ref[...] + m_ref[0, 2:3, :] * acc


def _resid(merged, w_all, layer, h_res2, mod3, bsz, in_place):
    m, k = merged.shape
    d = w_all.shape[2]
    seq = m // bsz
    tm = _tile(seq, 1024)
    tn = _tile(d, 512)
    ns = seq // tm
    return pl.pallas_call(
        _resid_kernel,
        grid=(d // tn, bsz, ns),
        in_specs=[
            pl.BlockSpec((tm, k), lambda j, b, i: (b * ns + i, 0)),
            pl.BlockSpec((1, k, tn), lambda j, b, i: (layer, 0, j)),
            pl.BlockSpec((tm, tn), lambda j, b, i: (b * ns + i, j)),
            pl.BlockSpec((1, 3, tn), lambda j, b, i: (b, 0, j)),
        ],
        out_specs=pl.BlockSpec((tm, tn), lambda j, b, i: (b * ns + i, j)),
        out_shape=jax.ShapeDtypeStruct((m, d), F32),
        input_output_aliases={2: 0} if in_place else {},
        scratch_shapes=[pltpu.VMEM((k, tn), BF16)],
        compiler_params=_params(("arbitrary", "arbitrary", "arbitrary")),
        name="resid",
    )(merged, w_all, h_res2, mod3)


def _final_norm_kernel(x_ref, g_ref, o_ref):
    x = x_ref[...]
    ms = jnp.mean(x * x, axis=-1, keepdims=True)
    o_ref[...] = (x * lax.rsqrt(ms + EPS) * g_ref[...]).astype(o_ref.dtype)


def _final_norm(h2, g, out_dtype):
    m, d = h2.shape
    tm = _tile(m, 512)
    return pl.pallas_call(
        _final_norm_kernel,
        grid=(m // tm,),
        in_specs=[pl.BlockSpec((tm, d), lambda i: (i, 0)), pl.BlockSpec((1, d), lambda i: (0, 0))],
        out_specs=pl.BlockSpec((tm, d), lambda i: (i, 0)),
        out_shape=jax.ShapeDtypeStruct((m, d), out_dtype),
        compiler_params=_params(("arbitrary",)),
        name="final_norm",
    )(h2, g.reshape(1, d))


def kernel(x, c, norm_g, w_ada, b_ada, w_in, ssm_lambda_re, ssm_lambda_im, ssm_log_dt, ssm_b_re, ssm_b_im, ssm_c_re, ssm_c_im, ssm_d, w_glu, lambda_q1, lambda_k1, lambda_q2, lambda_k2, subln_g, w_out_ssm, w_out_attn, w_o, rel_bias, final_g):
    out_dtype = x.dtype
    bsz, seq, d = x.shape
    depth = w_in.shape[0]
    d_ssm = w_glu.shape[1]
    d_attn = w_out_attn.shape[1]
    n_heads = d_attn // (2 * HEAD_DIM)
    npb = seq // CHUNK
    nst = npb.bit_length() - 1
    assert seq % CHUNK == 0 and (1 << nst) == npb, "sequence must be CHUNK * 2^k"
    assert d_ssm % LANES == 0

    c_pad = jnp.zeros((8, d), F32).at[:bsz].set(c.astype(F32))
    mod = _ada_mod(c_pad, w_ada.astype(F32), b_ada.astype(F32))
    bias_tiles = _bias_tiles(rel_bias, n_heads, _tile(seq, ATT_BLOCK))

    h_res = x.astype(F32)
    for l in range(depth):
        mod3 = mod[l, :bsz].reshape(bsz, 3, d)
        hn = _prenorm(h_res, norm_g[l].astype(F32), mod3)
        proj = _in_proj(hn.reshape(bsz * seq, d), w_in.astype(F32), l, d_ssm)

        g_act = _s5(proj, ssm_lambda_re[l].astype(F32), ssm_lambda_im[l].astype(F32), ssm_log_dt[l].astype(F32),
                    ssm_b_re[l].astype(F32), ssm_b_im[l].astype(F32), ssm_c_re[l].astype(F32),
                    ssm_c_im[l].astype(F32), ssm_d[l].astype(F32), nst, npb, d_ssm)
        y_s = _glu(g_act, w_glu.astype(F32), l, proj, d_ssm)

        lam_init = 0.8 - 0.6 * math.exp(-0.3 * l)
        lamv = jnp.stack([lambda_q1[l], lambda_k1[l], lambda_q2[l], lambda_k2[l]]).astype(F32)
        y_a = _attention(proj, bias_tiles, lamv, subln_g[l].astype(F32), bsz, seq, d_attn, lam_init)

        merged = _out_merge(y_s, y_a, w_out_ssm.astype(F32), w_out_attn.astype(F32), l, proj, d_ssm, d_attn)
        h_res = _resid(merged, w_o.astype(F32), l, h_res.reshape(bsz * seq, d), mod3, bsz,
                       in_place=l > 0).reshape(bsz, seq, d)

    out = _final_norm(h_res.reshape(bsz * seq, d), final_g.astype(F32), out_dtype)
    return out.reshape(bsz, seq, d)
```

```python
import functools
import math

import jax
import jax.numpy as jnp
from jax import lax
from jax.experimental import pallas as pl
from jax.experimental.pallas import tpu as pltpu

F32 = jnp.float32
BF16 = jnp.bfloat16

SSM_GROUP = 16
STATE = 64
HEAD_DIM = 128
N_BUCKETS = 32
MAX_DISTANCE = 128
EPS = 1e-6
SUBLN_EPS = 1e-5
NEG_INF = -1e30

LANES = 128
CHUNK = 16
SW = 2 * STATE
GROUPS_PER_STEP = LANES // SSM_GROUP
XW = CHUNK * LANES
HW = GROUPS_PER_STEP * SW
ATT_BLOCK = 512
INPROJ_ROW_CHUNKS = 2
LOG2E = math.log2(math.e)
V7X_VMEM_LIMIT = 56 * 1024 * 1024


def _sigmoid(x):
    return 1.0 / (1.0 + jnp.exp(-x))


def _silu(x):
    return x * _sigmoid(x)


def _gelu_tanh(x):
    return 0.5 * x * (1.0 + jnp.tanh(math.sqrt(2.0 / math.pi) * (x + 0.044715 * (x * x * x))))


def _tile(dim, pref):
    t = min(dim, pref)
    assert dim % t == 0, (dim, pref)
    return t


def _params(sem):
    return pltpu.CompilerParams(dimension_semantics=sem, vmem_limit_bytes=V7X_VMEM_LIMIT)


def _ada_kernel(c_ref, w_ref, b_ref, o_ref):
    c = c_ref[...]
    ca = _silu(c).astype(BF16)
    o_ref[0] = jnp.dot(ca, w_ref[0].astype(BF16), preferred_element_type=F32) + b_ref[0]


def _ada_mod(c_pad, w_ada, b_ada):
    n_layers, d, n3 = w_ada.shape
    rows = c_pad.shape[0]
    tn = _tile(n3, 1024)
    return pl.pallas_call(
        _ada_kernel,
        grid=(n_layers, n3 // tn),
        in_specs=[
            pl.BlockSpec((rows, d), lambda l, j: (0, 0)),
            pl.BlockSpec((1, d, tn), lambda l, j: (l, 0, j)),
            pl.BlockSpec((1, 1, tn), lambda l, j: (l, 0, j)),
        ],
        out_specs=pl.BlockSpec((1, rows, tn), lambda l, j: (l, 0, j)),
        out_shape=jax.ShapeDtypeStruct((n_layers, rows, n3), F32),
        compiler_params=_params(("arbitrary", "arbitrary")),
        name="ada_mod",
    )(c_pad, w_ada, b_ada.reshape(n_layers, 1, n3))


def _prenorm_kernel(x_ref, g_ref, m_ref, o_ref):
    x = x_ref[0]
    ms = jnp.mean(x * x, axis=-1, keepdims=True)
    shift = m_ref[0, 0:1, :]
    scale = m_ref[0, 1:2, :]
    h = x * lax.rsqrt(ms + EPS) * g_ref[...]
    o_ref[0] = (h * (1.0 + scale) + shift).astype(o_ref.dtype)


def _prenorm(h_res, g, mod3):
    b, s, d = h_res.shape
    ts = _tile(s, 512)
    return pl.pallas_call(
        _prenorm_kernel,
        grid=(b, s // ts),
        in_specs=[
            pl.BlockSpec((1, ts, d), lambda i, j: (i, j, 0)),
            pl.BlockSpec((1, d), lambda i, j: (0, 0)),
            pl.BlockSpec((1, 3, d), lambda i, j: (i, 0, 0)),
        ],
        out_specs=pl.BlockSpec((1, ts, d), lambda i, j: (i, j, 0)),
        out_shape=jax.ShapeDtypeStruct((b, s, d), BF16),
        compiler_params=_params(("arbitrary", "arbitrary")),
        name="prenorm",
    )(h_res, g.reshape(1, d), mod3)


def _inproj_kernel(a_ref, w_ref, o_ref, wb_scr, *, seg_tiles):
    j = pl.program_id(0)

    @pl.when(pl.program_id(1) == 0)
    def _():
        wb_scr[...] = w_ref[0].astype(BF16)

    st = seg_tiles
    is_silu = jnp.logical_or(jnp.logical_and(j >= st, j < 2 * st), jnp.logical_and(j >= 5 * st, j < 6 * st))
    is_q = jnp.logical_and(j >= 2 * st, j < 3 * st)
    is_gate = jnp.logical_or(is_silu, j >= 6 * st)

    tm = a_ref.shape[0]
    cm = tm // INPROJ_ROW_CHUNKS

    def run(epilogue):
        for c in range(INPROJ_ROW_CHUNKS):
            acc = jnp.dot(a_ref[c * cm:(c + 1) * cm, :], wb_scr[...], preferred_element_type=F32)
            o_ref[c * cm:(c + 1) * cm, :] = epilogue(acc).astype(o_ref.dtype)

    @pl.when(jnp.logical_not(is_gate))
    def _():
        scale = jnp.where(is_q, HEAD_DIM ** -0.5 * LOG2E, 1.0).astype(F32)
        run(lambda acc: acc * scale)

    @pl.when(is_gate)
    def _():
        run(lambda acc: _sigmoid(acc) * jnp.where(is_silu, acc, 1.0))


def _in_proj(a, w_all, layer, d_ssm):
    m, k = a.shape
    n = w_all.shape[2]
    tm = _tile(m, 512)
    tn = _tile(d_ssm, 1024)
    return pl.pallas_call(
        functools.partial(_inproj_kernel, seg_tiles=d_ssm // tn),
        grid=(n // tn, m // tm),
        in_specs=[
            pl.BlockSpec((tm, k), lambda j, i: (i, 0)),
            pl.BlockSpec((1, k, tn), lambda j, i: (layer, 0, j)),
        ],
        out_specs=pl.BlockSpec((tm, tn), lambda j, i: (i, j)),
        out_shape=jax.ShapeDtypeStruct((m, n), BF16),
        scratch_shapes=[pltpu.VMEM((k, tn), BF16)],
        compiler_params=_params(("arbitrary", "arbitrary")),
        name="in_proj",
    )(a, w_all)


def _s5_kernel(x_ref, lr_ref, li_ref, ldt_ref, ba_ref, bb_ref, ca_ref, cb_ref, d_ref, o_ref,
               t_scr, r_scr, ot_scr, xs_scr, os_scr, *, nst, npb):
    nc = x_ref.shape[0] // CHUNK
    gps = GROUPS_PER_STEP

    @pl.when(pl.program_id(0) == 0)
    def _():
        t_scr[...] = jnp.zeros(t_scr.shape, t_scr.dtype)

    lr = lr_ref[...]
    li = li_ref[...]
    dt = jnp.exp(ldt_ref[...])
    mag = jnp.exp(lr * dt)
    are = mag * jnp.cos(li * dt)
    aim = mag * jnp.sin(li * dt)
    den = lr * lr + li * li
    nre = are - 1.0
    cre = (nre * lr + aim * li) / den
    cim = (aim * lr - nre * li) / den
    b_a = ba_ref[...]
    b_b = bb_ref[...]
    c_a = ca_ref[...]
    c_b = cb_ref[...]
    cre3 = cre[:, None, :]
    cim3 = cim[:, None, :]
    bbv = cre3 * b_a + cim3 * b_b
    bbs = cre3 * b_b - cim3 * b_a

    pre = jnp.ones_like(are)
    pim = jnp.zeros_like(are)
    pows = [(pre, pim)]
    for _ in range(CHUNK):
        pre, pim = are * pre - aim * pim, are * pim + aim * pre
        pows.append((pre, pim))

    same_group = (lax.broadcasted_iota(jnp.int32, (LANES, LANES), 0) // SSM_GROUP
                  == lax.broadcasted_iota(jnp.int32, (LANES, LANES), 1) // SSM_GROUP)
    own_state = (lax.broadcasted_iota(jnp.int32, (LANES, HW), 0) // SSM_GROUP
                 == lax.broadcasted_iota(jnp.int32, (LANES, HW), 1) // SW)
    bbv2 = bbv.reshape(LANES, SW)
    for k in range(CHUNK + 1):
        pr, pi = pows[k]
        z_k = (pr[:, None, :] * c_a + pi[:, None, :] * c_b).reshape(LANES, SW)
        if k < CHUNK:
            kk = lax.dot_general(bbv2, z_k, (((1,), (1,)), ((), ())),
                                 precision=lax.Precision.HIGHEST, preferred_element_type=F32)
            kk = jnp.where(same_group, kk, 0.0).astype(t_scr.dtype)
            for s in range(CHUNK - k):
                t = s + k
                t_scr[s * LANES:(s + 1) * LANES, t * LANES:(t + 1) * LANES] = kk
        if k >= 1:
            ot_scr[(k - 1) * LANES:k * LANES, :] = jnp.where(
                own_state, jnp.concatenate([z_k] * gps, axis=1), 0.0).astype(ot_scr.dtype)
    for s in range(CHUNK):
        pr, pi = pows[CHUNK - 1 - s]
        r_s = (pr[:, None, :] * bbv + pi[:, None, :] * bbs).reshape(LANES, SW)
        r_scr[s * LANES:(s + 1) * LANES, :] = jnp.where(
            own_state, jnp.concatenate([r_s] * gps, axis=1), 0.0).astype(r_scr.dtype)

    first_half = lax.broadcasted_iota(jnp.int32, are.shape, 1) < STATE
    sre, sim = pows[CHUNK]
    steps = []
    for _ in range(nst):
        steps.append((sre, jnp.where(first_half, -sim, sim)))
        sre, sim = sre * sre - sim * sim, 2.0 * sre * sim

    xs_scr[...] = x_ref[...].astype(F32)
    xcat = jnp.concatenate([xs_scr[pl.ds(s, nc, stride=CHUNK), :].astype(BF16) for s in range(CHUNK)], axis=1)

    row_in_batch = jnp.bitwise_and(lax.broadcasted_iota(jnp.int32, (nc, SW), 0), npb - 1)
    r = jnp.dot(xcat, r_scr[...], preferred_element_type=F32)
    h_ins = []
    for g in range(gps):
        h = r[:, g * SW:(g + 1) * SW]
        for k in range(nst):
            d = 1 << k
            hs = jnp.where(row_in_batch >= d, pltpu.roll(h, d, axis=0), 0.0)
            ar, ai = steps[k]
            h = h + ar[g:g + 1, :] * hs + ai[g:g + 1, :] * pltpu.roll(hs, STATE, axis=1)
        h_in = jnp.where(row_in_batch >= 1, pltpu.roll(h, 1, axis=0), 0.0)
        h_ins.append(h_in.astype(BF16))
    hcat = jnp.concatenate(h_ins, axis=1)

    y = jnp.dot(xcat, t_scr[...], preferred_element_type=F32)
    y = y + lax.dot_general(hcat, ot_scr[...], (((1,), (1,)), ((), ())), preferred_element_type=F32)
    y = y + d_ref[0] * xcat.astype(F32)
    gact = _gelu_tanh(y)
    for s in range(CHUNK):
        os_scr[pl.ds(s, nc, stride=CHUNK), :] = gact[:, s * LANES:(s + 1) * LANES]
    o_ref[...] = os_scr[...].astype(o_ref.dtype)


def _s5(proj, lam_re, lam_im, log_dt, b_re, b_im, c_re, c_im, d_skip, nst, npb, d_ssm):
    rows = proj.shape[0]
    g = d_ssm // SSM_GROUP
    gps = GROUPS_PER_STEP
    lr2 = jnp.concatenate([lam_re, lam_re], axis=-1)
    li2 = jnp.concatenate([lam_im, lam_im], axis=-1)
    ldt2 = jnp.broadcast_to(log_dt[:, None], (g, SW))
    brt = jnp.swapaxes(b_re, 1, 2)
    bit = jnp.swapaxes(b_im, 1, 2)
    b_a = jnp.concatenate([brt, bit], axis=-1)
    b_b = jnp.concatenate([-bit, brt], axis=-1)
    c_a = jnp.concatenate([c_re, -c_im], axis=-1)
    c_b = jnp.concatenate([-c_im, -c_re], axis=-1)
    d_lane = jnp.tile(d_skip.reshape(g // gps, 1, LANES), (1, 1, CHUNK))
    vec = pl.BlockSpec((gps, SW), lambda i: (i, 0))
    mat = pl.BlockSpec((gps, SSM_GROUP, SW), lambda i: (i, 0, 0))
    return pl.pallas_call(
        functools.partial(_s5_kernel, nst=nst, npb=npb),
        grid=(g // gps,),
        in_specs=[pl.BlockSpec((rows, LANES), lambda i: (0, i)), vec, vec, vec, mat, mat, mat, mat,
                  pl.BlockSpec((1, 1, XW), lambda i: (i, 0, 0))],
        out_specs=pl.BlockSpec((rows, LANES), lambda i: (0, i)),
        out_shape=jax.ShapeDtypeStruct((rows, d_ssm), BF16),
        scratch_shapes=[
            pltpu.VMEM((XW, XW), BF16),
            pltpu.VMEM((XW, HW), BF16),
            pltpu.VMEM((XW, HW), BF16),
            pltpu.VMEM((rows, LANES), F32),
            pltpu.VMEM((rows, LANES), F32),
        ],
        compiler_params=_params(("arbitrary",)),
        name="s5",
    )(proj, lr2, li2, ldt2, b_a, b_b, c_a, c_b, d_lane)


def _cast_weight_once(w_ref, wb_scr, first):
    @pl.when(first)
    def _():
        wb_scr[...] = w_ref[0].astype(BF16)


def _glu_kernel(a_ref, w_ref, g_ref, z_ref, o_ref, wb_scr):
    _cast_weight_once(w_ref, wb_scr, pl.program_id(1) == 0)
    acc = jnp.dot(a_ref[...], wb_scr[...], preferred_element_type=F32)
    g = g_ref[...].astype(F32)
    o_ref[...] = (g * _sigmoid(acc) * z_ref[...].astype(F32)).astype(o_ref.dtype)


def _glu(g_act, w_all, layer, proj, d_ssm):
    m, k = g_act.shape
    tm = _tile(m, 1024)
    tn = _tile(d_ssm, 1024)
    zoff = d_ssm // tn
    return pl.pallas_call(
        _glu_kernel,
        grid=(d_ssm // tn, m // tm),
        in_specs=[
            pl.BlockSpec((tm, k), lambda j, i: (i, 0)),
            pl.BlockSpec((1, k, tn), lambda j, i: (layer, 0, j)),
            pl.BlockSpec((tm, tn), lambda j, i: (i, j)),
            pl.BlockSpec((tm, tn), lambda j, i: (i, zoff + j)),
        ],
        out_specs=pl.BlockSpec((tm, tn), lambda j, i: (i, j)),
        out_shape=jax.ShapeDtypeStruct((m, d_ssm), BF16),
        scratch_shapes=[pltpu.VMEM((k, tn), BF16)],
        compiler_params=_params(("arbitrary", "arbitrary")),
        name="glu",
    )(g_act, w_all, g_act, proj)


def _bias_kernel(tab_ref, o_ref):
    h = pl.program_id(0)
    blk = o_ref.shape[-1]
    r = lax.broadcasted_iota(jnp.int32, (blk, blk), 0)
    c = lax.broadcasted_iota(jnp.int32, (blk, blk), 1)
    max_exact = N_BUCKETS // 2
    far = tab_ref[h * N_BUCKETS + N_BUCKETS - 1]
    for t in range(2):
        rel = r - c + t * blk
        n = jnp.maximum(rel, 0)
        nf = jnp.maximum(n, 1).astype(F32)
        large = max_exact + (jnp.log(nf / max_exact) / math.log(MAX_DISTANCE / max_exact)
                             * (N_BUCKETS - max_exact)).astype(jnp.int32)
        large = jnp.minimum(large, N_BUCKETS - 1)
        bucket = jnp.where(n < max_exact, n, large)
        val = jnp.zeros((blk, blk), F32)
        for b in range(N_BUCKETS):
            val = jnp.where(bucket == b, tab_ref[h * N_BUCKETS + b], val)
        o_ref[0, t] = jnp.where(rel >= 0, (val - far) * LOG2E, NEG_INF)


def _bias_tiles(rel_bias, n_heads, blk):
    tab = jnp.transpose(rel_bias.astype(F32)).reshape(-1)
    return pl.pallas_call(
        _bias_kernel,
        grid=(n_heads,),
        in_specs=[pl.BlockSpec(memory_space=pltpu.SMEM)],
        out_specs=pl.BlockSpec((1, 2, blk, blk), lambda h: (h, 0, 0, 0)),
        out_shape=jax.ShapeDtypeStruct((n_heads, 2, blk, blk), F32),
        compiler_params=_params(("arbitrary",)),
        name="t5_bias_tiles",
    )(tab)


def _attn_kernel(q_ref, k_ref, v_ref, z_ref, bias_ref, lamv_ref, sg_ref, o_ref,
                 m_scr, l_scr, acc_scr, s_scr, mb_scr, *, lam_init):
    qi = pl.program_id(2)
    blk = q_ref.shape[0]
    q = q_ref[...]
    qs = (q[:, :HEAD_DIM], q[:, HEAD_DIM:])

    m_scr[...] = jnp.full(m_scr.shape, -jnp.inf, F32)
    l_scr[...] = jnp.zeros(l_scr.shape, F32)
    acc_scr[...] = jnp.zeros(acc_scr.shape, F32)

    nlb = blk // LANES

    def produce(j, slot, bias):
        off = pl.multiple_of(j * blk, blk)
        k = k_ref[pl.ds(off, blk), :]
        for mp in range(2):
            s = lax.dot_general(qs[mp], k[:, mp * HEAD_DIM:(mp + 1) * HEAD_DIM],
                                (((1,), (1,)), ((), ())), preferred_element_type=F32)
            if bias is not None:
                s = s + bias
            s_scr[slot, mp] = s
            mb_scr[slot, mp] = jnp.broadcast_to(jnp.max(s, axis=-1, keepdims=True), (blk, LANES))

    def consume(j, slot, late_bias):
        off = pl.multiple_of(j * blk, blk)
        v = v_ref[pl.ds(off, blk), :]
        for mp in range(2):
            s = s_scr[slot, mp]
            if late_bias is None:
                m_blk = mb_scr[slot, mp]
            else:
                s = s + late_bias
                m_blk = jnp.max(s, axis=-1, keepdims=True)
            m_prev = m_scr[mp]
            m_new = jnp.maximum(m_prev, m_blk)
            alpha = jnp.exp2(m_prev - m_new)
            p = jnp.exp2(s - jnp.concatenate([m_new] * nlb, axis=1))
            psum = p[:, :LANES]
            for t in range(1, nlb):
                psum = psum + p[:, t * LANES:(t + 1) * LANES]
            l_scr[mp] = alpha * l_scr[mp] + psum
            acc_scr[mp] = (jnp.concatenate([alpha] * (2 * HEAD_DIM // LANES), axis=1) * acc_scr[mp]
                           + jnp.dot(p.astype(BF16), v, preferred_element_type=F32))
            m_scr[mp] = m_new

    n_far = jnp.maximum(qi - 1, 0)
    n_pairs = n_far // 2
    produce(0, 0, None)

    def pair_body(jj, carry):
        j = 2 * jj
        produce(j + 1, 1, None)
        consume(j, 0, None)
        produce(j + 2, 0, None)
        consume(j + 1, 1, None)
        return carry

    lax.fori_loop(0, n_pairs, pair_body, 0)
    j0 = 2 * n_pairs

    @pl.when(qi == 0)
    def _():
        consume(0, 0, bias_ref[0, 0])

    @pl.when(jnp.logical_and(qi >= 1, n_far == j0))
    def _():
        produce(j0 + 1, 1, bias_ref[0, 0])
        consume(j0, 0, bias_ref[0, 1])
        consume(j0 + 1, 1, None)

    @pl.when(n_far > j0)
    def _():
        produce(j0 + 1, 1, bias_ref[0, 1])
        consume(j0, 0, None)
        produce(j0 + 2, 0, bias_ref[0, 0])
        consume(j0 + 1, 1, None)
        consume(j0 + 2, 0, None)

    lv = lamv_ref[...]
    lam = (jnp.exp(jnp.sum(lv[0:1] * lv[1:2], axis=-1, keepdims=True))
           - jnp.exp(jnp.sum(lv[2:3] * lv[3:4], axis=-1, keepdims=True)) + lam_init)
    l0 = jnp.sum(l_scr[0], axis=-1, keepdims=True)
    l1 = jnp.sum(l_scr[1], axis=-1, keepdims=True)
    o = acc_scr[0] / l0 - lam * (acc_scr[1] / l1)
    ms = jnp.mean(o * o, axis=-1, keepdims=True)
    o = o * lax.rsqrt(ms + SUBLN_EPS) * sg_ref[...] * (1.0 - lam_init)
    o_ref[...] = (o * z_ref[...].astype(F32)).astype(o_ref.dtype)


def _attention(proj, bias_tiles, lamv, subln_g, bsz, seq, d_attn, lam_init):
    hw = 2 * HEAD_DIM
    n_heads = d_attn // hw
    blk = _tile(seq, ATT_BLOCK)
    nq = seq // blk
    qoff, koff, voff, zoff = (2 * d_attn // hw, 3 * d_attn // hw, 4 * d_attn // hw, 5 * d_attn // hw)
    return pl.pallas_call(
        functools.partial(_attn_kernel, lam_init=lam_init),
        grid=(bsz, n_heads, nq),
        in_specs=[
            pl.BlockSpec((blk, hw), lambda b, h, i: (b * nq + i, qoff + h)),
            pl.BlockSpec((seq, hw), lambda b, h, i: (b, koff + h)),
            pl.BlockSpec((seq, hw), lambda b, h, i: (b, voff + h)),
            pl.BlockSpec((blk, hw), lambda b, h, i: (b * nq + i, zoff + h)),
            pl.BlockSpec((1, 2, blk, blk), lambda b, h, i: (h, 0, 0, 0)),
            pl.BlockSpec((4, HEAD_DIM), lambda b, h, i: (0, 0)),
            pl.BlockSpec((1, hw), lambda b, h, i: (0, 0)),
        ],
        out_specs=pl.BlockSpec((blk, hw), lambda b, h, i: (b * nq + i, h)),
        out_shape=jax.ShapeDtypeStruct((bsz * seq, d_attn), BF16),
        scratch_shapes=[
            pltpu.VMEM((2, blk, LANES), F32),
            pltpu.VMEM((2, blk, LANES), F32),
            pltpu.VMEM((2, blk, hw), F32),
            pltpu.VMEM((2, 2, blk, blk), F32),
            pltpu.VMEM((2, 2, blk, LANES), F32),
        ],
        compiler_params=_params(("arbitrary", "arbitrary", "arbitrary")),
        name="diff_attention",
    )(proj, proj, proj, proj, bias_tiles, lamv, subln_g.reshape(1, hw))


def _merge_kernel(ys_ref, ya_ref, ws_ref, wa_ref, gs_ref, ga_ref, o_ref, wsb_scr, wab_scr):
    first = pl.program_id(1) == 0
    _cast_weight_once(ws_ref, wsb_scr, first)
    _cast_weight_once(wa_ref, wab_scr, first)
    ps = jnp.dot(ys_ref[...], wsb_scr[...], preferred_element_type=F32)
    pa = jnp.dot(ya_ref[...], wab_scr[...], preferred_element_type=F32)
    o_ref[...] = (gs_ref[...].astype(F32) * ps + ga_ref[...].astype(F32) * pa).astype(o_ref.dtype)


def _out_merge(y_s, y_a, ws_all, wa_all, layer, proj, d_ssm, d_attn):
    m, ks = y_s.shape
    ka = y_a.shape[1]
    d = ws_all.shape[2]
    tm = _tile(m, 1024)
    tn = _tile(d, 512)
    gs_off = (2 * d_ssm + 4 * d_attn) // tn
    ga_off = gs_off + d // tn
    return pl.pallas_call(
        _merge_kernel,
        grid=(d // tn, m // tm),
        in_specs=[
            pl.BlockSpec((tm, ks), lambda j, i: (i, 0)),
            pl.BlockSpec((tm, ka), lambda j, i: (i, 0)),
            pl.BlockSpec((1, ks, tn), lambda j, i: (layer, 0, j)),
            pl.BlockSpec((1, ka, tn), lambda j, i: (layer, 0, j)),
            pl.BlockSpec((tm, tn), lambda j, i: (i, gs_off + j)),
            pl.BlockSpec((tm, tn), lambda j, i: (i, ga_off + j)),
        ],
        out_specs=pl.BlockSpec((tm, tn), lambda j, i: (i, j)),
        out_shape=jax.ShapeDtypeStruct((m, d), BF16),
        scratch_shapes=[pltpu.VMEM((ks, tn), BF16), pltpu.VMEM((ka, tn), BF16)],
        compiler_params=_params(("arbitrary", "arbitrary")),
        name="out_merge",
    )(y_s, y_a, ws_all, wa_all, proj, proj)


def _resid_kernel(a_ref, w_ref, h_ref, m_ref, o_ref, wb_scr):
    _cast_weight_once(w_ref, wb_scr, jnp.logical_and(pl.program_id(1) == 0, pl.program_id(2) == 0))
    acc = jnp.dot(a_ref[...], wb_scr[...], preferred_element_type=F32)
    o_ref[...] = h_ref[...] + m_ref[0, 2:3, :] * acc


def _resid(merged, w_all, layer, h_res2, mod3, bsz, in_place):
    m, k = merged.shape
    d = w_all.shape[2]
    seq = m // bsz
    tm = _tile(seq, 1024)
    tn = _tile(d, 512)
    ns = seq // tm
    return pl.pallas_call(
        _resid_kernel,
        grid=(d // tn, bsz, ns),
        in_specs=[
            pl.BlockSpec((tm, k), lambda j, b, i: (b * ns + i, 0)),
            pl.BlockSpec((1, k, tn), lambda j, b, i: (layer, 0, j)),
            pl.BlockSpec((tm, tn), lambda j, b, i: (b * ns + i, j)),
            pl.BlockSpec((1, 3, tn), lambda j, b, i: (b, 0, j)),
        ],
        out_specs=pl.BlockSpec((tm, tn), lambda j, b, i: (b * ns + i, j)),
        out_shape=jax.ShapeDtypeStruct((m, d), F32),
        input_output_aliases={2: 0} if in_place else {},
        scratch_shapes=[pltpu.VMEM((k, tn), BF16)],
        compiler_params=_params(("arbitrary", "arbitrary", "arbitrary")),
        name="resid",
    )(merged, w_all, h_res2, mod3)


def _final_norm_kernel(x_ref, g_ref, o_ref):
    x = x_ref[...]
    ms = jnp.mean(x * x, axis=-1, keepdims=True)
    o_ref[...] = (x * lax.rsqrt(ms + EPS) * g_ref[...]).astype(o_ref.dtype)


def _final_norm(h2, g, out_dtype):
    m, d = h2.shape
    tm = _tile(m, 512)
    return pl.pallas_call(
        _final_norm_kernel,
        grid=(m // tm,),
        in_specs=[pl.BlockSpec((tm, d), lambda i: (i, 0)), pl.BlockSpec((1, d), lambda i: (0, 0))],
        out_specs=pl.BlockSpec((tm, d), lambda i: (i, 0)),
        out_shape=jax.ShapeDtypeStruct((m, d), out_dtype),
        compiler_params=_params(("arbitrary",)),
        name="final_norm",
    )(h2, g.reshape(1, d))


def kernel(x, c, norm_g, w_ada, b_ada, w_in, ssm_lambda_re, ssm_lambda_im, ssm_log_dt, ssm_b_re, ssm_b_im, ssm_c_re, ssm_c_im, ssm_d, w_glu, lambda_q1, lambda_k1, lambda_q2, lambda_k2, subln_g, w_out_ssm, w_out_attn, w_o, rel_bias, final_g):
    out_dtype = x.dtype
    bsz, seq, d = x.shape
    depth = w_in.shape[0]
    d_ssm = w_glu.shape[1]
    d_attn = w_out_attn.shape[1]
    n_heads = d_attn // (2 * HEAD_DIM)
    npb = seq // CHUNK
    nst = npb.bit_length() - 1
    assert seq % CHUNK == 0 and (1 << nst) == npb, "sequence must be CHUNK * 2^k"
    assert d_ssm % LANES == 0

    c_pad = jnp.zeros((8, d), F32).at[:bsz].set(c.astype(F32))
    mod = _ada_mod(c_pad, w_ada.astype(F32), b_ada.astype(F32))
    bias_tiles = _bias_tiles(rel_bias, n_heads, _tile(seq, ATT_BLOCK))

    h_res = x.astype(F32)
    for l in range(depth):
        mod3 = mod[l, :bsz].reshape(bsz, 3, d)
        hn = _prenorm(h_res, norm_g[l].astype(F32), mod3)
        proj = _in_proj(hn.reshape(bsz * seq, d), w_in.astype(F32), l, d_ssm)

        g_act = _s5(proj, ssm_lambda_re[l].astype(F32), ssm_lambda_im[l].astype(F32), ssm_log_dt[l].astype(F32),
                    ssm_b_re[l].astype(F32), ssm_b_im[l].astype(F32), ssm_c_re[l].astype(F32),
                    ssm_c_im[l].astype(F32), ssm_d[l].astype(F32), nst, npb, d_ssm)
        y_s = _glu(g_act, w_glu.astype(F32), l, proj, d_ssm)

        lam_init = 0.8 - 0.6 * math.exp(-0.3 * l)
        lamv = jnp.stack([lambda_q1[l], lambda_k1[l], lambda_q2[l], lambda_k2[l]]).astype(F32)
        y_a = _attention(proj, bias_tiles, lamv, subln_g[l].astype(F32), bsz, seq, d_attn, lam_init)

        merged = _out_merge(y_s, y_a, w_out_ssm.astype(F32), w_out_attn.astype(F32), l, proj, d_ssm, d_attn)
        h_res = _resid(merged, w_o.astype(F32), l, h_res.reshape(bsz * seq, d), mod3, bsz,
                       in_place=l > 0).reshape(bsz, seq, d)

    out = _final_norm(h_res.reshape(bsz * seq, d), final_g.astype(F32), out_dtype)
    return out.reshape(bsz, seq, d)
```

```python
import functools
import math

import jax
import jax.numpy as jnp
from jax import lax
from jax.experimental import pallas as pl
from jax.experimental.pallas import tpu as pltpu

F32 = jnp.float32
BF16 = jnp.bfloat16

SSM_GROUP = 16
STATE = 64
HEAD_DIM = 128
N_BUCKETS = 32
MAX_DISTANCE = 128
EPS = 1e-6
SUBLN_EPS = 1e-5
NEG_INF = -1e30

LANES = 128
CHUNK = 16
SW = 2 * STATE
GROUPS_PER_STEP = LANES // SSM_GROUP
XW = CHUNK * LANES
HW = GROUPS_PER_STEP * SW
ATT_BLOCK = 512
INPROJ_ROW_CHUNKS = 2
LOG2E = math.log2(math.e)
V7X_VMEM_LIMIT = 56 * 1024 * 1024


def _sigmoid(x):
    return 1.0 / (1.0 + jnp.exp(-x))


def _silu(x):
    return x * _sigmoid(x)


def _gelu_tanh(x):
    return 0.5 * x * (1.0 + jnp.tanh(math.sqrt(2.0 / math.pi) * (x + 0.044715 * (x * x * x))))


def _tile(dim, pref):
    t = min(dim, pref)
    assert dim % t == 0, (dim, pref)
    return t


def _params(sem):
    return pltpu.CompilerParams(dimension_semantics=sem, vmem_limit_bytes=V7X_VMEM_LIMIT)


def _ada_kernel(c_ref, w_ref, b_ref, o_ref):
    c = c_ref[...]
    ca = _silu(c).astype(BF16)
    o_ref[0] = jnp.dot(ca, w_ref[0].astype(BF16), preferred_element_type=F32) + b_ref[0]


def _ada_mod(c_pad, w_ada, b_ada):
    n_layers, d, n3 = w_ada.shape
    rows = c_pad.shape[0]
    tn = _tile(n3, 1024)
    return pl.pallas_call(
        _ada_kernel,
        grid=(n_layers, n3 // tn),
        in_specs=[
            pl.BlockSpec((rows, d), lambda l, j: (0, 0)),
            pl.BlockSpec((1, d, tn), lambda l, j: (l, 0, j)),
            pl.BlockSpec((1, 1, tn), lambda l, j: (l, 0, j)),
        ],
        out_specs=pl.BlockSpec((1, rows, tn), lambda l, j: (l, 0, j)),
        out_shape=jax.ShapeDtypeStruct((n_layers, rows, n3), F32),
        compiler_params=_params(("arbitrary", "arbitrary")),
        name="ada_mod",
    )(c_pad, w_ada, b_ada.reshape(n_layers, 1, n3))


def _prenorm_kernel(x_ref, g_ref, m_ref, o_ref):
    x = x_ref[0]
    ms = jnp.mean(x * x, axis=-1, keepdims=True)
    shift = m_ref[0, 0:1, :]
    scale = m_ref[0, 1:2, :]
    h = x * lax.rsqrt(ms + EPS) * g_ref[...]
    o_ref[0] = (h * (1.0 + scale) + shift).astype(o_ref.dtype)


def _prenorm(h_res, g, mod3):
    b, s, d = h_res.shape
    ts = _tile(s, 512)
    return pl.pallas_call(
        _prenorm_kernel,
        grid=(b, s // ts),
        in_specs=[
            pl.BlockSpec((1, ts, d), lambda i, j: (i, j, 0)),
            pl.BlockSpec((1, d), lambda i, j: (0, 0)),
            pl.BlockSpec((1, 3, d), lambda i, j: (i, 0, 0)),
        ],
        out_specs=pl.BlockSpec((1, ts, d), lambda i, j: (i, j, 0)),
        out_shape=jax.ShapeDtypeStruct((b, s, d), BF16),
        compiler_params=_params(("arbitrary", "arbitrary")),
        name="prenorm",
    )(h_res, g.reshape(1, d), mod3)


def _inproj_kernel(a_ref, w_hbm, o_ref, stage, wb_scr, sem, *, seg_tiles, layer, n_col_tiles):
    j = pl.program_id(0)
    tn = wb_scr.shape[1]

    def w_copy(col_tile):
        cols = pl.ds(pl.multiple_of(col_tile * tn, tn), tn)
        return pltpu.make_async_copy(w_hbm.at[layer, :, cols], stage, sem)

    @pl.when(pl.program_id(1) == 0)
    def _():
        @pl.when(j == 0)
        def _():
            w_copy(0).start()

        w_copy(j).wait()
        wb_scr[...] = stage[...].astype(BF16)

        @pl.when(j + 1 < n_col_tiles)
        def _():
            w_copy(j + 1).start()

    st = seg_tiles
    is_silu = jnp.logical_or(jnp.logical_and(j >= st, j < 2 * st), jnp.logical_and(j >= 5 * st, j < 6 * st))
    is_q = jnp.logical_and(j >= 2 * st, j < 3 * st)
    is_gate = jnp.logical_or(is_silu, j >= 6 * st)

    tm = a_ref.shape[0]
    cm = tm // INPROJ_ROW_CHUNKS

    def run(epilogue):
        for c in range(INPROJ_ROW_CHUNKS):
            acc = jnp.dot(a_ref[c * cm:(c + 1) * cm, :], wb_scr[...], preferred_element_type=F32)
            o_ref[c * cm:(c + 1) * cm, :] = epilogue(acc).astype(o_ref.dtype)

    @pl.when(jnp.logical_not(is_gate))
    def _():
        scale = jnp.where(is_q, HEAD_DIM ** -0.5 * LOG2E, 1.0).astype(F32)
        run(lambda acc: acc * scale)

    @pl.when(is_gate)
    def _():
        run(lambda acc: _sigmoid(acc) * jnp.where(is_silu, acc, 1.0))


def _in_proj(a, w_all, layer, d_ssm):
    m, k = a.shape
    n = w_all.shape[2]
    tm = _tile(m, 1024)
    tn = _tile(d_ssm, 1024)
    return pl.pallas_call(
        functools.partial(_inproj_kernel, seg_tiles=d_ssm // tn, layer=layer, n_col_tiles=n // tn),
        grid=(n // tn, m // tm),
        in_specs=[
            pl.BlockSpec((tm, k), lambda j, i: (i, 0)),
            pl.BlockSpec(memory_space=pl.ANY),
        ],
        out_specs=pl.BlockSpec((tm, tn), lambda j, i: (i, j)),
        out_shape=jax.ShapeDtypeStruct((m, n), BF16),
        scratch_shapes=[pltpu.VMEM((k, tn), F32), pltpu.VMEM((k, tn), BF16), pltpu.SemaphoreType.DMA(())],
        compiler_params=_params(("arbitrary", "arbitrary")),
        name="in_proj",
    )(a, w_all)


def _s5_kernel(x_ref, lr_ref, li_ref, ldt_ref, ba_ref, bb_ref, ca_ref, cb_ref, d_ref, o_ref,
               t_scr, r_scr, ot_scr, xs_scr, os_scr, *, nst, npb):
    nc = x_ref.shape[0] // CHUNK
    gps = GROUPS_PER_STEP

    @pl.when(pl.program_id(0) == 0)
    def _():
        t_scr[...] = jnp.zeros(t_scr.shape, t_scr.dtype)

    lr = lr_ref[...]
    li = li_ref[...]
    dt = jnp.exp(ldt_ref[...])
    mag = jnp.exp(lr * dt)
    are = mag * jnp.cos(li * dt)
    aim = mag * jnp.sin(li * dt)
    den = lr * lr + li * li
    nre = are - 1.0
    cre = (nre * lr + aim * li) / den
    cim = (aim * lr - nre * li) / den
    b_a = ba_ref[...]
    b_b = bb_ref[...]
    c_a = ca_ref[...]
    c_b = cb_ref[...]
    cre3 = cre[:, None, :]
    cim3 = cim[:, None, :]
    bbv = cre3 * b_a + cim3 * b_b
    bbs = cre3 * b_b - cim3 * b_a

    pre = jnp.ones_like(are)
    pim = jnp.zeros_like(are)
    pows = [(pre, pim)]
    for _ in range(CHUNK):
        pre, pim = are * pre - aim * pim, are * pim + aim * pre
        pows.append((pre, pim))

    same_group = (lax.broadcasted_iota(jnp.int32, (LANES, LANES), 0) // SSM_GROUP
                  == lax.broadcasted_iota(jnp.int32, (LANES, LANES), 1) // SSM_GROUP)
    own_state = (lax.broadcasted_iota(jnp.int32, (LANES, HW), 0) // SSM_GROUP
                 == lax.broadcasted_iota(jnp.int32, (LANES, HW), 1) // SW)
    bbv2 = bbv.reshape(LANES, SW)
    for k in range(CHUNK + 1):
        pr, pi = pows[k]
        z_k = (pr[:, None, :] * c_a + pi[:, None, :] * c_b).reshape(LANES, SW)
        if k < CHUNK:
            kk = lax.dot_general(bbv2, z_k, (((1,), (1,)), ((), ())),
                                 precision=lax.Precision.HIGHEST, preferred_element_type=F32)
            kk = jnp.where(same_group, kk, 0.0).astype(t_scr.dtype)
            for s in range(CHUNK - k):
                t = s + k
                t_scr[s * LANES:(s + 1) * LANES, t * LANES:(t + 1) * LANES] = kk
        if k >= 1:
            ot_scr[(k - 1) * LANES:k * LANES, :] = jnp.where(
                own_state, jnp.concatenate([z_k] * gps, axis=1), 0.0).astype(ot_scr.dtype)
    for s in range(CHUNK):
        pr, pi = pows[CHUNK - 1 - s]
        r_s = (pr[:, None, :] * bbv + pi[:, None, :] * bbs).reshape(LANES, SW)
        r_scr[s * LANES:(s + 1) * LANES, :] = jnp.where(
            own_state, jnp.concatenate([r_s] * gps, axis=1), 0.0).astype(r_scr.dtype)

    first_half = lax.broadcasted_iota(jnp.int32, are.shape, 1) < STATE
    sre, sim = pows[CHUNK]
    steps = []
    for _ in range(nst):
        steps.append((sre, jnp.where(first_half, -sim, sim)))
        sre, sim = sre * sre - sim * sim, 2.0 * sre * sim

    xs_scr[...] = x_ref[...].astype(F32)
    xcat = jnp.concatenate([xs_scr[pl.ds(s, nc, stride=CHUNK), :].astype(BF16) for s in range(CHUNK)], axis=1)

    row_in_batch = jnp.bitwise_and(lax.broadcasted_iota(jnp.int32, (nc, SW), 0), npb - 1)
    r = jnp.dot(xcat, r_scr[...], preferred_element_type=F32)
    h_ins = []
    for g in range(gps):
        h = r[:, g * SW:(g + 1) * SW]
        for k in range(nst):
            d = 1 << k
            hs = jnp.where(row_in_batch >= d, pltpu.roll(h, d, axis=0), 0.0)
            ar, ai = steps[k]
            h = h + ar[g:g + 1, :] * hs + ai[g:g + 1, :] * pltpu.roll(hs, STATE, axis=1)
        h_in = jnp.where(row_in_batch >= 1, pltpu.roll(h, 1, axis=0), 0.0)
        h_ins.append(h_in.astype(BF16))
    hcat = jnp.concatenate(h_ins, axis=1)

    y = jnp.dot(xcat, t_scr[...], preferred_element_type=F32)
    y = y + lax.dot_general(hcat, ot_scr[...], (((1,), (1,)), ((), ())), preferred_element_type=F32)
    y = y + d_ref[0] * xcat.astype(F32)
    gact = _gelu_tanh(y)
    for s in range(CHUNK):
        os_scr[pl.ds(s, nc, stride=CHUNK), :] = gact[:, s * LANES:(s + 1) * LANES]
    o_ref[...] = os_scr[...].astype(o_ref.dtype)


def _s5(proj, lam_re, lam_im, log_dt, b_re, b_im, c_re, c_im, d_skip, nst, npb, d_ssm):
    rows = proj.shape[0]
    g = d_ssm // SSM_GROUP
    gps = GROUPS_PER_STEP
    lr2 = jnp.concatenate([lam_re, lam_re], axis=-1)
    li2 = jnp.concatenate([lam_im, lam_im], axis=-1)
    ldt2 = jnp.broadcast_to(log_dt[:, None], (g, SW))
    brt = jnp.swapaxes(b_re, 1, 2)
    bit = jnp.swapaxes(b_im, 1, 2)
    b_a = jnp.concatenate([brt, bit], axis=-1)
    b_b = jnp.concatenate([-bit, brt], axis=-1)
    c_a = jnp.concatenate([c_re, -c_im], axis=-1)
    c_b = jnp.concatenate([-c_im, -c_re], axis=-1)
    d_lane = jnp.tile(d_skip.reshape(g // gps, 1, LANES), (1, 1, CHUNK))
    vec = pl.BlockSpec((gps, SW), lambda i: (i, 0))
    mat = pl.BlockSpec((gps, SSM_GROUP, SW), lambda i: (i, 0, 0))
    return pl.pallas_call(
        functools.partial(_s5_kernel, nst=nst, npb=npb),
        grid=(g // gps,),
        in_specs=[pl.BlockSpec((rows, LANES), lambda i: (0, i)), vec, vec, vec, mat, mat, mat, mat,
                  pl.BlockSpec((1, 1, XW), lambda i: (i, 0, 0))],
        out_specs=pl.BlockSpec((rows, LANES), lambda i: (0, i)),
        out_shape=jax.ShapeDtypeStruct((rows, d_ssm), BF16),
        scratch_shapes=[
            pltpu.VMEM((XW, XW), BF16),
            pltpu.VMEM((XW, HW), BF16),
            pltpu.VMEM((XW, HW), BF16),
            pltpu.VMEM((rows, LANES), F32),
            pltpu.VMEM((rows, LANES), F32),
        ],
        compiler_params=_params(("arbitrary",)),
        name="s5",
    )(proj, lr2, li2, ldt2, b_a, b_b, c_a, c_b, d_lane)


def _cast_weight_once(w_ref, wb_scr, first):
    @pl.when(first)
    def _():
        wb_scr[...] = w_ref[0].astype(BF16)


def _glu_kernel(a_ref, w_ref, g_ref, z_ref, o_ref, wb_scr):
    _cast_weight_once(w_ref, wb_scr, pl.program_id(1) == 0)
    acc = jnp.dot(a_ref[...], wb_scr[...], preferred_element_type=F32)
    g = g_ref[...].astype(F32)
    o_ref[...] = (g * _sigmoid(acc) * z_ref[...].astype(F32)).astype(o_ref.dtype)


def _glu(g_act, w_all, layer, proj, d_ssm):
    m, k = g_act.shape
    tm = _tile(m, 1024)
    tn = _tile(d_ssm, 1024)
    zoff = d_ssm // tn
    return pl.pallas_call(
        _glu_kernel,
        grid=(d_ssm // tn, m // tm),
        in_specs=[
            pl.BlockSpec((tm, k), lambda j, i: (i, 0)),
            pl.BlockSpec((1, k, tn), lambda j, i: (layer, 0, j)),
            pl.BlockSpec((tm, tn), lambda j, i: (i, j)),
            pl.BlockSpec((tm, tn), lambda j, i: (i, zoff + j)),
        ],
        out_specs=pl.BlockSpec((tm, tn), lambda j, i: (i, j)),
        out_shape=jax.ShapeDtypeStruct((m, d_ssm), BF16),
        scratch_shapes=[pltpu.VMEM((k, tn), BF16)],
        compiler_params=_params(("arbitrary", "arbitrary")),
        name="glu",
    )(g_act, w_all, g_act, proj)


def _bias_kernel(tab_ref, o_ref):
    h = pl.program_id(0)
    blk = o_ref.shape[-1]
    r = lax.broadcasted_iota(jnp.int32, (blk, blk), 0)
    c = lax.broadcasted_iota(jnp.int32, (blk, blk), 1)
    max_exact = N_BUCKETS // 2
    far = tab_ref[h * N_BUCKETS + N_BUCKETS - 1]
    for t in range(2):
        rel = r - c + t * blk
        n = jnp.maximum(rel, 0)
        nf = jnp.maximum(n, 1).astype(F32)
        large = max_exact + (jnp.log(nf / max_exact) / math.log(MAX_DISTANCE / max_exact)
                             * (N_BUCKETS - max_exact)).astype(jnp.int32)
        large = jnp.minimum(large, N_BUCKETS - 1)
        bucket = jnp.where(n < max_exact, n, large)
        val = jnp.zeros((blk, blk), F32)
        for b in range(N_BUCKETS):
            val = jnp.where(bucket == b, tab_ref[h * N_BUCKETS + b], val)
        o_ref[0, t] = jnp.where(rel >= 0, (val - far) * LOG2E, NEG_INF)


def _bias_tiles(rel_bias, n_heads, blk):
    tab = jnp.transpose(rel_bias.astype(F32)).reshape(-1)
    return pl.pallas_call(
        _bias_kernel,
        grid=(n_heads,),
        in_specs=[pl.BlockSpec(memory_space=pltpu.SMEM)],
        out_specs=pl.BlockSpec((1, 2, blk, blk), lambda h: (h, 0, 0, 0)),
        out_shape=jax.ShapeDtypeStruct((n_heads, 2, blk, blk), F32),
        compiler_params=_params(("arbitrary",)),
        name="t5_bias_tiles",
    )(tab)


def _attn_kernel(q_ref, k_ref, v_ref, z_ref, bias_ref, lamv_ref, sg_ref, o_ref,
                 m_scr, l_scr, acc_scr, s_scr, mb_scr, *, lam_init):
    qi = pl.program_id(2)
    blk = q_ref.shape[0]
    q = q_ref[...]
    qs = (q[:, :HEAD_DIM], q[:, HEAD_DIM:])

    m_scr[...] = jnp.full(m_scr.shape, -jnp.inf, F32)
    l_scr[...] = jnp.zeros(l_scr.shape, F32)
    acc_scr[...] = jnp.zeros(acc_scr.shape, F32)

    nlb = blk // LANES

    def produce(j, slot, tile):
        off = pl.multiple_of(j * blk, blk)
        k = k_ref[pl.ds(off, blk), :]
        for mp in range(2):
            s = lax.dot_general(qs[mp], k[:, mp * HEAD_DIM:(mp + 1) * HEAD_DIM],
                                (((1,), (1,)), ((), ())), preferred_element_type=F32)
            if tile is not None:
                s = s + bias_ref[0, tile]
            s_scr[slot, mp] = s
            mb_scr[slot, mp] = jnp.broadcast_to(jnp.max(s, axis=-1, keepdims=True), (blk, LANES))

    def consume(j, slot, late_tile):
        off = pl.multiple_of(j * blk, blk)
        v = v_ref[pl.ds(off, blk), :]
        for mp in range(2):
            s = s_scr[slot, mp]
            if late_tile is None:
                m_blk = mb_scr[slot, mp]
            else:
                s = s + bias_ref[0, late_tile]
                m_blk = jnp.max(s, axis=-1, keepdims=True)
            m_prev = m_scr[mp]
            m_new = jnp.maximum(m_prev, m_blk)
            alpha = jnp.exp2(m_prev - m_new)
            p = jnp.exp2(s - jnp.concatenate([m_new] * nlb, axis=1))
            psum = p[:, :LANES]
            for t in range(1, nlb):
                psum = psum + p[:, t * LANES:(t + 1) * LANES]
            l_scr[mp] = alpha * l_scr[mp] + psum
            acc_scr[mp] = (jnp.concatenate([alpha] * (2 * HEAD_DIM // LANES), axis=1) * acc_scr[mp]
                           + jnp.dot(p.astype(BF16), v, preferred_element_type=F32))
            m_scr[mp] = m_new

    n_far = jnp.maximum(qi - 1, 0)
    n_pairs = n_far // 2
    produce(0, 0, None)

    def pair_body(jj, carry):
        j = 2 * jj
        produce(j + 1, 1, None)
        consume(j, 0, None)
        produce(j + 2, 0, None)
        consume(j + 1, 1, None)
        return carry

    lax.fori_loop(0, n_pairs, pair_body, 0)
    j0 = 2 * n_pairs

    @pl.when(qi == 0)
    def _():
        consume(0, 0, 0)

    @pl.when(jnp.logical_and(qi >= 1, n_far == j0))
    def _():
        produce(j0 + 1, 1, 0)
        consume(j0, 0, 1)
        consume(j0 + 1, 1, None)

    @pl.when(n_far > j0)
    def _():
        produce(j0 + 1, 1, 1)
        consume(j0, 0, None)
        produce(j0 + 2, 0, 0)
        consume(j0 + 1, 1, None)
        consume(j0 + 2, 0, None)

    lv = lamv_ref[...]
    lam = (jnp.exp(jnp.sum(lv[0:1] * lv[1:2], axis=-1, keepdims=True))
           - jnp.exp(jnp.sum(lv[2:3] * lv[3:4], axis=-1, keepdims=True)) + lam_init)
    l0 = jnp.sum(l_scr[0], axis=-1, keepdims=True)
    l1 = jnp.sum(l_scr[1], axis=-1, keepdims=True)
    o = acc_scr[0] / l0 - lam * (acc_scr[1] / l1)
    ms = jnp.mean(o * o, axis=-1, keepdims=True)
    o = o * lax.rsqrt(ms + SUBLN_EPS) * sg_ref[...] * (1.0 - lam_init)
    o_ref[...] = (o * z_ref[...].astype(F32)).astype(o_ref.dtype)


def _attention(proj, bias_tiles, lamv, subln_g, bsz, seq, d_attn, lam_init):
    hw = 2 * HEAD_DIM
    n_heads = d_attn // hw
    blk = _tile(seq, ATT_BLOCK)
    nq = seq // blk
    qoff, koff, voff, zoff = (2 * d_attn // hw, 3 * d_attn // hw, 4 * d_attn // hw, 5 * d_attn // hw)
    return pl.pallas_call(
        functools.partial(_attn_kernel, lam_init=lam_init),
        grid=(bsz, n_heads, nq),
        in_specs=[
            pl.BlockSpec((blk, hw), lambda b, h, i: (b * nq + i, qoff + h)),
            pl.BlockSpec((seq, hw), lambda b, h, i: (b, koff + h)),
            pl.BlockSpec((seq, hw), lambda b, h, i: (b, voff + h)),
            pl.BlockSpec((blk, hw), lambda b, h, i: (b * nq + i, zoff + h)),
            pl.BlockSpec((1, 2, blk, blk), lambda b, h, i: (h, 0, 0, 0)),
            pl.BlockSpec((4, HEAD_DIM), lambda b, h, i: (0, 0)),
            pl.BlockSpec((1, hw), lambda b, h, i: (0, 0)),
        ],
        out_specs=pl.BlockSpec((blk, hw), lambda b, h, i: (b * nq + i, h)),
        out_shape=jax.ShapeDtypeStruct((bsz * seq, d_attn), BF16),
        scratch_shapes=[
            pltpu.VMEM((2, blk, LANES), F32),
            pltpu.VMEM((2, blk, LANES), F32),
            pltpu.VMEM((2, blk, hw), F32),
            pltpu.VMEM((2, 2, blk, blk), F32),
            pltpu.VMEM((2, 2, blk, LANES), F32),
        ],
        compiler_params=_params(("arbitrary", "arbitrary", "arbitrary")),
        name="diff_attention",
    )(proj, proj, proj, proj, bias_tiles, lamv, subln_g.reshape(1, hw))


def _merge_kernel(ys_ref, ya_ref, ws_ref, wa_ref, gs_ref, ga_ref, o_ref, wsb_scr, wab_scr):
    first = pl.program_id(1) == 0
    _cast_weight_once(ws_ref, wsb_scr, first)
    _cast_weight_once(wa_ref, wab_scr, first)
    ps = jnp.dot(ys_ref[...], wsb_scr[...], preferred_element_type=F32)
    pa = jnp.dot(ya_ref[...], wab_scr[...], preferred_element_type=F32)
    o_ref[...] = (gs_ref[...].astype(F32) * ps + ga_ref[...].astype(F32) * pa).astype(o_ref.dtype)


def _out_merge(y_s, y_a, ws_all, wa_all, layer, proj, d_ssm, d_attn):
    m, ks = y_s.shape
    ka = y_a.shape[1]
    d = ws_all.shape[2]
    tm = _tile(m, 1024)
    tn = _tile(d, 512)
    gs_off = (2 * d_ssm + 4 * d_attn) // tn
    ga_off = gs_off + d // tn
    return pl.pallas_call(
        _merge_kernel,
        grid=(d // tn, m // tm),
        in_specs=[
            pl.BlockSpec((tm, ks), lambda j, i: (i, 0)),
            pl.BlockSpec((tm, ka), lambda j, i: (i, 0)),
            pl.BlockSpec((1, ks, tn), lambda j, i: (layer, 0, j)),
            pl.BlockSpec((1, ka, tn), lambda j, i: (layer, 0, j)),
            pl.BlockSpec((tm, tn), lambda j, i: (i, gs_off + j)),
            pl.BlockSpec((tm, tn), lambda j, i: (i, ga_off + j)),
        ],
        out_specs=pl.BlockSpec((tm, tn), lambda j, i: (i, j)),
        out_shape=jax.ShapeDtypeStruct((m, d), BF16),
        scratch_shapes=[pltpu.VMEM((ks, tn), BF16), pltpu.VMEM((ka, tn), BF16)],
        compiler_params=_params(("arbitrary", "arbitrary")),
        name="out_merge",
    )(y_s, y_a, ws_all, wa_all, proj, proj)


def _resid_kernel(a_ref, w_ref, h_ref, m_ref, o_ref, wb_scr):
    _cast_weight_once(w_ref, wb_scr, jnp.logical_and(pl.program_id(1) == 0, pl.program_id(2) == 0))
    acc = jnp.dot(a_ref[...], wb_scr[...], preferred_element_type=F32)
    o_ref[...] = h_ref[...] + m_ref[0, 2:3, :] * acc


def _resid(merged, w_all, layer, h_res2, mod3, bsz, in_place):
    m, k = merged.shape
    d = w_all.shape[2]
    seq = m // bsz
    tm = _tile(seq, 1024)
    tn = _tile(d, 512)
    ns = seq // tm
    return pl.pallas_call(
        _resid_kernel,
        grid=(d // tn, bsz, ns),
        in_specs=[
            pl.BlockSpec((tm, k), lambda j, b, i: (b * ns + i, 0)),
            pl.BlockSpec((1, k, tn), lambda j, b, i: (layer, 0, j)),
            pl.BlockSpec((tm, tn), lambda j, b, i: (b * ns + i, j)),
            pl.BlockSpec((1, 3, tn), lambda j, b, i: (b, 0, j)),
        ],
        out_specs=pl.BlockSpec((tm, tn), lambda j, b, i: (b * ns + i, j)),
        out_shape=jax.ShapeDtypeStruct((m, d), F32),
        input_output_aliases={2: 0} if in_place else {},
        scratch_shapes=[pltpu.VMEM((k, tn), BF16)],
        compiler_params=_params(("arbitrary", "arbitrary", "arbitrary")),
        name="resid",
    )(merged, w_all, h_res2, mod3)


def _final_norm_kernel(x_ref, g_ref, o_ref):
    x = x_ref[...]
    ms = jnp.mean(x * x, axis=-1, keepdims=True)
    o_ref[...] = (x * lax.rsqrt(ms + EPS) * g_ref[...]).astype(o_ref.dtype)


def _final_norm(h2, g, out_dtype):
    m, d = h2.shape
    tm = _tile(m, 512)
    return pl.pallas_call(
        _final_norm_kernel,
        grid=(m // tm,),
        in_specs=[pl.BlockSpec((tm, d), lambda i: (i, 0)), pl.BlockSpec((1, d), lambda i: (0, 0))],
        out_specs=pl.BlockSpec((tm, d), lambda i: (i, 0)),
        out_shape=jax.ShapeDtypeStruct((m, d), out_dtype),
        compiler_params=_params(("arbitrary",)),
        name="final_norm",
    )(h2, g.reshape(1, d))


def kernel(x, c, norm_g, w_ada, b_ada, w_in, ssm_lambda_re, ssm_lambda_im, ssm_log_dt, ssm_b_re, ssm_b_im, ssm_c_re, ssm_c_im, ssm_d, w_glu, lambda_q1, lambda_k1, lambda_q2, lambda_k2, subln_g, w_out_ssm, w_out_attn, w_o, rel_bias, final_g):
    out_dtype = x.dtype
    bsz, seq, d = x.shape
    depth = w_in.shape[0]
    d_ssm = w_glu.shape[1]
    d_attn = w_out_attn.shape[1]
    n_heads = d_attn // (2 * HEAD_DIM)
    npb = seq // CHUNK
    nst = npb.bit_length() - 1
    assert seq % CHUNK == 0 and (1 << nst) == npb, "sequence must be CHUNK * 2^k"
    assert d_ssm % LANES == 0

    c_pad = jnp.zeros((8, d), F32).at[:bsz].set(c.astype(F32))
    mod = _ada_mod(c_pad, w_ada.astype(F32), b_ada.astype(F32))
    bias_tiles = _bias_tiles(rel_bias, n_heads, _tile(seq, ATT_BLOCK))

    h_res = x.astype(F32)
    for l in range(depth):
        mod3 = mod[l, :bsz].reshape(bsz, 3, d)
        hn = _prenorm(h_res, norm_g[l].astype(F32), mod3)
        proj = _in_proj(hn.reshape(bsz * seq, d), w_in.astype(F32), l, d_ssm)

        g_act = _s5(proj, ssm_lambda_re[l].astype(F32), ssm_lambda_im[l].astype(F32), ssm_log_dt[l].astype(F32),
                    ssm_b_re[l].astype(F32), ssm_b_im[l].astype(F32), ssm_c_re[l].astype(F32),
                    ssm_c_im[l].astype(F32), ssm_d[l].astype(F32), nst, npb, d_ssm)
        y_s = _glu(g_act, w_glu.astype(F32), l, proj, d_ssm)

        lam_init = 0.8 - 0.6 * math.exp(-0.3 * l)
        lamv = jnp.stack([lambda_q1[l], lambda_k1[l], lambda_q2[l], lambda_k2[l]]).astype(F32)
        y_a = _attention(proj, bias_tiles, lamv, subln_g[l].astype(F32), bsz, seq, d_attn, lam_init)

        merged = _out_merge(y_s, y_a, w_out_ssm.astype(F32), w_out_attn.astype(F32), l, proj, d_ssm, d_attn)
        h_res = _resid(merged, w_o.astype(F32), l, h_res.reshape(bsz * seq, d), mod3, bsz,
                       in_place=l > 0).reshape(bsz, seq, d)

    out = _final_norm(h_res.reshape(bsz * seq, d), final_g.astype(F32), out_dtype)
    return out.reshape(bsz, seq, d)
```

```python
import functools
import math

import jax
import jax.numpy as jnp
from jax import lax
from jax.experimental import pallas as pl
from jax.experimental.pallas import tpu as pltpu

F32 = jnp.float32
BF16 = jnp.bfloat16

SSM_GROUP = 16
STATE = 64
HEAD_DIM = 128
N_BUCKETS = 32
MAX_DISTANCE = 128
EPS = 1e-6
SUBLN_EPS = 1e-5
NEG_INF = -1e30

LANES = 128
CHUNK = 16
SW = 2 * STATE
GROUPS_PER_STEP = LANES // SSM_GROUP
XW = CHUNK * LANES
HW = GROUPS_PER_STEP * SW
ATT_BLOCK = 512
MERGE_ROW_CHUNKS = 2
RESID_STAGE_PARTS = 4
INPROJ_ROW_CHUNKS = 2
LOG2E = math.log2(math.e)
V7X_VMEM_LIMIT = 56 * 1024 * 1024


def _sigmoid(x):
    return 1.0 / (1.0 + jnp.exp(-x))


def _silu(x):
    return x * _sigmoid(x)


def _gelu_tanh(x):
    return 0.5 * x * (1.0 + jnp.tanh(math.sqrt(2.0 / math.pi) * (x + 0.044715 * (x * x * x))))


def _tile(dim, pref):
    t = min(dim, pref)
    assert dim % t == 0, (dim, pref)
    return t


def _params(sem):
    return pltpu.CompilerParams(dimension_semantics=sem, vmem_limit_bytes=V7X_VMEM_LIMIT)


def _ada_kernel(c_ref, w_ref, b_ref, o_ref):
    c = c_ref[...]
    ca = _silu(c).astype(BF16)
    o_ref[0] = jnp.dot(ca, w_ref[0].astype(BF16), preferred_element_type=F32) + b_ref[0]


def _ada_mod(c_pad, w_ada, b_ada):
    n_layers, d, n3 = w_ada.shape
    rows = c_pad.shape[0]
    tn = _tile(n3, 1024)
    return pl.pallas_call(
        _ada_kernel,
        grid=(n_layers, n3 // tn),
        in_specs=[
            pl.BlockSpec((rows, d), lambda l, j: (0, 0)),
            pl.BlockSpec((1, d, tn), lambda l, j: (l, 0, j)),
            pl.BlockSpec((1, 1, tn), lambda l, j: (l, 0, j)),
        ],
        out_specs=pl.BlockSpec((1, rows, tn), lambda l, j: (l, 0, j)),
        out_shape=jax.ShapeDtypeStruct((n_layers, rows, n3), F32),
        compiler_params=_params(("arbitrary", "arbitrary")),
        name="ada_mod",
    )(c_pad, w_ada, b_ada.reshape(n_layers, 1, n3))


def _prenorm_kernel(x_ref, g_ref, m_ref, o_ref):
    x = x_ref[0]
    ms = jnp.mean(x * x, axis=-1, keepdims=True)
    shift = m_ref[0, 0:1, :]
    scale = m_ref[0, 1:2, :]
    h = x * lax.rsqrt(ms + EPS) * g_ref[...]
    o_ref[0] = (h * (1.0 + scale) + shift).astype(o_ref.dtype)


def _prenorm(h_res, g, mod3):
    b, s, d = h_res.shape
    ts = _tile(s, 512)
    return pl.pallas_call(
        _prenorm_kernel,
        grid=(b, s // ts),
        in_specs=[
            pl.BlockSpec((1, ts, d), lambda i, j: (i, j, 0)),
            pl.BlockSpec((1, d), lambda i, j: (0, 0)),
            pl.BlockSpec((1, 3, d), lambda i, j: (i, 0, 0)),
        ],
        out_specs=pl.BlockSpec((1, ts, d), lambda i, j: (i, j, 0)),
        out_shape=jax.ShapeDtypeStruct((b, s, d), BF16),
        compiler_params=_params(("arbitrary", "arbitrary")),
        name="prenorm",
    )(h_res, g.reshape(1, d), mod3)


def _stage_weights(parts, stage, sem, first, col_tile, n_col_tiles):
    def copy(part, tile):
        return pltpu.make_async_copy(parts[part][0](tile), stage, sem)

    @pl.when(first)
    def _():
        @pl.when(col_tile == 0)
        def _():
            copy(0, 0).start()

        for part in range(len(parts)):
            copy(part, col_tile).wait()
            parts[part][1][...] = stage[...].astype(BF16)
            if part + 1 < len(parts):
                copy(part + 1, col_tile).start()
            else:
                @pl.when(col_tile + 1 < n_col_tiles)
                def _():
                    copy(0, col_tile + 1).start()


def _weight_parts(w_hbm, layer, wb_scr, stage_rows):
    k, tn = wb_scr.shape
    parts = []
    for r0 in range(0, k, stage_rows):
        rows = pl.ds(r0, stage_rows)
        parts.append((lambda tile, rows=rows: w_hbm.at[layer, rows, pl.ds(pl.multiple_of(tile * tn, tn), tn)],
                      wb_scr.at[rows, :]))
    return parts


def _inproj_kernel(a_ref, w_hbm, o_ref, stage, wb_scr, sem, *, seg_tiles, layer, n_col_tiles):
    j = pl.program_id(0)
    _stage_weights(_weight_parts(w_hbm, layer, wb_scr, stage.shape[0]), stage, sem,
                   pl.program_id(1) == 0, j, n_col_tiles)

    st = seg_tiles
    is_silu = jnp.logical_or(jnp.logical_and(j >= st, j < 2 * st), jnp.logical_and(j >= 5 * st, j < 6 * st))
    is_q = jnp.logical_and(j >= 2 * st, j < 3 * st)
    is_gate = jnp.logical_or(is_silu, j >= 6 * st)

    tm = a_ref.shape[0]
    cm = tm // INPROJ_ROW_CHUNKS

    def run(epilogue):
        for c in range(INPROJ_ROW_CHUNKS):
            acc = jnp.dot(a_ref[c * cm:(c + 1) * cm, :], wb_scr[...], preferred_element_type=F32)
            o_ref[c * cm:(c + 1) * cm, :] = epilogue(acc).astype(o_ref.dtype)

    @pl.when(jnp.logical_not(is_gate))
    def _():
        scale = jnp.where(is_q, HEAD_DIM ** -0.5 * LOG2E, 1.0).astype(F32)
        run(lambda acc: acc * scale)

    @pl.when(is_gate)
    def _():
        run(lambda acc: _sigmoid(acc) * jnp.where(is_silu, acc, 1.0))


def _in_proj(a, w_all, layer, d_ssm):
    m, k = a.shape
    n = w_all.shape[2]
    tm = _tile(m, 1024)
    tn = _tile(d_ssm, 1024)
    return pl.pallas_call(
        functools.partial(_inproj_kernel, seg_tiles=d_ssm // tn, layer=layer, n_col_tiles=n // tn),
        grid=(n // tn, m // tm),
        in_specs=[
            pl.BlockSpec((tm, k), lambda j, i: (i, 0)),
            pl.BlockSpec(memory_space=pl.ANY),
        ],
        out_specs=pl.BlockSpec((tm, tn), lambda j, i: (i, j)),
        out_shape=jax.ShapeDtypeStruct((m, n), BF16),
        scratch_shapes=[pltpu.VMEM((k, tn), F32), pltpu.VMEM((k, tn), BF16), pltpu.SemaphoreType.DMA(())],
        compiler_params=_params(("arbitrary", "arbitrary")),
        name="in_proj",
    )(a, w_all)


def _s5_kernel(x_ref, lr_ref, li_ref, ldt_ref, ba_ref, bb_ref, ca_ref, cb_ref, d_ref, o_ref,
               t_scr, r_scr, ot_scr, xs_scr, os_scr, *, nst, npb):
    nc = x_ref.shape[0] // CHUNK
    gps = GROUPS_PER_STEP

    @pl.when(pl.program_id(0) == 0)
    def _():
        t_scr[...] = jnp.zeros(t_scr.shape, t_scr.dtype)

    lr = lr_ref[...]
    li = li_ref[...]
    dt = jnp.exp(ldt_ref[...])
    mag = jnp.exp(lr * dt)
    are = mag * jnp.cos(li * dt)
    aim = mag * jnp.sin(li * dt)
    den = lr * lr + li * li
    nre = are - 1.0
    cre = (nre * lr + aim * li) / den
    cim = (aim * lr - nre * li) / den
    b_a = ba_ref[...]
    b_b = bb_ref[...]
    c_a = ca_ref[...]
    c_b = cb_ref[...]
    cre3 = cre[:, None, :]
    cim3 = cim[:, None, :]
    bbv = cre3 * b_a + cim3 * b_b
    bbs = cre3 * b_b - cim3 * b_a

    pre = jnp.ones_like(are)
    pim = jnp.zeros_like(are)
    pows = [(pre, pim)]
    for _ in range(CHUNK):
        pre, pim = are * pre - aim * pim, are * pim + aim * pre
        pows.append((pre, pim))

    same_group = (lax.broadcasted_iota(jnp.int32, (LANES, LANES), 0) // SSM_GROUP
                  == lax.broadcasted_iota(jnp.int32, (LANES, LANES), 1) // SSM_GROUP)
    own_state = (lax.broadcasted_iota(jnp.int32, (LANES, HW), 0) // SSM_GROUP
                 == lax.broadcasted_iota(jnp.int32, (LANES, HW), 1) // SW)
    bbv2 = bbv.reshape(LANES, SW)
    for k in range(CHUNK + 1):
        pr, pi = pows[k]
        z_k = (pr[:, None, :] * c_a + pi[:, None, :] * c_b).reshape(LANES, SW)
        if k < CHUNK:
            kk = lax.dot_general(bbv2, z_k, (((1,), (1,)), ((), ())),
                                 precision=lax.Precision.HIGHEST, preferred_element_type=F32)
            kk = jnp.where(same_group, kk, 0.0).astype(t_scr.dtype)
            for s in range(CHUNK - k):
                t = s + k
                t_scr[s * LANES:(s + 1) * LANES, t * LANES:(t + 1) * LANES] = kk
        if k >= 1:
            ot_scr[(k - 1) * LANES:k * LANES, :] = jnp.where(
                own_state, jnp.concatenate([z_k] * gps, axis=1), 0.0).astype(ot_scr.dtype)
    for s in range(CHUNK):
        pr, pi = pows[CHUNK - 1 - s]
        r_s = (pr[:, None, :] * bbv + pi[:, None, :] * bbs).reshape(LANES, SW)
        r_scr[s * LANES:(s + 1) * LANES, :] = jnp.where(
            own_state, jnp.concatenate([r_s] * gps, axis=1), 0.0).astype(r_scr.dtype)

    first_half = lax.broadcasted_iota(jnp.int32, are.shape, 1) < STATE
    sre, sim = pows[CHUNK]
    steps = []
    for _ in range(nst):
        steps.append((sre, jnp.where(first_half, -sim, sim)))
        sre, sim = sre * sre - sim * sim, 2.0 * sre * sim

    xs_scr[...] = x_ref[...].astype(F32)
    xcat = jnp.concatenate([xs_scr[pl.ds(s, nc, stride=CHUNK), :].astype(BF16) for s in range(CHUNK)], axis=1)

    row_in_batch = jnp.bitwise_and(lax.broadcasted_iota(jnp.int32, (nc, SW), 0), npb - 1)
    r = jnp.dot(xcat, r_scr[...], preferred_element_type=F32)
    h_ins = []
    for g in range(gps):
        h = r[:, g * SW:(g + 1) * SW]
        for k in range(nst):
            d = 1 << k
            hs = jnp.where(row_in_batch >= d, pltpu.roll(h, d, axis=0), 0.0)
            ar, ai = steps[k]
            h = h + ar[g:g + 1, :] * hs + ai[g:g + 1, :] * pltpu.roll(hs, STATE, axis=1)
        h_in = jnp.where(row_in_batch >= 1, pltpu.roll(h, 1, axis=0), 0.0)
        h_ins.append(h_in.astype(BF16))
    hcat = jnp.concatenate(h_ins, axis=1)

    y = jnp.dot(xcat, t_scr[...], preferred_element_type=F32)
    y = y + lax.dot_general(hcat, ot_scr[...], (((1,), (1,)), ((), ())), preferred_element_type=F32)
    y = y + d_ref[0] * xcat.astype(F32)
    gact = _gelu_tanh(y)
    for s in range(CHUNK):
        os_scr[pl.ds(s, nc, stride=CHUNK), :] = gact[:, s * LANES:(s + 1) * LANES]
    o_ref[...] = os_scr[...].astype(o_ref.dtype)


def _s5(proj, lam_re, lam_im, log_dt, b_re, b_im, c_re, c_im, d_skip, nst, npb, d_ssm):
    rows = proj.shape[0]
    g = d_ssm // SSM_GROUP
    gps = GROUPS_PER_STEP
    lr2 = jnp.concatenate([lam_re, lam_re], axis=-1)
    li2 = jnp.concatenate([lam_im, lam_im], axis=-1)
    ldt2 = jnp.broadcast_to(log_dt[:, None], (g, SW))
    brt = jnp.swapaxes(b_re, 1, 2)
    bit = jnp.swapaxes(b_im, 1, 2)
    b_a = jnp.concatenate([brt, bit], axis=-1)
    b_b = jnp.concatenate([-bit, brt], axis=-1)
    c_a = jnp.concatenate([c_re, -c_im], axis=-1)
    c_b = jnp.concatenate([-c_im, -c_re], axis=-1)
    d_lane = jnp.tile(d_skip.reshape(g // gps, 1, LANES), (1, 1, CHUNK))
    vec = pl.BlockSpec((gps, SW), lambda i: (i, 0))
    mat = pl.BlockSpec((gps, SSM_GROUP, SW), lambda i: (i, 0, 0))
    return pl.pallas_call(
        functools.partial(_s5_kernel, nst=nst, npb=npb),
        grid=(g // gps,),
        in_specs=[pl.BlockSpec((rows, LANES), lambda i: (0, i)), vec, vec, vec, mat, mat, mat, mat,
                  pl.BlockSpec((1, 1, XW), lambda i: (i, 0, 0))],
        out_specs=pl.BlockSpec((rows, LANES), lambda i: (0, i)),
        out_shape=jax.ShapeDtypeStruct((rows, d_ssm), BF16),
        scratch_shapes=[
            pltpu.VMEM((XW, XW), BF16),
            pltpu.VMEM((XW, HW), BF16),
            pltpu.VMEM((XW, HW), BF16),
            pltpu.VMEM((rows, LANES), F32),
            pltpu.VMEM((rows, LANES), F32),
        ],
        compiler_params=_params(("arbitrary",)),
        name="s5",
    )(proj, lr2, li2, ldt2, b_a, b_b, c_a, c_b, d_lane)


def _cast_weight_once(w_ref, wb_scr, first):
    @pl.when(first)
    def _():
        wb_scr[...] = w_ref[0].astype(BF16)


def _glu_kernel(a_ref, w_ref, g_ref, z_ref, o_ref, wb_scr):
    _cast_weight_once(w_ref, wb_scr, pl.program_id(1) == 0)
    acc = jnp.dot(a_ref[...], wb_scr[...], preferred_element_type=F32)
    g = g_ref[...].astype(F32)
    o_ref[...] = (g * _sigmoid(acc) * z_ref[...].astype(F32)).astype(o_ref.dtype)


def _glu(g_act, w_all, layer, proj, d_ssm):
    m, k = g_act.shape
    tm = _tile(m, 1024)
    tn = _tile(d_ssm, 1024)
    zoff = d_ssm // tn
    return pl.pallas_call(
        _glu_kernel,
        grid=(d_ssm // tn, m // tm),
        in_specs=[
            pl.BlockSpec((tm, k), lambda j, i: (i, 0)),
            pl.BlockSpec((1, k, tn), lambda j, i: (layer, 0, j)),
            pl.BlockSpec((tm, tn), lambda j, i: (i, j)),
            pl.BlockSpec((tm, tn), lambda j, i: (i, zoff + j)),
        ],
        out_specs=pl.BlockSpec((tm, tn), lambda j, i: (i, j)),
        out_shape=jax.ShapeDtypeStruct((m, d_ssm), BF16),
        scratch_shapes=[pltpu.VMEM((k, tn), BF16)],
        compiler_params=_params(("arbitrary", "arbitrary")),
        name="glu",
    )(g_act, w_all, g_act, proj)


def _bias_kernel(tab_ref, o_ref):
    h = pl.program_id(0)
    blk = o_ref.shape[-1]
    r = lax.broadcasted_iota(jnp.int32, (blk, blk), 0)
    c = lax.broadcasted_iota(jnp.int32, (blk, blk), 1)
    max_exact = N_BUCKETS // 2
    far = tab_ref[h * N_BUCKETS + N_BUCKETS - 1]
    for t in range(2):
        rel = r - c + t * blk
        n = jnp.maximum(rel, 0)
        nf = jnp.maximum(n, 1).astype(F32)
        large = max_exact + (jnp.log(nf / max_exact) / math.log(MAX_DISTANCE / max_exact)
                             * (N_BUCKETS - max_exact)).astype(jnp.int32)
        large = jnp.minimum(large, N_BUCKETS - 1)
        bucket = jnp.where(n < max_exact, n, large)
        val = jnp.zeros((blk, blk), F32)
        for b in range(N_BUCKETS):
            val = jnp.where(bucket == b, tab_ref[h * N_BUCKETS + b], val)
        o_ref[0, t] = jnp.where(rel >= 0, (val - far) * LOG2E, NEG_INF)


def _bias_tiles(rel_bias, n_heads, blk):
    tab = jnp.transpose(rel_bias.astype(F32)).reshape(-1)
    return pl.pallas_call(
        _bias_kernel,
        grid=(n_heads,),
        in_specs=[pl.BlockSpec(memory_space=pltpu.SMEM)],
        out_specs=pl.BlockSpec((1, 2, blk, blk), lambda h: (h, 0, 0, 0)),
        out_shape=jax.ShapeDtypeStruct((n_heads, 2, blk, blk), F32),
        compiler_params=_params(("arbitrary",)),
        name="t5_bias_tiles",
    )(tab)


def _attn_kernel(q_ref, k_ref, v_ref, z_ref, bias_ref, lamv_ref, sg_ref, o_ref,
                 m_scr, l_scr, acc_scr, s_scr, mb_scr, *, lam_init):
    blk = s_scr.shape[-1]
    nlb = blk // LANES
    lv = lamv_ref[...]
    lam = (jnp.exp(jnp.sum(lv[0:1] * lv[1:2], axis=-1, keepdims=True))
           - jnp.exp(jnp.sum(lv[2:3] * lv[3:4], axis=-1, keepdims=True)) + lam_init)

    def q_block(qi, carry):
        _attn_q_block(qi, lam, q_ref, k_ref, v_ref, z_ref, bias_ref, sg_ref, o_ref,
                      m_scr, l_scr, acc_scr, s_scr, mb_scr, blk=blk, nlb=nlb, lam_init=lam_init)
        return carry

    lax.fori_loop(0, q_ref.shape[0] // blk, q_block, 0)


def _attn_q_block(qi, lam, q_ref, k_ref, v_ref, z_ref, bias_ref, sg_ref, o_ref,
                  m_scr, l_scr, acc_scr, s_scr, mb_scr, *, blk, nlb, lam_init):
    q_rows = pl.ds(pl.multiple_of(qi * blk, blk), blk)
    q = q_ref[q_rows, :]
    qs = (q[:, :HEAD_DIM], q[:, HEAD_DIM:])

    m_scr[...] = jnp.full(m_scr.shape, -jnp.inf, F32)
    l_scr[...] = jnp.zeros(l_scr.shape, F32)
    acc_scr[...] = jnp.zeros(acc_scr.shape, F32)

    def produce(j, slot, tile):
        off = pl.multiple_of(j * blk, blk)
        k = k_ref[pl.ds(off, blk), :]
        for mp in range(2):
            s = lax.dot_general(qs[mp], k[:, mp * HEAD_DIM:(mp + 1) * HEAD_DIM],
                                (((1,), (1,)), ((), ())), preferred_element_type=F32)
            if tile is not None:
                s = s + bias_ref[0, tile]
            s_scr[slot, mp] = s
            mb_scr[slot, mp] = jnp.broadcast_to(jnp.max(s, axis=-1, keepdims=True), (blk, LANES))

    def consume(j, slot, late_tile):
        off = pl.multiple_of(j * blk, blk)
        v = v_ref[pl.ds(off, blk), :]
        for mp in range(2):
            s = s_scr[slot, mp]
            if late_tile is None:
                m_blk = mb_scr[slot, mp]
            else:
                s = s + bias_ref[0, late_tile]
                m_blk = jnp.max(s, axis=-1, keepdims=True)
            m_prev = m_scr[mp]
            m_new = jnp.maximum(m_prev, m_blk)
            alpha = jnp.exp2(m_prev - m_new)
            p = jnp.exp2(s - jnp.concatenate([m_new] * nlb, axis=1))
            psum = p[:, :LANES]
            for t in range(1, nlb):
                psum = psum + p[:, t * LANES:(t + 1) * LANES]
            l_scr[mp] = alpha * l_scr[mp] + psum
            acc_scr[mp] = (jnp.concatenate([alpha] * (2 * HEAD_DIM // LANES), axis=1) * acc_scr[mp]
                           + jnp.dot(p.astype(BF16), v, preferred_element_type=F32))
            m_scr[mp] = m_new

    n_far = jnp.maximum(qi - 1, 0)
    n_pairs = n_far // 2
    produce(0, 0, None)

    def pair_body(jj, carry):
        j = 2 * jj
        produce(j + 1, 1, None)
        consume(j, 0, None)
        produce(j + 2, 0, None)
        consume(j + 1, 1, None)
        return carry

    lax.fori_loop(0, n_pairs, pair_body, 0)
    j0 = 2 * n_pairs

    @pl.when(qi == 0)
    def _():
        consume(0, 0, 0)

    @pl.when(jnp.logical_and(qi >= 1, n_far == j0))
    def _():
        produce(j0 + 1, 1, 0)
        consume(j0, 0, 1)
        consume(j0 + 1, 1, None)

    @pl.when(n_far > j0)
    def _():
        produce(j0 + 1, 1, 1)
        consume(j0, 0, None)
        produce(j0 + 2, 0, 0)
        consume(j0 + 1, 1, None)
        consume(j0 + 2, 0, None)

    l0 = jnp.sum(l_scr[0], axis=-1, keepdims=True)
    l1 = jnp.sum(l_scr[1], axis=-1, keepdims=True)
    o = acc_scr[0] / l0 - lam * (acc_scr[1] / l1)
    ms = jnp.mean(o * o, axis=-1, keepdims=True)
    o = o * lax.rsqrt(ms + SUBLN_EPS) * sg_ref[...] * (1.0 - lam_init)
    o_ref[q_rows, :] = (o * z_ref[q_rows, :].astype(F32)).astype(o_ref.dtype)


def _attention(proj, bias_tiles, lamv, subln_g, bsz, seq, d_attn, lam_init):
    hw = 2 * HEAD_DIM
    n_heads = d_attn // hw
    blk = _tile(seq, ATT_BLOCK)
    qoff, koff, voff, zoff = (2 * d_attn // hw, 3 * d_attn // hw, 4 * d_attn // hw, 5 * d_attn // hw)
    return pl.pallas_call(
        functools.partial(_attn_kernel, lam_init=lam_init),
        grid=(bsz, n_heads),
        in_specs=[
            pl.BlockSpec((seq, hw), lambda b, h: (b, qoff + h)),
            pl.BlockSpec((seq, hw), lambda b, h: (b, koff + h)),
            pl.BlockSpec((seq, hw), lambda b, h: (b, voff + h)),
            pl.BlockSpec((seq, hw), lambda b, h: (b, zoff + h)),
            pl.BlockSpec((1, 2, blk, blk), lambda b, h: (h, 0, 0, 0)),
            pl.BlockSpec((4, HEAD_DIM), lambda b, h: (0, 0)),
            pl.BlockSpec((1, hw), lambda b, h: (0, 0)),
        ],
        out_specs=pl.BlockSpec((seq, hw), lambda b, h: (b, h)),
        out_shape=jax.ShapeDtypeStruct((bsz * seq, d_attn), BF16),
        scratch_shapes=[
            pltpu.VMEM((2, blk, LANES), F32),
            pltpu.VMEM((2, blk, LANES), F32),
            pltpu.VMEM((2, blk, hw), F32),
            pltpu.VMEM((2, 2, blk, blk), F32),
            pltpu.VMEM((2, 2, blk, LANES), F32),
        ],
        compiler_params=_params(("arbitrary", "arbitrary")),
        name="diff_attention",
    )(proj, proj, proj, proj, bias_tiles, lamv, subln_g.reshape(1, hw))


def _merge_kernel(ys_ref, ya_ref, ws_hbm, wa_hbm, gs_ref, ga_ref, o_ref, stage, wsb_scr, wab_scr, sem,
                  *, layer, n_col_tiles):
    parts = (_weight_parts(ws_hbm, layer, wsb_scr, stage.shape[0])
             + _weight_parts(wa_hbm, layer, wab_scr, stage.shape[0]))
    _stage_weights(parts, stage, sem, pl.program_id(1) == 0, pl.program_id(0), n_col_tiles)
    tm = ys_ref.shape[0]
    cm = tm // MERGE_ROW_CHUNKS
    for c in range(MERGE_ROW_CHUNKS):
        rows = slice(c * cm, (c + 1) * cm)
        ps = jnp.dot(ys_ref[rows, :], wsb_scr[...], preferred_element_type=F32)
        pa = jnp.dot(ya_ref[rows, :], wab_scr[...], preferred_element_type=F32)
        o_ref[rows, :] = (gs_ref[rows, :].astype(F32) * ps + ga_ref[rows, :].astype(F32) * pa).astype(o_ref.dtype)


def _out_merge(y_s, y_a, ws_all, wa_all, layer, proj, d_ssm, d_attn):
    m, ks = y_s.shape
    ka = y_a.shape[1]
    assert ks == ka, "the two branch widths share one staging buffer"
    d = ws_all.shape[2]
    tm = _tile(m, 1024)
    tn = _tile(d, 1024)
    gs_off = (2 * d_ssm + 4 * d_attn) // tn
    ga_off = gs_off + d // tn
    return pl.pallas_call(
        functools.partial(_merge_kernel, layer=layer, n_col_tiles=d // tn),
        grid=(d // tn, m // tm),
        in_specs=[
            pl.BlockSpec((tm, ks), lambda j, i: (i, 0)),
            pl.BlockSpec((tm, ka), lambda j, i: (i, 0)),
            pl.BlockSpec(memory_space=pl.ANY),
            pl.BlockSpec(memory_space=pl.ANY),
            pl.BlockSpec((tm, tn), lambda j, i: (i, gs_off + j)),
            pl.BlockSpec((tm, tn), lambda j, i: (i, ga_off + j)),
        ],
        out_specs=pl.BlockSpec((tm, tn), lambda j, i: (i, j)),
        out_shape=jax.ShapeDtypeStruct((m, d), BF16),
        scratch_shapes=[pltpu.VMEM((ks, tn), F32), pltpu.VMEM((ks, tn), BF16), pltpu.VMEM((ka, tn), BF16),
                        pltpu.SemaphoreType.DMA(())],
        compiler_params=_params(("arbitrary", "arbitrary")),
        name="out_merge",
    )(y_s, y_a, ws_all, wa_all, proj, proj)


def _resid_kernel(a_ref, w_hbm, h_ref, m_ref, o_ref, stage, wb_scr, sem, *, layer, n_col_tiles):
    first = jnp.logical_and(pl.program_id(1) == 0, pl.program_id(2) == 0)
    _stage_weights(_weight_parts(w_hbm, layer, wb_scr, stage.shape[0]), stage, sem,
                   first, pl.program_id(0), n_col_tiles)
    acc = jnp.dot(a_ref[...], wb_scr[...], preferred_element_type=F32)
    o_ref[...] = h_ref[...] + m_ref[0, 2:3, :] * acc


def _resid(merged, w_all, layer, h_res2, mod3, bsz, in_place):
    m, k = merged.shape
    d = w_all.shape[2]
    seq = m // bsz
    tm = _tile(seq, 1024)
    tn = _tile(d, 1024)
    ns = seq // tm
    stage_rows = k // RESID_STAGE_PARTS
    return pl.pallas_call(
        functools.partial(_resid_kernel, layer=layer, n_col_tiles=d // tn),
        grid=(d // tn, bsz, ns),
        in_specs=[
            pl.BlockSpec((tm, k), lambda j, b, i: (b * ns + i, 0)),
            pl.BlockSpec(memory_space=pl.ANY),
            pl.BlockSpec((tm, tn), lambda j, b, i: (b * ns + i, j)),
            pl.BlockSpec((1, 3, tn), lambda j, b, i: (b, 0, j)),
        ],
        out_specs=pl.BlockSpec((tm, tn), lambda j, b, i: (b * ns + i, j)),
        out_shape=jax.ShapeDtypeStruct((m, d), F32),
        input_output_aliases={2: 0} if in_place else {},
        scratch_shapes=[pltpu.VMEM((stage_rows, tn), F32), pltpu.VMEM((k, tn), BF16), pltpu.SemaphoreType.DMA(())],
        compiler_params=_params(("arbitrary", "arbitrary", "arbitrary")),
        name="resid",
    )(merged, w_all, h_res2, mod3)


def _final_norm_kernel(x_ref, g_ref, o_ref):
    x = x_ref[...]
    ms = jnp.mean(x * x, axis=-1, keepdims=True)
    o_ref[...] = (x * lax.rsqrt(ms + EPS) * g_ref[...]).astype(o_ref.dtype)


def _final_norm(h2, g, out_dtype):
    m, d = h2.shape
    tm = _tile(m, 512)
    return pl.pallas_call(
        _final_norm_kernel,
        grid=(m // tm,),
        in_specs=[pl.BlockSpec((tm, d), lambda i: (i, 0)), pl.BlockSpec((1, d), lambda i: (0, 0))],
        out_specs=pl.BlockSpec((tm, d), lambda i: (i, 0)),
        out_shape=jax.ShapeDtypeStruct((m, d), out_dtype),
        compiler_params=_params(("arbitrary",)),
        name="final_norm",
    )(h2, g.reshape(1, d))


def kernel(x, c, norm_g, w_ada, b_ada, w_in, ssm_lambda_re, ssm_lambda_im, ssm_log_dt, ssm_b_re, ssm_b_im, ssm_c_re, ssm_c_im, ssm_d, w_glu, lambda_q1, lambda_k1, lambda_q2, lambda_k2, subln_g, w_out_ssm, w_out_attn, w_o, rel_bias, final_g):
    out_dtype = x.dtype
    bsz, seq, d = x.shape
    depth = w_in.shape[0]
    d_ssm = w_glu.shape[1]
    d_attn = w_out_attn.shape[1]
    n_heads = d_attn // (2 * HEAD_DIM)
    npb = seq // CHUNK
    nst = npb.bit_length() - 1
    assert seq % CHUNK == 0 and (1 << nst) == npb, "sequence must be CHUNK * 2^k"
    assert d_ssm % LANES == 0

    c_pad = jnp.zeros((8, d), F32).at[:bsz].set(c.astype(F32))
    mod = _ada_mod(c_pad, w_ada.astype(F32), b_ada.astype(F32))
    bias_tiles = _bias_tiles(rel_bias, n_heads, _tile(seq, ATT_BLOCK))

    h_res = x.astype(F32)
    for l in range(depth):
        mod3 = mod[l, :bsz].reshape(bsz, 3, d)
        hn = _prenorm(h_res, norm_g[l].astype(F32), mod3)
        proj = _in_proj(hn.reshape(bsz * seq, d), w_in.astype(F32), l, d_ssm)

        g_act = _s5(proj, ssm_lambda_re[l].astype(F32), ssm_lambda_im[l].astype(F32), ssm_log_dt[l].astype(F32),
                    ssm_b_re[l].astype(F32), ssm_b_im[l].astype(F32), ssm_c_re[l].astype(F32),
                    ssm_c_im[l].astype(F32), ssm_d[l].astype(F32), nst, npb, d_ssm)
        y_s = _glu(g_act, w_glu.astype(F32), l, proj, d_ssm)

        lam_init = 0.8 - 0.6 * math.exp(-0.3 * l)
        lamv = jnp.stack([lambda_q1[l], lambda_k1[l], lambda_q2[l], lambda_k2[l]]).astype(F32)
        y_a = _attention(proj, bias_tiles, lamv, subln_g[l].astype(F32), bsz, seq, d_attn, lam_init)

        merged = _out_merge(y_s, y_a, w_out_ssm.astype(F32), w_out_attn.astype(F32), l, proj, d_ssm, d_attn)
        h_res = _resid(merged, w_o.astype(F32), l, h_res.reshape(bsz * seq, d), mod3, bsz,
                       in_place=l > 0).reshape(bsz, seq, d)

    out = _final_norm(h_res.reshape(bsz * seq, d), final_g.astype(F32), out_dtype)
    return out.reshape(bsz, seq, d)
```

```python
import functools
import math

import jax
import jax.numpy as jnp
from jax import lax
from jax.experimental import pallas as pl
from jax.experimental.pallas import tpu as pltpu

F32 = jnp.float32
BF16 = jnp.bfloat16

SSM_GROUP = 16
STATE = 64
HEAD_DIM = 128
N_BUCKETS = 32
MAX_DISTANCE = 128
EPS = 1e-6
SUBLN_EPS = 1e-5
NEG_INF = -1e30

LANES = 128
CHUNK = 16
SW = 2 * STATE
GROUPS_PER_STEP = LANES // SSM_GROUP
XW = CHUNK * LANES
HW = GROUPS_PER_STEP * SW
ATT_BLOCK = 512
MERGE_ROW_CHUNKS = 2
RESID_STAGE_PARTS = 4
INPROJ_ROW_CHUNKS = 4
LOG2E = math.log2(math.e)
V7X_VMEM_LIMIT = 56 * 1024 * 1024


def _sigmoid(x):
    return 0.5 * jnp.tanh(0.5 * x) + 0.5


def _silu(x):
    return x * _sigmoid(x)


def _gelu_tanh(x):
    return 0.5 * x * (1.0 + jnp.tanh(math.sqrt(2.0 / math.pi) * (x + 0.044715 * (x * x * x))))


def _tile(dim, pref):
    t = min(dim, pref)
    assert dim % t == 0, (dim, pref)
    return t


def _params(sem):
    return pltpu.CompilerParams(dimension_semantics=sem, vmem_limit_bytes=V7X_VMEM_LIMIT)


def _ada_kernel(c_ref, w_ref, b_ref, o_ref):
    c = c_ref[...]
    ca = _silu(c).astype(BF16)
    o_ref[0] = jnp.dot(ca, w_ref[0].astype(BF16), preferred_element_type=F32) + b_ref[0]


def _ada_mod(c_pad, w_ada, b_ada):
    n_layers, d, n3 = w_ada.shape
    rows = c_pad.shape[0]
    tn = _tile(n3, 1024)
    return pl.pallas_call(
        _ada_kernel,
        grid=(n_layers, n3 // tn),
        in_specs=[
            pl.BlockSpec((rows, d), lambda l, j: (0, 0)),
            pl.BlockSpec((1, d, tn), lambda l, j: (l, 0, j)),
            pl.BlockSpec((1, 1, tn), lambda l, j: (l, 0, j)),
        ],
        out_specs=pl.BlockSpec((1, rows, tn), lambda l, j: (l, 0, j)),
        out_shape=jax.ShapeDtypeStruct((n_layers, rows, n3), F32),
        compiler_params=_params(("arbitrary", "arbitrary")),
        name="ada_mod",
    )(c_pad, w_ada, b_ada.reshape(n_layers, 1, n3))


def _prenorm_kernel(x_ref, g_ref, m_ref, o_ref):
    x = x_ref[0]
    ms = jnp.mean(x * x, axis=-1, keepdims=True)
    shift = m_ref[0, 0:1, :]
    scale = m_ref[0, 1:2, :]
    h = x * lax.rsqrt(ms + EPS) * g_ref[...]
    o_ref[0] = (h * (1.0 + scale) + shift).astype(o_ref.dtype)


def _prenorm(h_res, g, mod3):
    b, s, d = h_res.shape
    ts = _tile(s, 512)
    return pl.pallas_call(
        _prenorm_kernel,
        grid=(b, s // ts),
        in_specs=[
            pl.BlockSpec((1, ts, d), lambda i, j: (i, j, 0)),
            pl.BlockSpec((1, d), lambda i, j: (0, 0)),
            pl.BlockSpec((1, 3, d), lambda i, j: (i, 0, 0)),
        ],
        out_specs=pl.BlockSpec((1, ts, d), lambda i, j: (i, j, 0)),
        out_shape=jax.ShapeDtypeStruct((b, s, d), BF16),
        compiler_params=_params(("arbitrary", "arbitrary")),
        name="prenorm",
    )(h_res, g.reshape(1, d), mod3)


def _stage_weights(parts, stage, sem, first, col_tile, n_col_tiles):
    def copy(part, tile):
        return pltpu.make_async_copy(parts[part][0](tile), stage, sem)

    @pl.when(first)
    def _():
        @pl.when(col_tile == 0)
        def _():
            copy(0, 0).start()

        for part in range(len(parts)):
            copy(part, col_tile).wait()
            parts[part][1][...] = stage[...].astype(BF16)
            if part + 1 < len(parts):
                copy(part + 1, col_tile).start()
            else:
                @pl.when(col_tile + 1 < n_col_tiles)
                def _():
                    copy(0, col_tile + 1).start()


def _weight_parts(w_hbm, layer, wb_scr, stage_rows):
    k, tn = wb_scr.shape
    parts = []
    for r0 in range(0, k, stage_rows):
        rows = pl.ds(r0, stage_rows)
        parts.append((lambda tile, rows=rows: w_hbm.at[layer, rows, pl.ds(pl.multiple_of(tile * tn, tn), tn)],
                      wb_scr.at[rows, :]))
    return parts


def _inproj_kernel(a_ref, w_hbm, o_ref, stage, wb_scr, sem, *, seg_tiles, layer, n_col_tiles):
    j = pl.program_id(0)
    _stage_weights(_weight_parts(w_hbm, layer, wb_scr, stage.shape[0]), stage, sem,
                   pl.program_id(1) == 0, j, n_col_tiles)

    st = seg_tiles
    is_silu = jnp.logical_or(jnp.logical_and(j >= st, j < 2 * st), jnp.logical_and(j >= 5 * st, j < 6 * st))
    is_q = jnp.logical_and(j >= 2 * st, j < 3 * st)
    is_gate = jnp.logical_or(is_silu, j >= 6 * st)

    tm = a_ref.shape[0]
    cm = tm // INPROJ_ROW_CHUNKS

    def run(epilogue):
        for c in range(INPROJ_ROW_CHUNKS):
            acc = jnp.dot(a_ref[c * cm:(c + 1) * cm, :], wb_scr[...], preferred_element_type=F32)
            o_ref[c * cm:(c + 1) * cm, :] = epilogue(acc).astype(o_ref.dtype)

    @pl.when(jnp.logical_not(is_gate))
    def _():
        scale = jnp.where(is_q, HEAD_DIM ** -0.5 * LOG2E, 1.0).astype(F32)
        run(lambda acc: acc * scale)

    @pl.when(is_gate)
    def _():
        run(lambda acc: _sigmoid(acc) * jnp.where(is_silu, acc, 1.0))


def _in_proj(a, w_all, layer, d_ssm):
    m, k = a.shape
    n = w_all.shape[2]
    tm = _tile(m, 1024)
    tn = _tile(d_ssm, 1024)
    return pl.pallas_call(
        functools.partial(_inproj_kernel, seg_tiles=d_ssm // tn, layer=layer, n_col_tiles=n // tn),
        grid=(n // tn, m // tm),
        in_specs=[
            pl.BlockSpec((tm, k), lambda j, i: (i, 0)),
            pl.BlockSpec(memory_space=pl.ANY),
        ],
        out_specs=pl.BlockSpec((tm, tn), lambda j, i: (i, j)),
        out_shape=jax.ShapeDtypeStruct((m, n), BF16),
        scratch_shapes=[pltpu.VMEM((k, tn), F32), pltpu.VMEM((k, tn), BF16), pltpu.SemaphoreType.DMA(())],
        compiler_params=_params(("arbitrary", "arbitrary")),
        name="in_proj",
    )(a, w_all)


def _s5_kernel(x_ref, lr_ref, li_ref, ldt_ref, ba_ref, bb_ref, ca_ref, cb_ref, d_ref, o_ref,
               t_scr, r_scr, ot_scr, xs_scr, os_scr, *, nst, npb):
    nc = x_ref.shape[0] // CHUNK
    gps = GROUPS_PER_STEP

    @pl.when(pl.program_id(0) == 0)
    def _():
        t_scr[...] = jnp.zeros(t_scr.shape, t_scr.dtype)

    lr = lr_ref[...]
    li = li_ref[...]
    dt = jnp.exp(ldt_ref[...])
    mag = jnp.exp(lr * dt)
    are = mag * jnp.cos(li * dt)
    aim = mag * jnp.sin(li * dt)
    den = lr * lr + li * li
    nre = are - 1.0
    cre = (nre * lr + aim * li) / den
    cim = (aim * lr - nre * li) / den
    b_a = ba_ref[...]
    b_b = bb_ref[...]
    c_a = ca_ref[...]
    c_b = cb_ref[...]
    cre3 = cre[:, None, :]
    cim3 = cim[:, None, :]
    bbv = cre3 * b_a + cim3 * b_b
    bbs = cre3 * b_b - cim3 * b_a

    pre = jnp.ones_like(are)
    pim = jnp.zeros_like(are)
    pows = [(pre, pim)]
    for _ in range(CHUNK):
        pre, pim = are * pre - aim * pim, are * pim + aim * pre
        pows.append((pre, pim))

    same_group = (lax.broadcasted_iota(jnp.int32, (LANES, LANES), 0) // SSM_GROUP
                  == lax.broadcasted_iota(jnp.int32, (LANES, LANES), 1) // SSM_GROUP)
    own_state = (lax.broadcasted_iota(jnp.int32, (LANES, HW), 0) // SSM_GROUP
                 == lax.broadcasted_iota(jnp.int32, (LANES, HW), 1) // SW)
    bbv2 = bbv.reshape(LANES, SW)
    for k in range(CHUNK + 1):
        pr, pi = pows[k]
        z_k = (pr[:, None, :] * c_a + pi[:, None, :] * c_b).reshape(LANES, SW)
        if k < CHUNK:
            kk = lax.dot_general(bbv2, z_k, (((1,), (1,)), ((), ())),
                                 precision=lax.Precision.HIGHEST, preferred_element_type=F32)
            kk = jnp.where(same_group, kk, 0.0).astype(t_scr.dtype)
            for s in range(CHUNK - k):
                t = s + k
                t_scr[s * LANES:(s + 1) * LANES, t * LANES:(t + 1) * LANES] = kk
        if k >= 1:
            ot_scr[(k - 1) * LANES:k * LANES, :] = jnp.where(
                own_state, jnp.concatenate([z_k] * gps, axis=1), 0.0).astype(ot_scr.dtype)
    for s in range(CHUNK):
        pr, pi = pows[CHUNK - 1 - s]
        r_s = (pr[:, None, :] * bbv + pi[:, None, :] * bbs).reshape(LANES, SW)
        r_scr[s * LANES:(s + 1) * LANES, :] = jnp.where(
            own_state, jnp.concatenate([r_s] * gps, axis=1), 0.0).astype(r_scr.dtype)

    first_half = lax.broadcasted_iota(jnp.int32, are.shape, 1) < STATE
    sre, sim = pows[CHUNK]
    steps = []
    for _ in range(nst):
        steps.append((sre, jnp.where(first_half, -sim, sim)))
        sre, sim = sre * sre - sim * sim, 2.0 * sre * sim

    xs_scr[...] = x_ref[...].astype(F32)
    xcat = jnp.concatenate([xs_scr[pl.ds(s, nc, stride=CHUNK), :].astype(BF16) for s in range(CHUNK)], axis=1)

    row_in_batch = jnp.bitwise_and(lax.broadcasted_iota(jnp.int32, (nc, SW), 0), npb - 1)
    r = jnp.dot(xcat, r_scr[...], preferred_element_type=F32)
    h_ins = []
    for g in range(gps):
        h = r[:, g * SW:(g + 1) * SW]
        for k in range(nst):
            d = 1 << k
            hs = jnp.where(row_in_batch >= d, pltpu.roll(h, d, axis=0), 0.0)
            ar, ai = steps[k]
            h = h + ar[g:g + 1, :] * hs + ai[g:g + 1, :] * pltpu.roll(hs, STATE, axis=1)
        h_in = jnp.where(row_in_batch >= 1, pltpu.roll(h, 1, axis=0), 0.0)
        h_ins.append(h_in.astype(BF16))
    hcat = jnp.concatenate(h_ins, axis=1)

    y = jnp.dot(xcat, t_scr[...], preferred_element_type=F32)
    y = y + lax.dot_general(hcat, ot_scr[...], (((1,), (1,)), ((), ())), preferred_element_type=F32)
    y = y + d_ref[0] * xcat.astype(F32)
    gact = _gelu_tanh(y)
    for s in range(CHUNK):
        os_scr[pl.ds(s, nc, stride=CHUNK), :] = gact[:, s * LANES:(s + 1) * LANES]
    o_ref[...] = os_scr[...].astype(o_ref.dtype)


def _s5(proj, lam_re, lam_im, log_dt, b_re, b_im, c_re, c_im, d_skip, nst, npb, d_ssm):
    rows = proj.shape[0]
    g = d_ssm // SSM_GROUP
    gps = GROUPS_PER_STEP
    lr2 = jnp.concatenate([lam_re, lam_re], axis=-1)
    li2 = jnp.concatenate([lam_im, lam_im], axis=-1)
    ldt2 = jnp.broadcast_to(log_dt[:, None], (g, SW))
    brt = jnp.swapaxes(b_re, 1, 2)
    bit = jnp.swapaxes(b_im, 1, 2)
    b_a = jnp.concatenate([brt, bit], axis=-1)
    b_b = jnp.concatenate([-bit, brt], axis=-1)
    c_a = jnp.concatenate([c_re, -c_im], axis=-1)
    c_b = jnp.concatenate([-c_im, -c_re], axis=-1)
    d_lane = jnp.tile(d_skip.reshape(g // gps, 1, LANES), (1, 1, CHUNK))
    vec = pl.BlockSpec((gps, SW), lambda i: (i, 0))
    mat = pl.BlockSpec((gps, SSM_GROUP, SW), lambda i: (i, 0, 0))
    return pl.pallas_call(
        functools.partial(_s5_kernel, nst=nst, npb=npb),
        grid=(g // gps,),
        in_specs=[pl.BlockSpec((rows, LANES), lambda i: (0, i)), vec, vec, vec, mat, mat, mat, mat,
                  pl.BlockSpec((1, 1, XW), lambda i: (i, 0, 0))],
        out_specs=pl.BlockSpec((rows, LANES), lambda i: (0, i)),
        out_shape=jax.ShapeDtypeStruct((rows, d_ssm), BF16),
        scratch_shapes=[
            pltpu.VMEM((XW, XW), BF16),
            pltpu.VMEM((XW, HW), BF16),
            pltpu.VMEM((XW, HW), BF16),
            pltpu.VMEM((rows, LANES), F32),
            pltpu.VMEM((rows, LANES), F32),
        ],
        compiler_params=_params(("arbitrary",)),
        name="s5",
    )(proj, lr2, li2, ldt2, b_a, b_b, c_a, c_b, d_lane)


def _cast_weight_once(w_ref, wb_scr, first):
    @pl.when(first)
    def _():
        wb_scr[...] = w_ref[0].astype(BF16)


def _glu_kernel(a_ref, w_ref, g_ref, z_ref, o_ref, wb_scr):
    _cast_weight_once(w_ref, wb_scr, pl.program_id(1) == 0)
    acc = jnp.dot(a_ref[...], wb_scr[...], preferred_element_type=F32)
    g = g_ref[...].astype(F32)
    o_ref[...] = (g * _sigmoid(acc) * z_ref[...].astype(F32)).astype(o_ref.dtype)


def _glu(g_act, w_all, layer, proj, d_ssm):
    m, k = g_act.shape
    tm = _tile(m, 1024)
    tn = _tile(d_ssm, 1024)
    zoff = d_ssm // tn
    return pl.pallas_call(
        _glu_kernel,
        grid=(d_ssm // tn, m // tm),
        in_specs=[
            pl.BlockSpec((tm, k), lambda j, i: (i, 0)),
            pl.BlockSpec((1, k, tn), lambda j, i: (layer, 0, j)),
            pl.BlockSpec((tm, tn), lambda j, i: (i, j)),
            pl.BlockSpec((tm, tn), lambda j, i: (i, zoff + j)),
        ],
        out_specs=pl.BlockSpec((tm, tn), lambda j, i: (i, j)),
        out_shape=jax.ShapeDtypeStruct((m, d_ssm), BF16),
        scratch_shapes=[pltpu.VMEM((k, tn), BF16)],
        compiler_params=_params(("arbitrary", "arbitrary")),
        name="glu",
    )(g_act, w_all, g_act, proj)


def _bias_kernel(tab_ref, o_ref):
    h = pl.program_id(0)
    blk = o_ref.shape[-1]
    nb = blk // LANES
    r = lax.broadcasted_iota(jnp.int32, (LANES, LANES), 0)
    c = lax.broadcasted_iota(jnp.int32, (LANES, LANES), 1)
    max_exact = N_BUCKETS // 2
    far = tab_ref[h * N_BUCKETS + N_BUCKETS - 1]

    def band(offset):
        rel = r - c + offset
        n = jnp.maximum(rel, 0)
        nf = jnp.maximum(n, 1).astype(F32)
        large = max_exact + (jnp.log(nf / max_exact) / math.log(MAX_DISTANCE / max_exact)
                             * (N_BUCKETS - max_exact)).astype(jnp.int32)
        large = jnp.minimum(large, N_BUCKETS - 1)
        bucket = jnp.where(n < max_exact, n, large)
        val = jnp.zeros((LANES, LANES), F32)
        for b in range(N_BUCKETS):
            val = jnp.where(bucket == b, tab_ref[h * N_BUCKETS + b], val)
        return jnp.where(rel >= 0, (val - far) * LOG2E, NEG_INF)

    near = band(0)
    next_band = band(LANES)
    zeros = jnp.zeros((LANES, LANES), F32)
    masked = jnp.full((LANES, LANES), NEG_INF, F32)
    for a in range(nb):
        for b in range(nb):
            rows, cols = slice(a * LANES, (a + 1) * LANES), slice(b * LANES, (b + 1) * LANES)
            o_ref[0, 0, rows, cols] = near if b == a else next_band if b == a - 1 else zeros if b < a else masked
            o_ref[0, 1, rows, cols] = next_band if (a == 0 and b == nb - 1) else zeros


def _bias_tiles(rel_bias, n_heads, blk):
    assert MAX_DISTANCE <= LANES <= blk and blk % LANES == 0
    tab = jnp.transpose(rel_bias.astype(F32)).reshape(-1)
    return pl.pallas_call(
        _bias_kernel,
        grid=(n_heads,),
        in_specs=[pl.BlockSpec(memory_space=pltpu.SMEM)],
        out_specs=pl.BlockSpec((1, 2, blk, blk), lambda h: (h, 0, 0, 0)),
        out_shape=jax.ShapeDtypeStruct((n_heads, 2, blk, blk), F32),
        compiler_params=_params(("arbitrary",)),
        name="t5_bias_tiles",
    )(tab)


def _attn_kernel(q_ref, k_ref, v_ref, z_ref, bias_ref, lamv_ref, sg_ref, o_ref,
                 m_scr, l_scr, acc_scr, s_scr, mb_scr, *, lam_init):
    blk = s_scr.shape[-1]
    nlb = blk // LANES
    lv = lamv_ref[...]
    lam = (jnp.exp(jnp.sum(lv[0:1] * lv[1:2], axis=-1, keepdims=True))
           - jnp.exp(jnp.sum(lv[2:3] * lv[3:4], axis=-1, keepdims=True)) + lam_init)

    def q_block(qi, carry):
        _attn_q_block(qi, lam, q_ref, k_ref, v_ref, z_ref, bias_ref, sg_ref, o_ref,
                      m_scr, l_scr, acc_scr, s_scr, mb_scr, blk=blk, nlb=nlb, lam_init=lam_init)
        return carry

    lax.fori_loop(0, q_ref.shape[0] // blk, q_block, 0)


def _attn_q_block(qi, lam, q_ref, k_ref, v_ref, z_ref, bias_ref, sg_ref, o_ref,
                  m_scr, l_scr, acc_scr, s_scr, mb_scr, *, blk, nlb, lam_init):
    q_rows = pl.ds(pl.multiple_of(qi * blk, blk), blk)
    q = q_ref[q_rows, :]
    qs = (q[:, :HEAD_DIM], q[:, HEAD_DIM:])

    m_scr[...] = jnp.full(m_scr.shape, -jnp.inf, F32)
    l_scr[...] = jnp.zeros(l_scr.shape, F32)
    acc_scr[...] = jnp.zeros(acc_scr.shape, F32)

    def produce(j, slot, tile):
        off = pl.multiple_of(j * blk, blk)
        k = k_ref[pl.ds(off, blk), :]
        for mp in range(2):
            s = lax.dot_general(qs[mp], k[:, mp * HEAD_DIM:(mp + 1) * HEAD_DIM],
                                (((1,), (1,)), ((), ())), preferred_element_type=F32)
            if tile is not None:
                s = s + bias_ref[0, tile]
            s_scr[slot, mp] = s
            mb_scr[slot, mp] = jnp.broadcast_to(jnp.max(s, axis=-1, keepdims=True), (blk, LANES))

    def consume(j, slot, late_tile):
        off = pl.multiple_of(j * blk, blk)
        v = v_ref[pl.ds(off, blk), :]
        for mp in range(2):
            s = s_scr[slot, mp]
            if late_tile is None:
                m_blk = mb_scr[slot, mp]
            else:
                s = s + bias_ref[0, late_tile]
                m_blk = jnp.max(s, axis=-1, keepdims=True)
            m_prev = m_scr[mp]
            m_new = jnp.maximum(m_prev, m_blk)
            alpha = jnp.exp2(m_prev - m_new)
            p = jnp.exp2(s - jnp.concatenate([m_new] * nlb, axis=1))
            psum = p[:, :LANES]
            for t in range(1, nlb):
                psum = psum + p[:, t * LANES:(t + 1) * LANES]
            l_scr[mp] = alpha * l_scr[mp] + psum
            acc_scr[mp] = (jnp.concatenate([alpha] * (2 * HEAD_DIM // LANES), axis=1) * acc_scr[mp]
                           + jnp.dot(p.astype(BF16), v, preferred_element_type=F32))
            m_scr[mp] = m_new

    n_far = jnp.maximum(qi - 1, 0)
    n_pairs = n_far // 2
    produce(0, 0, None)

    def pair_body(jj, carry):
        j = 2 * jj
        produce(j + 1, 1, None)
        consume(j, 0, None)
        produce(j + 2, 0, None)
        consume(j + 1, 1, None)
        return carry

    lax.fori_loop(0, n_pairs, pair_body, 0)
    j0 = 2 * n_pairs

    @pl.when(qi == 0)
    def _():
        consume(0, 0, 0)

    @pl.when(jnp.logical_and(qi >= 1, n_far == j0))
    def _():
        produce(j0 + 1, 1, 0)
        consume(j0, 0, 1)
        consume(j0 + 1, 1, None)

    @pl.when(n_far > j0)
    def _():
        produce(j0 + 1, 1, 1)
        consume(j0, 0, None)
        produce(j0 + 2, 0, 0)
        consume(j0 + 1, 1, None)
        consume(j0 + 2, 0, None)

    l0 = jnp.sum(l_scr[0], axis=-1, keepdims=True)
    l1 = jnp.sum(l_scr[1], axis=-1, keepdims=True)
    o = acc_scr[0] / l0 - lam * (acc_scr[1] / l1)
    ms = jnp.mean(o * o, axis=-1, keepdims=True)
    o = o * lax.rsqrt(ms + SUBLN_EPS) * sg_ref[...] * (1.0 - lam_init)
    o_ref[q_rows, :] = (o * z_ref[q_rows, :].astype(F32)).astype(o_ref.dtype)


def _attention(proj, bias_tiles, lamv, subln_g, bsz, seq, d_attn, lam_init):
    hw = 2 * HEAD_DIM
    n_heads = d_attn // hw
    blk = _tile(seq, ATT_BLOCK)
    qoff, koff, voff, zoff = (2 * d_attn // hw, 3 * d_attn // hw, 4 * d_attn // hw, 5 * d_attn // hw)
    return pl.pallas_call(
        functools.partial(_attn_kernel, lam_init=lam_init),
        grid=(bsz, n_heads),
        in_specs=[
            pl.BlockSpec((seq, hw), lambda b, h: (b, qoff + h)),
            pl.BlockSpec((seq, hw), lambda b, h: (b, koff + h)),
            pl.BlockSpec((seq, hw), lambda b, h: (b, voff + h)),
            pl.BlockSpec((seq, hw), lambda b, h: (b, zoff + h)),
            pl.BlockSpec((1, 2, blk, blk), lambda b, h: (h, 0, 0, 0)),
            pl.BlockSpec((4, HEAD_DIM), lambda b, h: (0, 0)),
            pl.BlockSpec((1, hw), lambda b, h: (0, 0)),
        ],
        out_specs=pl.BlockSpec((seq, hw), lambda b, h: (b, h)),
        out_shape=jax.ShapeDtypeStruct((bsz * seq, d_attn), BF16),
        scratch_shapes=[
            pltpu.VMEM((2, blk, LANES), F32),
            pltpu.VMEM((2, blk, LANES), F32),
            pltpu.VMEM((2, blk, hw), F32),
            pltpu.VMEM((2, 2, blk, blk), F32),
            pltpu.VMEM((2, 2, blk, LANES), F32),
        ],
        compiler_params=_params(("arbitrary", "arbitrary")),
        name="diff_attention",
    )(proj, proj, proj, proj, bias_tiles, lamv, subln_g.reshape(1, hw))


def _merge_kernel(ys_ref, ya_ref, ws_hbm, wa_hbm, gs_ref, ga_ref, o_ref, stage, wsb_scr, wab_scr, sem,
                  *, layer, n_col_tiles):
    parts = (_weight_parts(ws_hbm, layer, wsb_scr, stage.shape[0])
             + _weight_parts(wa_hbm, layer, wab_scr, stage.shape[0]))
    _stage_weights(parts, stage, sem, pl.program_id(1) == 0, pl.program_id(0), n_col_tiles)
    tm = ys_ref.shape[0]
    cm = tm // MERGE_ROW_CHUNKS
    for c in range(MERGE_ROW_CHUNKS):
        rows = slice(c * cm, (c + 1) * cm)
        ps = jnp.dot(ys_ref[rows, :], wsb_scr[...], preferred_element_type=F32)
        pa = jnp.dot(ya_ref[rows, :], wab_scr[...], preferred_element_type=F32)
        o_ref[rows, :] = (gs_ref[rows, :].astype(F32) * ps + ga_ref[rows, :].astype(F32) * pa).astype(o_ref.dtype)


def _out_merge(y_s, y_a, ws_all, wa_all, layer, proj, d_ssm, d_attn):
    m, ks = y_s.shape
    ka = y_a.shape[1]
    assert ks == ka, "the two branch widths share one staging buffer"
    d = ws_all.shape[2]
    tm = _tile(m, 1024)
    tn = _tile(d, 1024)
    gs_off = (2 * d_ssm + 4 * d_attn) // tn
    ga_off = gs_off + d // tn
    return pl.pallas_call(
        functools.partial(_merge_kernel, layer=layer, n_col_tiles=d // tn),
        grid=(d // tn, m // tm),
        in_specs=[
            pl.BlockSpec((tm, ks), lambda j, i: (i, 0)),
            pl.BlockSpec((tm, ka), lambda j, i: (i, 0)),
            pl.BlockSpec(memory_space=pl.ANY),
            pl.BlockSpec(memory_space=pl.ANY),
            pl.BlockSpec((tm, tn), lambda j, i: (i, gs_off + j)),
            pl.BlockSpec((tm, tn), lambda j, i: (i, ga_off + j)),
        ],
        out_specs=pl.BlockSpec((tm, tn), lambda j, i: (i, j)),
        out_shape=jax.ShapeDtypeStruct((m, d), BF16),
        scratch_shapes=[pltpu.VMEM((ks, tn), F32), pltpu.VMEM((ks, tn), BF16), pltpu.VMEM((ka, tn), BF16),
                        pltpu.SemaphoreType.DMA(())],
        compiler_params=_params(("arbitrary", "arbitrary")),
        name="out_merge",
    )(y_s, y_a, ws_all, wa_all, proj, proj)


def _resid_kernel(a_ref, w_hbm, h_ref, m_ref, o_ref, stage, wb_scr, sem, *, layer, n_col_tiles):
    first = jnp.logical_and(pl.program_id(1) == 0, pl.program_id(2) == 0)
    _stage_weights(_weight_parts(w_hbm, layer, wb_scr, stage.shape[0]), stage, sem,
                   first, pl.program_id(0), n_col_tiles)
    acc = jnp.dot(a_ref[...], wb_scr[...], preferred_element_type=F32)
    o_ref[...] = h_ref[...] + m_ref[0, 2:3, :] * acc


def _resid(merged, w_all, layer, h_res2, mod3, bsz, in_place):
    m, k = merged.shape
    d = w_all.shape[2]
    seq = m // bsz
    tm = _tile(seq, 1024)
    tn = _tile(d, 1024)
    ns = seq // tm
    stage_rows = k // RESID_STAGE_PARTS
    return pl.pallas_call(
        functools.partial(_resid_kernel, layer=layer, n_col_tiles=d // tn),
        grid=(d // tn, bsz, ns),
        in_specs=[
            pl.BlockSpec((tm, k), lambda j, b, i: (b * ns + i, 0)),
            pl.BlockSpec(memory_space=pl.ANY),
            pl.BlockSpec((tm, tn), lambda j, b, i: (b * ns + i, j)),
            pl.BlockSpec((1, 3, tn), lambda j, b, i: (b, 0, j)),
        ],
        out_specs=pl.BlockSpec((tm, tn), lambda j, b, i: (b * ns + i, j)),
        out_shape=jax.ShapeDtypeStruct((m, d), F32),
        input_output_aliases={2: 0} if in_place else {},
        scratch_shapes=[pltpu.VMEM((stage_rows, tn), F32), pltpu.VMEM((k, tn), BF16), pltpu.SemaphoreType.DMA(())],
        compiler_params=_params(("arbitrary", "arbitrary", "arbitrary")),
        name="resid",
    )(merged, w_all, h_res2, mod3)


def _final_norm_kernel(x_ref, g_ref, o_ref):
    x = x_ref[...]
    ms = jnp.mean(x * x, axis=-1, keepdims=True)
    o_ref[...] = (x * lax.rsqrt(ms + EPS) * g_ref[...]).astype(o_ref.dtype)


def _final_norm(h2, g, out_dtype):
    m, d = h2.shape
    tm = _tile(m, 512)
    return pl.pallas_call(
        _final_norm_kernel,
        grid=(m // tm,),
        in_specs=[pl.BlockSpec((tm, d), lambda i: (i, 0)), pl.BlockSpec((1, d), lambda i: (0, 0))],
        out_specs=pl.BlockSpec((tm, d), lambda i: (i, 0)),
        out_shape=jax.ShapeDtypeStruct((m, d), out_dtype),
        compiler_params=_params(("arbitrary",)),
        name="final_norm",
    )(h2, g.reshape(1, d))


def kernel(x, c, norm_g, w_ada, b_ada, w_in, ssm_lambda_re, ssm_lambda_im, ssm_log_dt, ssm_b_re, ssm_b_im, ssm_c_re, ssm_c_im, ssm_d, w_glu, lambda_q1, lambda_k1, lambda_q2, lambda_k2, subln_g, w_out_ssm, w_out_attn, w_o, rel_bias, final_g):
    out_dtype = x.dtype
    bsz, seq, d = x.shape
    depth = w_in.shape[0]
    d_ssm = w_glu.shape[1]
    d_attn = w_out_attn.shape[1]
    n_heads = d_attn // (2 * HEAD_DIM)
    npb = seq // CHUNK
    nst = npb.bit_length() - 1
    assert seq % CHUNK == 0 and (1 << nst) == npb, "sequence must be CHUNK * 2^k"
    assert d_ssm % LANES == 0

    c_pad = jnp.zeros((8, d), F32).at[:bsz].set(c.astype(F32))
    mod = _ada_mod(c_pad, w_ada.astype(F32), b_ada.astype(F32))
    bias_tiles = _bias_tiles(rel_bias, n_heads, _tile(seq, ATT_BLOCK))

    h_res = x.astype(F32)
    for l in range(depth):
        mod3 = mod[l, :bsz].reshape(bsz, 3, d)
        hn = _prenorm(h_res, norm_g[l].astype(F32), mod3)
        proj = _in_proj(hn.reshape(bsz * seq, d), w_in.astype(F32), l, d_ssm)

        g_act = _s5(proj, ssm_lambda_re[l].astype(F32), ssm_lambda_im[l].astype(F32), ssm_log_dt[l].astype(F32),
                    ssm_b_re[l].astype(F32), ssm_b_im[l].astype(F32), ssm_c_re[l].astype(F32),
                    ssm_c_im[l].astype(F32), ssm_d[l].astype(F32), nst, npb, d_ssm)
        y_s = _glu(g_act, w_glu.astype(F32), l, proj, d_ssm)

        lam_init = 0.8 - 0.6 * math.exp(-0.3 * l)
        lamv = jnp.stack([lambda_q1[l], lambda_k1[l], lambda_q2[l], lambda_k2[l]]).astype(F32)
        y_a = _attention(proj, bias_tiles, lamv, subln_g[l].astype(F32), bsz, seq, d_attn, lam_init)

        merged = _out_merge(y_s, y_a, w_out_ssm.astype(F32), w_out_attn.astype(F32), l, proj, d_ssm, d_attn)
        h_res = _resid(merged, w_o.astype(F32), l, h_res.reshape(bsz * seq, d), mod3, bsz,
                       in_place=l > 0).reshape(bsz, seq, d)

    out = _final_norm(h_res.reshape(bsz * seq, d), final_g.astype(F32), out_dtype)
    return out.reshape(bsz, seq, d)
```

```python
import functools
import math

import jax
import jax.numpy as jnp
from jax import lax
from jax.experimental import pallas as pl
from jax.experimental.pallas import tpu as pltpu

F32 = jnp.float32
BF16 = jnp.bfloat16

SSM_GROUP = 16
STATE = 64
HEAD_DIM = 128
N_BUCKETS = 32
MAX_DISTANCE = 128
EPS = 1e-6
SUBLN_EPS = 1e-5
NEG_INF = -1e30

LANES = 128
CHUNK = 16
SW = 2 * STATE
GROUPS_PER_STEP = LANES // SSM_GROUP
XW = CHUNK * LANES
HW = GROUPS_PER_STEP * SW
ATT_BLOCK = 512
MERGE_ROW_CHUNKS = 2
RESID_STAGE_PARTS = 1
INPROJ_ROW_CHUNKS = 2
LOG2E = math.log2(math.e)
V7X_VMEM_LIMIT = 56 * 1024 * 1024


def _sigmoid(x):
    return 0.5 * jnp.tanh(0.5 * x) + 0.5


def _silu(x):
    return x * _sigmoid(x)


def _gelu_tanh(x):
    return 0.5 * x * (1.0 + jnp.tanh(math.sqrt(2.0 / math.pi) * (x + 0.044715 * (x * x * x))))


def _tile(dim, pref):
    t = min(dim, pref)
    assert dim % t == 0, (dim, pref)
    return t


def _params(sem):
    return pltpu.CompilerParams(dimension_semantics=sem, vmem_limit_bytes=V7X_VMEM_LIMIT)


def _ada_kernel(c_ref, w_ref, b_ref, o_ref):
    c = c_ref[...]
    ca = _silu(c).astype(BF16)
    o_ref[0] = jnp.dot(ca, w_ref[0].astype(BF16), preferred_element_type=F32) + b_ref[0]


def _ada_mod(c_pad, w_ada, b_ada):
    n_layers, d, n3 = w_ada.shape
    rows = c_pad.shape[0]
    tn = _tile(n3, 1024)
    return pl.pallas_call(
        _ada_kernel,
        grid=(n_layers, n3 // tn),
        in_specs=[
            pl.BlockSpec((rows, d), lambda l, j: (0, 0)),
            pl.BlockSpec((1, d, tn), lambda l, j: (l, 0, j)),
            pl.BlockSpec((1, 1, tn), lambda l, j: (l, 0, j)),
        ],
        out_specs=pl.BlockSpec((1, rows, tn), lambda l, j: (l, 0, j)),
        out_shape=jax.ShapeDtypeStruct((n_layers, rows, n3), F32),
        compiler_params=_params(("arbitrary", "arbitrary")),
        name="ada_mod",
    )(c_pad, w_ada, b_ada.reshape(n_layers, 1, n3))


def _prenorm_kernel(x_ref, g_ref, m_ref, o_ref):
    x = x_ref[0]
    ms = jnp.mean(x * x, axis=-1, keepdims=True)
    shift = m_ref[0, 0:1, :]
    scale = m_ref[0, 1:2, :]
    h = x * lax.rsqrt(ms + EPS) * g_ref[...]
    o_ref[0] = (h * (1.0 + scale) + shift).astype(o_ref.dtype)


def _prenorm(h_res, g, mod3):
    b, s, d = h_res.shape
    ts = _tile(s, 512)
    return pl.pallas_call(
        _prenorm_kernel,
        grid=(b, s // ts),
        in_specs=[
            pl.BlockSpec((1, ts, d), lambda i, j: (i, j, 0)),
            pl.BlockSpec((1, d), lambda i, j: (0, 0)),
            pl.BlockSpec((1, 3, d), lambda i, j: (i, 0, 0)),
        ],
        out_specs=pl.BlockSpec((1, ts, d), lambda i, j: (i, j, 0)),
        out_shape=jax.ShapeDtypeStruct((b, s, d), BF16),
        compiler_params=_params(("arbitrary", "arbitrary")),
        name="prenorm",
    )(h_res, g.reshape(1, d), mod3)


def _stage_weights(parts, stage, sem, first, col_tile, n_col_tiles):
    def copy(part, tile):
        return pltpu.make_async_copy(parts[part][0](tile), stage, sem)

    @pl.when(first)
    def _():
        @pl.when(col_tile == 0)
        def _():
            copy(0, 0).start()

        for part in range(len(parts)):
            copy(part, col_tile).wait()
            parts[part][1][...] = stage[...].astype(BF16)
            if part + 1 < len(parts):
                copy(part + 1, col_tile).start()
            else:
                @pl.when(col_tile + 1 < n_col_tiles)
                def _():
                    copy(0, col_tile + 1).start()


def _weight_parts(w_hbm, layer, wb_scr, stage_rows):
    k, tn = wb_scr.shape
    parts = []
    for r0 in range(0, k, stage_rows):
        rows = pl.ds(r0, stage_rows)
        parts.append((lambda tile, rows=rows: w_hbm.at[layer, rows, pl.ds(pl.multiple_of(tile * tn, tn), tn)],
                      wb_scr.at[rows, :]))
    return parts


def _inproj_kernel(a_ref, w_hbm, o_ref, stage, wb_scr, sem, *, seg_tiles, layer, n_col_tiles):
    j = pl.program_id(0)
    _stage_weights(_weight_parts(w_hbm, layer, wb_scr, stage.shape[0]), stage, sem,
                   pl.program_id(1) == 0, j, n_col_tiles)

    st = seg_tiles
    is_silu = jnp.logical_or(jnp.logical_and(j >= st, j < 2 * st), jnp.logical_and(j >= 5 * st, j < 6 * st))
    is_q = jnp.logical_and(j >= 2 * st, j < 3 * st)
    is_gate = jnp.logical_or(is_silu, j >= 6 * st)

    tm = a_ref.shape[0]
    cm = tm // INPROJ_ROW_CHUNKS

    def run(epilogue):
        for c in range(INPROJ_ROW_CHUNKS):
            acc = jnp.dot(a_ref[c * cm:(c + 1) * cm, :], wb_scr[...], preferred_element_type=F32)
            o_ref[c * cm:(c + 1) * cm, :] = epilogue(acc).astype(o_ref.dtype)

    @pl.when(jnp.logical_not(is_gate))
    def _():
        scale = jnp.where(is_q, HEAD_DIM ** -0.5 * LOG2E, 1.0).astype(F32)
        run(lambda acc: acc * scale)

    @pl.when(is_gate)
    def _():
        run(lambda acc: _sigmoid(acc) * jnp.where(is_silu, acc, 1.0))


def _in_proj(a, w_all, layer, d_ssm):
    m, k = a.shape
    n = w_all.shape[2]
    tm = _tile(m, 1024)
    tn = _tile(d_ssm, 1024)
    return pl.pallas_call(
        functools.partial(_inproj_kernel, seg_tiles=d_ssm // tn, layer=layer, n_col_tiles=n // tn),
        grid=(n // tn, m // tm),
        in_specs=[
            pl.BlockSpec((tm, k), lambda j, i: (i, 0)),
            pl.BlockSpec(memory_space=pl.ANY),
        ],
        out_specs=pl.BlockSpec((tm, tn), lambda j, i: (i, j)),
        out_shape=jax.ShapeDtypeStruct((m, n), BF16),
        scratch_shapes=[pltpu.VMEM((k, tn), F32), pltpu.VMEM((k, tn), BF16), pltpu.SemaphoreType.DMA(())],
        compiler_params=_params(("arbitrary", "arbitrary")),
        name="in_proj",
    )(a, w_all)


def _s5_kernel(x_ref, lr_ref, li_ref, ldt_ref, ba_ref, bb_ref, ca_ref, cb_ref, d_ref, o_ref,
               t_scr, r_scr, ot_scr, xs_scr, os_scr, *, nst, npb):
    nc = x_ref.shape[0] // CHUNK
    gps = GROUPS_PER_STEP

    @pl.when(pl.program_id(0) == 0)
    def _():
        t_scr[...] = jnp.zeros(t_scr.shape, t_scr.dtype)

    lr = lr_ref[...]
    li = li_ref[...]
    dt = jnp.exp(ldt_ref[...])
    mag = jnp.exp(lr * dt)
    are = mag * jnp.cos(li * dt)
    aim = mag * jnp.sin(li * dt)
    den = lr * lr + li * li
    nre = are - 1.0
    cre = (nre * lr + aim * li) / den
    cim = (aim * lr - nre * li) / den
    b_a = ba_ref[...]
    b_b = bb_ref[...]
    c_a = ca_ref[...]
    c_b = cb_ref[...]
    cre3 = cre[:, None, :]
    cim3 = cim[:, None, :]
    bbv = cre3 * b_a + cim3 * b_b
    bbs = cre3 * b_b - cim3 * b_a

    pre = jnp.ones_like(are)
    pim = jnp.zeros_like(are)
    pows = [(pre, pim)]
    for _ in range(CHUNK):
        pre, pim = are * pre - aim * pim, are * pim + aim * pre
        pows.append((pre, pim))

    same_group = (lax.broadcasted_iota(jnp.int32, (LANES, LANES), 0) // SSM_GROUP
                  == lax.broadcasted_iota(jnp.int32, (LANES, LANES), 1) // SSM_GROUP)
    own_state = (lax.broadcasted_iota(jnp.int32, (LANES, HW), 0) // SSM_GROUP
                 == lax.broadcasted_iota(jnp.int32, (LANES, HW), 1) // SW)
    bbv2 = bbv.reshape(LANES, SW)
    for k in range(CHUNK + 1):
        pr, pi = pows[k]
        z_k = (pr[:, None, :] * c_a + pi[:, None, :] * c_b).reshape(LANES, SW)
        if k < CHUNK:
            kk = lax.dot_general(bbv2, z_k, (((1,), (1,)), ((), ())),
                                 precision=lax.Precision.HIGHEST, preferred_element_type=F32)
            kk = jnp.where(same_group, kk, 0.0).astype(t_scr.dtype)
            for s in range(CHUNK - k):
                t = s + k
                t_scr[s * LANES:(s + 1) * LANES, t * LANES:(t + 1) * LANES] = kk
        if k >= 1:
            ot_scr[(k - 1) * LANES:k * LANES, :] = jnp.where(
                own_state, jnp.concatenate([z_k] * gps, axis=1), 0.0).astype(ot_scr.dtype)
    for s in range(CHUNK):
        pr, pi = pows[CHUNK - 1 - s]
        r_s = (pr[:, None, :] * bbv + pi[:, None, :] * bbs).reshape(LANES, SW)
        r_scr[s * LANES:(s + 1) * LANES, :] = jnp.where(
            own_state, jnp.concatenate([r_s] * gps, axis=1), 0.0).astype(r_scr.dtype)

    first_half = lax.broadcasted_iota(jnp.int32, are.shape, 1) < STATE
    sre, sim = pows[CHUNK]
    steps = []
    for _ in range(nst):
        steps.append((sre, jnp.where(first_half, -sim, sim)))
        sre, sim = sre * sre - sim * sim, 2.0 * sre * sim

    xs_scr[...] = x_ref[...].astype(F32)
    xcat = jnp.concatenate([xs_scr[pl.ds(s, nc, stride=CHUNK), :].astype(BF16) for s in range(CHUNK)], axis=1)

    row_in_batch = jnp.bitwise_and(lax.broadcasted_iota(jnp.int32, (nc, SW), 0), npb - 1)
    r = jnp.dot(xcat, r_scr[...], preferred_element_type=F32)
    h_ins = []
    for g in range(gps):
        h = r[:, g * SW:(g + 1) * SW]
        for k in range(nst):
            d = 1 << k
            hs = jnp.where(row_in_batch >= d, pltpu.roll(h, d, axis=0), 0.0)
            ar, ai = steps[k]
            h = h + ar[g:g + 1, :] * hs + ai[g:g + 1, :] * pltpu.roll(hs, STATE, axis=1)
        h_in = jnp.where(row_in_batch >= 1, pltpu.roll(h, 1, axis=0), 0.0)
        h_ins.append(h_in.astype(BF16))
    hcat = jnp.concatenate(h_ins, axis=1)

    y = jnp.dot(xcat, t_scr[...], preferred_element_type=F32)
    y = y + lax.dot_general(hcat, ot_scr[...], (((1,), (1,)), ((), ())), preferred_element_type=F32)
    y = y + d_ref[0] * xcat.astype(F32)
    gact = _gelu_tanh(y)
    for s in range(CHUNK):
        os_scr[pl.ds(s, nc, stride=CHUNK), :] = gact[:, s * LANES:(s + 1) * LANES]
    o_ref[...] = os_scr[...].astype(o_ref.dtype)


def _s5(proj, lam_re, lam_im, log_dt, b_re, b_im, c_re, c_im, d_skip, nst, npb, d_ssm):
    rows = proj.shape[0]
    g = d_ssm // SSM_GROUP
    gps = GROUPS_PER_STEP
    lr2 = jnp.concatenate([lam_re, lam_re], axis=-1)
    li2 = jnp.concatenate([lam_im, lam_im], axis=-1)
    ldt2 = jnp.broadcast_to(log_dt[:, None], (g, SW))
    brt = jnp.swapaxes(b_re, 1, 2)
    bit = jnp.swapaxes(b_im, 1, 2)
    b_a = jnp.concatenate([brt, bit], axis=-1)
    b_b = jnp.concatenate([-bit, brt], axis=-1)
    c_a = jnp.concatenate([c_re, -c_im], axis=-1)
    c_b = jnp.concatenate([-c_im, -c_re], axis=-1)
    d_lane = jnp.tile(d_skip.reshape(g // gps, 1, LANES), (1, 1, CHUNK))
    vec = pl.BlockSpec((gps, SW), lambda i: (i, 0))
    mat = pl.BlockSpec((gps, SSM_GROUP, SW), lambda i: (i, 0, 0))
    return pl.pallas_call(
        functools.partial(_s5_kernel, nst=nst, npb=npb),
        grid=(g // gps,),
        in_specs=[pl.BlockSpec((rows, LANES), lambda i: (0, i)), vec, vec, vec, mat, mat, mat, mat,
                  pl.BlockSpec((1, 1, XW), lambda i: (i, 0, 0))],
        out_specs=pl.BlockSpec((rows, LANES), lambda i: (0, i)),
        out_shape=jax.ShapeDtypeStruct((rows, d_ssm), BF16),
        scratch_shapes=[
            pltpu.VMEM((XW, XW), BF16),
            pltpu.VMEM((XW, HW), BF16),
            pltpu.VMEM((XW, HW), BF16),
            pltpu.VMEM((rows, LANES), F32),
            pltpu.VMEM((rows, LANES), F32),
        ],
        compiler_params=_params(("arbitrary",)),
        name="s5",
    )(proj, lr2, li2, ldt2, b_a, b_b, c_a, c_b, d_lane)


def _cast_weight_once(w_ref, wb_scr, first):
    @pl.when(first)
    def _():
        wb_scr[...] = w_ref[0].astype(BF16)


def _glu_kernel(a_ref, w_ref, g_ref, z_ref, o_ref, wb_scr):
    _cast_weight_once(w_ref, wb_scr, pl.program_id(1) == 0)
    acc = jnp.dot(a_ref[...], wb_scr[...], preferred_element_type=F32)
    g = g_ref[...].astype(F32)
    o_ref[...] = (g * _sigmoid(acc) * z_ref[...].astype(F32)).astype(o_ref.dtype)


def _glu(g_act, w_all, layer, proj, d_ssm):
    m, k = g_act.shape
    tm = _tile(m, 1024)
    tn = _tile(d_ssm, 1024)
    zoff = d_ssm // tn
    return pl.pallas_call(
        _glu_kernel,
        grid=(d_ssm // tn, m // tm),
        in_specs=[
            pl.BlockSpec((tm, k), lambda j, i: (i, 0)),
            pl.BlockSpec((1, k, tn), lambda j, i: (layer, 0, j)),
            pl.BlockSpec((tm, tn), lambda j, i: (i, j)),
            pl.BlockSpec((tm, tn), lambda j, i: (i, zoff + j)),
        ],
        out_specs=pl.BlockSpec((tm, tn), lambda j, i: (i, j)),
        out_shape=jax.ShapeDtypeStruct((m, d_ssm), BF16),
        scratch_shapes=[pltpu.VMEM((k, tn), BF16)],
        compiler_params=_params(("arbitrary", "arbitrary")),
        name="glu",
    )(g_act, w_all, g_act, proj)


def _bias_kernel(tab_ref, o_ref):
    h = pl.program_id(0)
    blk = o_ref.shape[-1]
    nb = blk // LANES
    r = lax.broadcasted_iota(jnp.int32, (LANES, LANES), 0)
    c = lax.broadcasted_iota(jnp.int32, (LANES, LANES), 1)
    max_exact = N_BUCKETS // 2
    far = tab_ref[h * N_BUCKETS + N_BUCKETS - 1]

    def band(offset):
        rel = r - c + offset
        n = jnp.maximum(rel, 0)
        nf = jnp.maximum(n, 1).astype(F32)
        large = max_exact + (jnp.log(nf / max_exact) / math.log(MAX_DISTANCE / max_exact)
                             * (N_BUCKETS - max_exact)).astype(jnp.int32)
        large = jnp.minimum(large, N_BUCKETS - 1)
        bucket = jnp.where(n < max_exact, n, large)
        val = jnp.zeros((LANES, LANES), F32)
        for b in range(N_BUCKETS):
            val = jnp.where(bucket == b, tab_ref[h * N_BUCKETS + b], val)
        return jnp.where(rel >= 0, (val - far) * LOG2E, NEG_INF)

    near = band(0)
    next_band = band(LANES)
    zeros = jnp.zeros((LANES, LANES), F32)
    masked = jnp.full((LANES, LANES), NEG_INF, F32)
    for a in range(nb):
        for b in range(nb):
            rows, cols = slice(a * LANES, (a + 1) * LANES), slice(b * LANES, (b + 1) * LANES)
            o_ref[0, 0, rows, cols] = near if b == a else next_band if b == a - 1 else zeros if b < a else masked
            o_ref[0, 1, rows, cols] = next_band if (a == 0 and b == nb - 1) else zeros


def _bias_tiles(rel_bias, n_heads, blk):
    assert MAX_DISTANCE <= LANES <= blk and blk % LANES == 0
    tab = jnp.transpose(rel_bias.astype(F32)).reshape(-1)
    return pl.pallas_call(
        _bias_kernel,
        grid=(n_heads,),
        in_specs=[pl.BlockSpec(memory_space=pltpu.SMEM)],
        out_specs=pl.BlockSpec((1, 2, blk, blk), lambda h: (h, 0, 0, 0)),
        out_shape=jax.ShapeDtypeStruct((n_heads, 2, blk, blk), F32),
        compiler_params=_params(("arbitrary",)),
        name="t5_bias_tiles",
    )(tab)


def _attn_kernel(q_ref, k_ref, v_ref, z_ref, bias_ref, lamv_ref, sg_ref, o_ref,
                 m_scr, l_scr, acc_scr, s_scr, mb_scr, *, lam_init):
    blk = s_scr.shape[-1]
    nlb = blk // LANES
    lv = lamv_ref[...]
    lam = (jnp.exp(jnp.sum(lv[0:1] * lv[1:2], axis=-1, keepdims=True))
           - jnp.exp(jnp.sum(lv[2:3] * lv[3:4], axis=-1, keepdims=True)) + lam_init)

    def q_block(qi, carry):
        _attn_q_block(qi, lam, q_ref, k_ref, v_ref, z_ref, bias_ref, sg_ref, o_ref,
                      m_scr, l_scr, acc_scr, s_scr, mb_scr, blk=blk, nlb=nlb, lam_init=lam_init)
        return carry

    lax.fori_loop(0, q_ref.shape[0] // blk, q_block, 0)


def _attn_q_block(qi, lam, q_ref, k_ref, v_ref, z_ref, bias_ref, sg_ref, o_ref,
                  m_scr, l_scr, acc_scr, s_scr, mb_scr, *, blk, nlb, lam_init):
    q_rows = pl.ds(pl.multiple_of(qi * blk, blk), blk)
    q = q_ref[q_rows, :]
    qs = (q[:, :HEAD_DIM], q[:, HEAD_DIM:])

    m_scr[...] = jnp.full(m_scr.shape, -jnp.inf, F32)
    l_scr[...] = jnp.zeros(l_scr.shape, F32)
    acc_scr[...] = jnp.zeros(acc_scr.shape, F32)

    def produce(j, slot, tile):
        off = pl.multiple_of(j * blk, blk)
        k = k_ref[pl.ds(off, blk), :]
        for mp in range(2):
            s = lax.dot_general(qs[mp], k[:, mp * HEAD_DIM:(mp + 1) * HEAD_DIM],
                                (((1,), (1,)), ((), ())), preferred_element_type=F32)
            if tile is not None:
                s = s + bias_ref[0, tile]
            s_scr[slot, mp] = s
            mb_scr[slot, mp] = jnp.broadcast_to(jnp.max(s, axis=-1, keepdims=True), (blk, LANES))

    def consume(j, slot, late_tile):
        off = pl.multiple_of(j * blk, blk)
        v = v_ref[pl.ds(off, blk), :]
        for mp in range(2):
            s = s_scr[slot, mp]
            if late_tile is None:
                m_blk = mb_scr[slot, mp]
            else:
                s = s + bias_ref[0, late_tile]
                m_blk = jnp.max(s, axis=-1, keepdims=True)
            m_prev = m_scr[mp]
            m_new = jnp.maximum(m_prev, m_blk)
            alpha = jnp.exp2(m_prev - m_new)
            p = jnp.exp2(s - jnp.concatenate([m_new] * nlb, axis=1))
            psum = p[:, :LANES]
            for t in range(1, nlb):
                psum = psum + p[:, t * LANES:(t + 1) * LANES]
            l_scr[mp] = alpha * l_scr[mp] + psum
            acc_scr[mp] = (jnp.concatenate([alpha] * (2 * HEAD_DIM // LANES), axis=1) * acc_scr[mp]
                           + jnp.dot(p.astype(BF16), v, preferred_element_type=F32))
            m_scr[mp] = m_new

    n_far = jnp.maximum(qi - 1, 0)
    n_pairs = n_far // 2
    produce(0, 0, None)

    def pair_body(jj, carry):
        j = 2 * jj
        produce(j + 1, 1, None)
        consume(j, 0, None)
        produce(j + 2, 0, None)
        consume(j + 1, 1, None)
        return carry

    lax.fori_loop(0, n_pairs, pair_body, 0)
    j0 = 2 * n_pairs

    @pl.when(qi == 0)
    def _():
        consume(0, 0, 0)

    @pl.when(jnp.logical_and(qi >= 1, n_far == j0))
    def _():
        produce(j0 + 1, 1, 0)
        consume(j0, 0, 1)
        consume(j0 + 1, 1, None)

    @pl.when(n_far > j0)
    def _():
        produce(j0 + 1, 1, 1)
        consume(j0, 0, None)
        produce(j0 + 2, 0, 0)
        consume(j0 + 1, 1, None)
        consume(j0 + 2, 0, None)

    l0 = jnp.sum(l_scr[0], axis=-1, keepdims=True)
    l1 = jnp.sum(l_scr[1], axis=-1, keepdims=True)
    o = acc_scr[0] / l0 - lam * (acc_scr[1] / l1)
    ms = jnp.mean(o * o, axis=-1, keepdims=True)
    o = o * lax.rsqrt(ms + SUBLN_EPS) * sg_ref[...] * (1.0 - lam_init)
    o_ref[q_rows, :] = (o * z_ref[q_rows, :].astype(F32)).astype(o_ref.dtype)


def _attention(proj, bias_tiles, lamv, subln_g, bsz, seq, d_attn, lam_init):
    hw = 2 * HEAD_DIM
    n_heads = d_attn // hw
    blk = _tile(seq, ATT_BLOCK)
    qoff, koff, voff, zoff = (2 * d_attn // hw, 3 * d_attn // hw, 4 * d_attn // hw, 5 * d_attn // hw)
    return pl.pallas_call(
        functools.partial(_attn_kernel, lam_init=lam_init),
        grid=(bsz, n_heads),
        in_specs=[
            pl.BlockSpec((seq, hw), lambda b, h: (b, qoff + h)),
            pl.BlockSpec((seq, hw), lambda b, h: (b, koff + h)),
            pl.BlockSpec((seq, hw), lambda b, h: (b, voff + h)),
            pl.BlockSpec((seq, hw), lambda b, h: (b, zoff + h)),
            pl.BlockSpec((1, 2, blk, blk), lambda b, h: (h, 0, 0, 0)),
            pl.BlockSpec((4, HEAD_DIM), lambda b, h: (0, 0)),
            pl.BlockSpec((1, hw), lambda b, h: (0, 0)),
        ],
        out_specs=pl.BlockSpec((seq, hw), lambda b, h: (b, h)),
        out_shape=jax.ShapeDtypeStruct((bsz * seq, d_attn), BF16),
        scratch_shapes=[
            pltpu.VMEM((2, blk, LANES), F32),
            pltpu.VMEM((2, blk, LANES), F32),
            pltpu.VMEM((2, blk, hw), F32),
            pltpu.VMEM((2, 2, blk, blk), F32),
            pltpu.VMEM((2, 2, blk, LANES), F32),
        ],
        compiler_params=_params(("arbitrary", "arbitrary")),
        name="diff_attention",
    )(proj, proj, proj, proj, bias_tiles, lamv, subln_g.reshape(1, hw))


def _merge_kernel(ys_ref, ya_ref, ws_hbm, wa_hbm, gs_ref, ga_ref, o_ref, stage_s, stage_a, wsb_scr, wab_scr,
                  sem_s, sem_a, *, layer, n_col_tiles):
    first = pl.program_id(1) == 0
    _stage_weights(_weight_parts(ws_hbm, layer, wsb_scr, stage_s.shape[0]), stage_s, sem_s,
                   first, pl.program_id(0), n_col_tiles)
    _stage_weights(_weight_parts(wa_hbm, layer, wab_scr, stage_a.shape[0]), stage_a, sem_a,
                   first, pl.program_id(0), n_col_tiles)
    tm = ys_ref.shape[0]
    cm = tm // MERGE_ROW_CHUNKS
    for c in range(MERGE_ROW_CHUNKS):
        rows = slice(c * cm, (c + 1) * cm)
        ps = jnp.dot(ys_ref[rows, :], wsb_scr[...], preferred_element_type=F32)
        pa = jnp.dot(ya_ref[rows, :], wab_scr[...], preferred_element_type=F32)
        o_ref[rows, :] = (gs_ref[rows, :].astype(F32) * ps + ga_ref[rows, :].astype(F32) * pa).astype(o_ref.dtype)


def _out_merge(y_s, y_a, ws_all, wa_all, layer, proj, d_ssm, d_attn):
    m, ks = y_s.shape
    ka = y_a.shape[1]
    d = ws_all.shape[2]
    tm = _tile(m, 1024)
    tn = _tile(d, 512)
    gs_off = (2 * d_ssm + 4 * d_attn) // tn
    ga_off = gs_off + d // tn
    return pl.pallas_call(
        functools.partial(_merge_kernel, layer=layer, n_col_tiles=d // tn),
        grid=(d // tn, m // tm),
        in_specs=[
            pl.BlockSpec((tm, ks), lambda j, i: (i, 0)),
            pl.BlockSpec((tm, ka), lambda j, i: (i, 0)),
            pl.BlockSpec(memory_space=pl.ANY),
            pl.BlockSpec(memory_space=pl.ANY),
            pl.BlockSpec((tm, tn), lambda j, i: (i, gs_off + j)),
            pl.BlockSpec((tm, tn), lambda j, i: (i, ga_off + j)),
        ],
        out_specs=pl.BlockSpec((tm, tn), lambda j, i: (i, j)),
        out_shape=jax.ShapeDtypeStruct((m, d), BF16),
        scratch_shapes=[pltpu.VMEM((ks, tn), F32), pltpu.VMEM((ka, tn), F32),
                        pltpu.VMEM((ks, tn), BF16), pltpu.VMEM((ka, tn), BF16),
                        pltpu.SemaphoreType.DMA(()), pltpu.SemaphoreType.DMA(())],
        compiler_params=_params(("arbitrary", "arbitrary")),
        name="out_merge",
    )(y_s, y_a, ws_all, wa_all, proj, proj)


def _resid_kernel(a_ref, w_hbm, h_ref, m_ref, o_ref, stage, wb_scr, sem, *, layer, n_col_tiles):
    first = jnp.logical_and(pl.program_id(1) == 0, pl.program_id(2) == 0)
    _stage_weights(_weight_parts(w_hbm, layer, wb_scr, stage.shape[0]), stage, sem,
                   first, pl.program_id(0), n_col_tiles)
    acc = jnp.dot(a_ref[...], wb_scr[...], preferred_element_type=F32)
    o_ref[...] = h_ref[...] + m_ref[0, 2:3, :] * acc


def _resid(merged, w_all, layer, h_res2, mod3, bsz, in_place):
    m, k = merged.shape
    d = w_all.shape[2]
    seq = m // bsz
    tm = _tile(seq, 1024)
    tn = _tile(d, 512)
    ns = seq // tm
    stage_rows = k // RESID_STAGE_PARTS
    return pl.pallas_call(
        functools.partial(_resid_kernel, layer=layer, n_col_tiles=d // tn),
        grid=(d // tn, bsz, ns),
        in_specs=[
            pl.BlockSpec((tm, k), lambda j, b, i: (b * ns + i, 0)),
            pl.BlockSpec(memory_space=pl.ANY),
            pl.BlockSpec((tm, tn), lambda j, b, i: (b * ns + i, j)),
            pl.BlockSpec((1, 3, tn), lambda j, b, i: (b, 0, j)),
        ],
        out_specs=pl.BlockSpec((tm, tn), lambda j, b, i: (b * ns + i, j)),
        out_shape=jax.ShapeDtypeStruct((m, d), F32),
        input_output_aliases={2: 0} if in_place else {},
        scratch_shapes=[pltpu.VMEM((stage_rows, tn), F32), pltpu.VMEM((k, tn), BF16), pltpu.SemaphoreType.DMA(())],
        compiler_params=_params(("arbitrary", "arbitrary", "arbitrary")),
        name="resid",
    )(merged, w_all, h_res2, mod3)


def _final_norm_kernel(x_ref, g_ref, o_ref):
    x = x_ref[...]
    ms = jnp.mean(x * x, axis=-1, keepdims=True)
    o_ref[...] = (x * lax.rsqrt(ms + EPS) * g_ref[...]).astype(o_ref.dtype)


def _final_norm(h2, g, out_dtype):
    m, d = h2.shape
    tm = _tile(m, 512)
    return pl.pallas_call(
        _final_norm_kernel,
        grid=(m // tm,),
        in_specs=[pl.BlockSpec((tm, d), lambda i: (i, 0)), pl.BlockSpec((1, d), lambda i: (0, 0))],
        out_specs=pl.BlockSpec((tm, d), lambda i: (i, 0)),
        out_shape=jax.ShapeDtypeStruct((m, d), out_dtype),
        compiler_params=_params(("arbitrary",)),
        name="final_norm",
    )(h2, g.reshape(1, d))


def kernel(x, c, norm_g, w_ada, b_ada, w_in, ssm_lambda_re, ssm_lambda_im, ssm_log_dt, ssm_b_re, ssm_b_im, ssm_c_re, ssm_c_im, ssm_d, w_glu, lambda_q1, lambda_k1, lambda_q2, lambda_k2, subln_g, w_out_ssm, w_out_attn, w_o, rel_bias, final_g):
    out_dtype = x.dtype
    bsz, seq, d = x.shape
    depth = w_in.shape[0]
    d_ssm = w_glu.shape[1]
    d_attn = w_out_attn.shape[1]
    n_heads = d_attn // (2 * HEAD_DIM)
    npb = seq // CHUNK
    nst = npb.bit_length() - 1
    assert seq % CHUNK == 0 and (1 << nst) == npb, "sequence must be CHUNK * 2^k"
    assert d_ssm % LANES == 0

    c_pad = jnp.zeros((8, d), F32).at[:bsz].set(c.astype(F32))
    mod = _ada_mod(c_pad, w_ada.astype(F32), b_ada.astype(F32))
    bias_tiles = _bias_tiles(rel_bias, n_heads, _tile(seq, ATT_BLOCK))

    h_res = x.astype(F32)
    for l in range(depth):
        mod3 = mod[l, :bsz].reshape(bsz, 3, d)
        hn = _prenorm(h_res, norm_g[l].astype(F32), mod3)
        proj = _in_proj(hn.reshape(bsz * seq, d), w_in.astype(F32), l, d_ssm)

        g_act = _s5(proj, ssm_lambda_re[l].astype(F32), ssm_lambda_im[l].astype(F32), ssm_log_dt[l].astype(F32),
                    ssm_b_re[l].astype(F32), ssm_b_im[l].astype(F32), ssm_c_re[l].astype(F32),
                    ssm_c_im[l].astype(F32), ssm_d[l].astype(F32), nst, npb, d_ssm)
        y_s = _glu(g_act, w_glu.astype(F32), l, proj, d_ssm)

        lam_init = 0.8 - 0.6 * math.exp(-0.3 * l)
        lamv = jnp.stack([lambda_q1[l], lambda_k1[l], lambda_q2[l], lambda_k2[l]]).astype(F32)
        y_a = _attention(proj, bias_tiles, lamv, subln_g[l].astype(F32), bsz, seq, d_attn, lam_init)

        merged = _out_merge(y_s, y_a, w_out_ssm.astype(F32), w_out_attn.astype(F32), l, proj, d_ssm, d_attn)
        h_res = _resid(merged, w_o.astype(F32), l, h_res.reshape(bsz * seq, d), mod3, bsz,
                       in_place=l > 0).reshape(bsz, seq, d)

    out = _final_norm(h_res.reshape(bsz * seq, d), final_g.astype(F32), out_dtype)
    return out.reshape(bsz, seq, d)
```

```python
import functools
import math

import jax
import jax.numpy as jnp
from jax import lax
from jax.experimental import pallas as pl
from jax.experimental.pallas import tpu as pltpu

F32 = jnp.float32
BF16 = jnp.bfloat16

SSM_GROUP = 16
STATE = 64
HEAD_DIM = 128
N_BUCKETS = 32
MAX_DISTANCE = 128
EPS = 1e-6
SUBLN_EPS = 1e-5
NEG_INF = -1e30

LANES = 128
SUBLANES = 8
T_COL_BLOCK = 256
CHUNK = 16
SW = 2 * STATE
GROUPS_PER_STEP = LANES // SSM_GROUP
XW = CHUNK * LANES
HW = GROUPS_PER_STEP * SW
ATT_BLOCK = 512
MERGE_ROW_CHUNKS = 2
RESID_STAGE_PARTS = 1
INPROJ_ROW_CHUNKS = 2
LOG2E = math.log2(math.e)
V7X_VMEM_LIMIT = 56 * 1024 * 1024


def _sigmoid(x):
    return 0.5 * jnp.tanh(0.5 * x) + 0.5


def _silu(x):
    return x * _sigmoid(x)


def _gelu_tanh(x):
    return 0.5 * x * (1.0 + jnp.tanh(math.sqrt(2.0 / math.pi) * (x + 0.044715 * (x * x * x))))


def _tile(dim, pref):
    t = min(dim, pref)
    assert dim % t == 0, (dim, pref)
    return t


def _params(sem):
    return pltpu.CompilerParams(dimension_semantics=sem, vmem_limit_bytes=V7X_VMEM_LIMIT)


def _ada_kernel(c_ref, w_ref, b_ref, o_ref):
    c = c_ref[...]
    ca = _silu(c).astype(BF16)
    o_ref[0] = jnp.dot(ca, w_ref[0].astype(BF16), preferred_element_type=F32) + b_ref[0]


def _ada_mod(c_pad, w_ada, b_ada):
    n_layers, d, n3 = w_ada.shape
    rows = c_pad.shape[0]
    tn = _tile(n3, 1024)
    return pl.pallas_call(
        _ada_kernel,
        grid=(n_layers, n3 // tn),
        in_specs=[
            pl.BlockSpec((rows, d), lambda l, j: (0, 0)),
            pl.BlockSpec((1, d, tn), lambda l, j: (l, 0, j)),
            pl.BlockSpec((1, 1, tn), lambda l, j: (l, 0, j)),
        ],
        out_specs=pl.BlockSpec((1, rows, tn), lambda l, j: (l, 0, j)),
        out_shape=jax.ShapeDtypeStruct((n_layers, rows, n3), F32),
        compiler_params=_params(("arbitrary", "arbitrary")),
        name="ada_mod",
    )(c_pad, w_ada, b_ada.reshape(n_layers, 1, n3))


def _prenorm_kernel(x_ref, g_ref, m_ref, o_ref):
    x = x_ref[0]
    ms = jnp.mean(x * x, axis=-1, keepdims=True)
    shift = m_ref[0, 0:1, :]
    scale = m_ref[0, 1:2, :]
    h = x * lax.rsqrt(ms + EPS) * g_ref[...]
    o_ref[0] = (h * (1.0 + scale) + shift).astype(o_ref.dtype)


def _prenorm(h_res, g, mod3):
    b, s, d = h_res.shape
    ts = _tile(s, 512)
    return pl.pallas_call(
        _prenorm_kernel,
        grid=(b, s // ts),
        in_specs=[
            pl.BlockSpec((1, ts, d), lambda i, j: (i, j, 0)),
            pl.BlockSpec((1, d), lambda i, j: (0, 0)),
            pl.BlockSpec((1, 3, d), lambda i, j: (i, 0, 0)),
        ],
        out_specs=pl.BlockSpec((1, ts, d), lambda i, j: (i, j, 0)),
        out_shape=jax.ShapeDtypeStruct((b, s, d), BF16),
        compiler_params=_params(("arbitrary", "arbitrary")),
        name="prenorm",
    )(h_res, g.reshape(1, d), mod3)


def _stage_weights(parts, stage, sem, first, col_tile, n_col_tiles):
    def copy(part, tile):
        return pltpu.make_async_copy(parts[part][0](tile), stage, sem)

    @pl.when(first)
    def _():
        @pl.when(col_tile == 0)
        def _():
            copy(0, 0).start()

        for part in range(len(parts)):
            copy(part, col_tile).wait()
            parts[part][1][...] = stage[...].astype(BF16)
            if part + 1 < len(parts):
                copy(part + 1, col_tile).start()
            else:
                @pl.when(col_tile + 1 < n_col_tiles)
                def _():
                    copy(0, col_tile + 1).start()


def _weight_parts(w_hbm, layer, wb_scr, stage_rows):
    k, tn = wb_scr.shape
    parts = []
    for r0 in range(0, k, stage_rows):
        rows = pl.ds(r0, stage_rows)
        parts.append((lambda tile, rows=rows: w_hbm.at[layer, rows, pl.ds(pl.multiple_of(tile * tn, tn), tn)],
                      wb_scr.at[rows, :]))
    return parts


def _inproj_kernel(a_ref, w_hbm, o_ref, stage, wb_scr, sem, *, seg_tiles, layer, n_col_tiles):
    j = pl.program_id(0)
    _stage_weights(_weight_parts(w_hbm, layer, wb_scr, stage.shape[0]), stage, sem,
                   pl.program_id(1) == 0, j, n_col_tiles)

    st = seg_tiles
    is_silu = jnp.logical_or(jnp.logical_and(j >= st, j < 2 * st), jnp.logical_and(j >= 5 * st, j < 6 * st))
    is_q = jnp.logical_and(j >= 2 * st, j < 3 * st)
    is_gate = jnp.logical_or(is_silu, j >= 6 * st)

    tm = a_ref.shape[0]
    cm = tm // INPROJ_ROW_CHUNKS

    def run(epilogue):
        for c in range(INPROJ_ROW_CHUNKS):
            acc = jnp.dot(a_ref[c * cm:(c + 1) * cm, :], wb_scr[...], preferred_element_type=F32)
            o_ref[c * cm:(c + 1) * cm, :] = epilogue(acc).astype(o_ref.dtype)

    @pl.when(jnp.logical_not(is_gate))
    def _():
        scale = jnp.where(is_q, HEAD_DIM ** -0.5 * LOG2E, 1.0).astype(F32)
        run(lambda acc: acc * scale)

    @pl.when(is_gate)
    def _():
        run(lambda acc: _sigmoid(acc) * jnp.where(is_silu, acc, 1.0))


def _in_proj(a, w_all, layer, d_ssm):
    m, k = a.shape
    n = w_all.shape[2]
    tm = _tile(m, 1024)
    tn = _tile(d_ssm, 1024)
    return pl.pallas_call(
        functools.partial(_inproj_kernel, seg_tiles=d_ssm // tn, layer=layer, n_col_tiles=n // tn),
        grid=(n // tn, m // tm),
        in_specs=[
            pl.BlockSpec((tm, k), lambda j, i: (i, 0)),
            pl.BlockSpec(memory_space=pl.ANY),
        ],
        out_specs=pl.BlockSpec((tm, tn), lambda j, i: (i, j)),
        out_shape=jax.ShapeDtypeStruct((m, n), BF16),
        scratch_shapes=[pltpu.VMEM((k, tn), F32), pltpu.VMEM((k, tn), BF16), pltpu.SemaphoreType.DMA(())],
        compiler_params=_params(("arbitrary", "arbitrary")),
        name="in_proj",
    )(a, w_all)


def _s5_kernel(x_ref, lr_ref, li_ref, ldt_ref, ba_ref, bb_ref, ca_ref, cb_ref, d_ref, o_ref,
               t_scr, r_scr, ot_scr, xs_scr, os_scr, ha_scr, hb_scr, *, nst, npb):
    nc = x_ref.shape[0] // CHUNK
    gps = GROUPS_PER_STEP

    @pl.when(pl.program_id(0) == 0)
    def _():
        t_scr[...] = jnp.zeros(t_scr.shape, t_scr.dtype)

    lr = lr_ref[...]
    li = li_ref[...]
    dt = jnp.exp(ldt_ref[...])
    mag = jnp.exp(lr * dt)
    are = mag * jnp.cos(li * dt)
    aim = mag * jnp.sin(li * dt)
    den = lr * lr + li * li
    nre = are - 1.0
    cre = (nre * lr + aim * li) / den
    cim = (aim * lr - nre * li) / den
    b_a = ba_ref[...]
    b_b = bb_ref[...]
    c_a = ca_ref[...]
    c_b = cb_ref[...]
    cre3 = cre[:, None, :]
    cim3 = cim[:, None, :]
    bbv = cre3 * b_a + cim3 * b_b
    bbs = cre3 * b_b - cim3 * b_a

    pre = jnp.ones_like(are)
    pim = jnp.zeros_like(are)
    pows = [(pre, pim)]
    for _ in range(CHUNK):
        pre, pim = are * pre - aim * pim, are * pim + aim * pre
        pows.append((pre, pim))

    same_group = (lax.broadcasted_iota(jnp.int32, (LANES, LANES), 0) // SSM_GROUP
                  == lax.broadcasted_iota(jnp.int32, (LANES, LANES), 1) // SSM_GROUP)
    own_state = (lax.broadcasted_iota(jnp.int32, (LANES, HW), 0) // SSM_GROUP
                 == lax.broadcasted_iota(jnp.int32, (LANES, HW), 1) // SW)
    bbv2 = bbv.reshape(LANES, SW)
    for k in range(CHUNK + 1):
        pr, pi = pows[k]
        z_k = (pr[:, None, :] * c_a + pi[:, None, :] * c_b).reshape(LANES, SW)
        if k < CHUNK:
            kk = lax.dot_general(bbv2, z_k, (((1,), (1,)), ((), ())),
                                 precision=lax.Precision.HIGHEST, preferred_element_type=F32)
            kk = jnp.where(same_group, kk, 0.0).astype(t_scr.dtype)
            for s in range(CHUNK - k):
                t = s + k
                t_scr[s * LANES:(s + 1) * LANES, t * LANES:(t + 1) * LANES] = kk
        if k >= 1:
            ot_scr[(k - 1) * LANES:k * LANES, :] = jnp.where(
                own_state, jnp.concatenate([z_k] * gps, axis=1), 0.0).astype(ot_scr.dtype)
    for s in range(CHUNK):
        pr, pi = pows[CHUNK - 1 - s]
        r_s = (pr[:, None, :] * bbv + pi[:, None, :] * bbs).reshape(LANES, SW)
        r_scr[s * LANES:(s + 1) * LANES, :] = jnp.where(
            own_state, jnp.concatenate([r_s] * gps, axis=1), 0.0).astype(r_scr.dtype)

    first_half = lax.broadcasted_iota(jnp.int32, are.shape, 1) < STATE
    sre, sim = pows[CHUNK]
    steps = []
    for _ in range(nst):
        steps.append((sre, jnp.where(first_half, -sim, sim)))
        sre, sim = sre * sre - sim * sim, 2.0 * sre * sim
    low_steps = SUBLANES.bit_length() - 1
    ure, uim = pows[CHUNK]
    qre, qim = ure, uim
    rowp = []
    for _ in range(SUBLANES):
        rowp.append((qre, jnp.where(first_half, -qim, qim)))
        qre, qim = ure * qre - uim * qim, ure * qim + uim * qre

    xs_scr[...] = x_ref[...].astype(F32)
    xcat = jnp.concatenate([xs_scr[pl.ds(s, nc, stride=CHUNK), :].astype(BF16) for s in range(CHUNK)], axis=1)

    row = lax.broadcasted_iota(jnp.int32, (nc, SW), 0)
    row_in_batch = jnp.bitwise_and(row, npb - 1)
    row_in_tile = jnp.bitwise_and(row, SUBLANES - 1)
    n_tiles = nc // SUBLANES
    tile_in_batch = jnp.bitwise_and(lax.broadcasted_iota(jnp.int32, (n_tiles, SW), 0), npb // SUBLANES - 1)

    def cmul_acc(acc, coef, x):
        return acc + coef[0] * x + coef[1] * pltpu.roll(x, STATE, axis=1)

    r = jnp.dot(xcat, r_scr[...], preferred_element_type=F32)
    h_ins = []
    for g in range(gps):
        coef = [(ar[g:g + 1, :], ai[g:g + 1, :]) for ar, ai in steps]
        h = r[:, g * SW:(g + 1) * SW]
        for k in range(low_steps):
            d = 1 << k
            h = cmul_acc(h, coef[k], jnp.where(row_in_tile >= d, pltpu.roll(h, d, axis=0), 0.0))
        ha_scr[...] = h
        t = ha_scr[pl.ds(SUBLANES - 1, n_tiles, stride=SUBLANES), :]
        for k in range(low_steps, nst):
            d = 1 << (k - low_steps)
            t = cmul_acc(t, coef[k], jnp.where(tile_in_batch >= d, pltpu.roll(t, d, axis=0), 0.0))
        t_in = jnp.where(tile_in_batch >= 1, pltpu.roll(t, 1, axis=0), 0.0)
        zero = jnp.zeros_like(t_in)
        for i in range(SUBLANES):
            hb_scr[pl.ds(i, n_tiles, stride=SUBLANES), :] = cmul_acc(zero, (rowp[i][0][g:g + 1, :],
                                                                          rowp[i][1][g:g + 1, :]), t_in)
        h = h + hb_scr[...]
        h_in = jnp.where(row_in_batch >= 1, pltpu.roll(h, 1, axis=0), 0.0)
        h_ins.append(h_in.astype(BF16))
    hcat = jnp.concatenate(h_ins, axis=1)

    y_cols = []
    for c0 in range(0, XW, T_COL_BLOCK):
        c1 = c0 + T_COL_BLOCK
        y_cols.append(jnp.dot(xcat[:, :c1], t_scr[:c1, c0:c1], preferred_element_type=F32))
    y = jnp.concatenate(y_cols, axis=1)
    y = y + lax.dot_general(hcat, ot_scr[...], (((1,), (1,)), ((), ())), preferred_element_type=F32)
    y = y + d_ref[0] * xcat.astype(F32)
    gact = _gelu_tanh(y)
    for s in range(CHUNK):
        os_scr[pl.ds(s, nc, stride=CHUNK), :] = gact[:, s * LANES:(s + 1) * LANES]
    o_ref[...] = os_scr[...].astype(o_ref.dtype)


def _s5(proj, lam_re, lam_im, log_dt, b_re, b_im, c_re, c_im, d_skip, nst, npb, d_ssm):
    rows = proj.shape[0]
    g = d_ssm // SSM_GROUP
    gps = GROUPS_PER_STEP
    lr2 = jnp.concatenate([lam_re, lam_re], axis=-1)
    li2 = jnp.concatenate([lam_im, lam_im], axis=-1)
    ldt2 = jnp.broadcast_to(log_dt[:, None], (g, SW))
    brt = jnp.swapaxes(b_re, 1, 2)
    bit = jnp.swapaxes(b_im, 1, 2)
    b_a = jnp.concatenate([brt, bit], axis=-1)
    b_b = jnp.concatenate([-bit, brt], axis=-1)
    c_a = jnp.concatenate([c_re, -c_im], axis=-1)
    c_b = jnp.concatenate([-c_im, -c_re], axis=-1)
    d_lane = jnp.tile(d_skip.reshape(g // gps, 1, LANES), (1, 1, CHUNK))
    vec = pl.BlockSpec((gps, SW), lambda i: (i, 0))
    mat = pl.BlockSpec((gps, SSM_GROUP, SW), lambda i: (i, 0, 0))
    return pl.pallas_call(
        functools.partial(_s5_kernel, nst=nst, npb=npb),
        grid=(g // gps,),
        in_specs=[pl.BlockSpec((rows, LANES), lambda i: (0, i)), vec, vec, vec, mat, mat, mat, mat,
                  pl.BlockSpec((1, 1, XW), lambda i: (i, 0, 0))],
        out_specs=pl.BlockSpec((rows, LANES), lambda i: (0, i)),
        out_shape=jax.ShapeDtypeStruct((rows, d_ssm), BF16),
        scratch_shapes=[
            pltpu.VMEM((XW, XW), BF16),
            pltpu.VMEM((XW, HW), BF16),
            pltpu.VMEM((XW, HW), BF16),
            pltpu.VMEM((rows, LANES), F32),
            pltpu.VMEM((rows, LANES), F32),
            pltpu.VMEM((rows // CHUNK, SW), F32),
            pltpu.VMEM((rows // CHUNK, SW), F32),
        ],
        compiler_params=_params(("arbitrary",)),
        name="s5",
    )(proj, lr2, li2, ldt2, b_a, b_b, c_a, c_b, d_lane)


def _cast_weight_once(w_ref, wb_scr, first):
    @pl.when(first)
    def _():
        wb_scr[...] = w_ref[0].astype(BF16)


def _glu_kernel(a_ref, w_ref, g_ref, z_ref, o_ref, wb_scr):
    _cast_weight_once(w_ref, wb_scr, pl.program_id(1) == 0)
    acc = jnp.dot(a_ref[...], wb_scr[...], preferred_element_type=F32)
    g = g_ref[...].astype(F32)
    o_ref[...] = (g * _sigmoid(acc) * z_ref[...].astype(F32)).astype(o_ref.dtype)


def _glu(g_act, w_all, layer, proj, d_ssm):
    m, k = g_act.shape
    tm = _tile(m, 1024)
    tn = _tile(d_ssm, 1024)
    zoff = d_ssm // tn
    return pl.pallas_call(
        _glu_kernel,
        grid=(d_ssm // tn, m // tm),
        in_specs=[
            pl.BlockSpec((tm, k), lambda j, i: (i, 0)),
            pl.BlockSpec((1, k, tn), lambda j, i: (layer, 0, j)),
            pl.BlockSpec((tm, tn), lambda j, i: (i, j)),
            pl.BlockSpec((tm, tn), lambda j, i: (i, zoff + j)),
        ],
        out_specs=pl.BlockSpec((tm, tn), lambda j, i: (i, j)),
        out_shape=jax.ShapeDtypeStruct((m, d_ssm), BF16),
        scratch_shapes=[pltpu.VMEM((k, tn), BF16)],
        compiler_params=_params(("arbitrary", "arbitrary")),
        name="glu",
    )(g_act, w_all, g_act, proj)


def _bias_kernel(tab_ref, o_ref):
    h = pl.program_id(0)
    blk = o_ref.shape[-1]
    nb = blk // LANES
    r = lax.broadcasted_iota(jnp.int32, (LANES, LANES), 0)
    c = lax.broadcasted_iota(jnp.int32, (LANES, LANES), 1)
    max_exact = N_BUCKETS // 2
    far = tab_ref[h * N_BUCKETS + N_BUCKETS - 1]

    def band(offset):
        rel = r - c + offset
        n = jnp.maximum(rel, 0)
        nf = jnp.maximum(n, 1).astype(F32)
        large = max_exact + (jnp.log(nf / max_exact) / math.log(MAX_DISTANCE / max_exact)
                             * (N_BUCKETS - max_exact)).astype(jnp.int32)
        large = jnp.minimum(large, N_BUCKETS - 1)
        bucket = jnp.where(n < max_exact, n, large)
        val = jnp.zeros((LANES, LANES), F32)
        for b in range(N_BUCKETS):
            val = jnp.where(bucket == b, tab_ref[h * N_BUCKETS + b], val)
        return jnp.where(rel >= 0, (val - far) * LOG2E, NEG_INF)

    near = band(0)
    next_band = band(LANES)
    zeros = jnp.zeros((LANES, LANES), F32)
    masked = jnp.full((LANES, LANES), NEG_INF, F32)
    for a in range(nb):
        for b in range(nb):
            rows, cols = slice(a * LANES, (a + 1) * LANES), slice(b * LANES, (b + 1) * LANES)
            o_ref[0, 0, rows, cols] = near if b == a else next_band if b == a - 1 else zeros if b < a else masked
            o_ref[0, 1, rows, cols] = next_band if (a == 0 and b == nb - 1) else zeros


def _bias_tiles(rel_bias, n_heads, blk):
    assert MAX_DISTANCE <= LANES <= blk and blk % LANES == 0
    tab = jnp.transpose(rel_bias.astype(F32)).reshape(-1)
    return pl.pallas_call(
        _bias_kernel,
        grid=(n_heads,),
        in_specs=[pl.BlockSpec(memory_space=pltpu.SMEM)],
        out_specs=pl.BlockSpec((1, 2, blk, blk), lambda h: (h, 0, 0, 0)),
        out_shape=jax.ShapeDtypeStruct((n_heads, 2, blk, blk), F32),
        compiler_params=_params(("arbitrary",)),
        name="t5_bias_tiles",
    )(tab)


def _attn_kernel(q_ref, k_ref, v_ref, z_ref, bias_ref, lamv_ref, sg_ref, o_ref,
                 m_scr, l_scr, acc_scr, s_scr, mb_scr, *, lam_init):
    blk = s_scr.shape[-1]
    nlb = blk // LANES
    lv = lamv_ref[...]
    lam = (jnp.exp(jnp.sum(lv[0:1] * lv[1:2], axis=-1, keepdims=True))
           - jnp.exp(jnp.sum(lv[2:3] * lv[3:4], axis=-1, keepdims=True)) + lam_init)

    def q_block(qi, carry):
        _attn_q_block(qi, lam, q_ref, k_ref, v_ref, z_ref, bias_ref, sg_ref, o_ref,
                      m_scr, l_scr, acc_scr, s_scr, mb_scr, blk=blk, nlb=nlb, lam_init=lam_init)
        return carry

    lax.fori_loop(0, q_ref.shape[0] // blk, q_block, 0)


def _attn_q_block(qi, lam, q_ref, k_ref, v_ref, z_ref, bias_ref, sg_ref, o_ref,
                  m_scr, l_scr, acc_scr, s_scr, mb_scr, *, blk, nlb, lam_init):
    q_rows = pl.ds(pl.multiple_of(qi * blk, blk), blk)
    q = q_ref[q_rows, :]
    qs = (q[:, :HEAD_DIM], q[:, HEAD_DIM:])

    m_scr[...] = jnp.full(m_scr.shape, -jnp.inf, F32)
    l_scr[...] = jnp.zeros(l_scr.shape, F32)
    acc_scr[...] = jnp.zeros(acc_scr.shape, F32)

    def produce(j, slot, tile):
        off = pl.multiple_of(j * blk, blk)
        k = k_ref[pl.ds(off, blk), :]
        for mp in range(2):
            s = lax.dot_general(qs[mp], k[:, mp * HEAD_DIM:(mp + 1) * HEAD_DIM],
                                (((1,), (1,)), ((), ())), preferred_element_type=F32)
            if tile is not None:
                s = s + bias_ref[0, tile]
            s_scr[slot, mp] = s
            mb_scr[slot, mp] = jnp.broadcast_to(jnp.max(s, axis=-1, keepdims=True), (blk, LANES))

    def consume(j, slot, late_tile):
        off = pl.multiple_of(j * blk, blk)
        v = v_ref[pl.ds(off, blk), :]
        for mp in range(2):
            s = s_scr[slot, mp]
            if late_tile is None:
                m_blk = mb_scr[slot, mp]
            else:
                s = s + bias_ref[0, late_tile]
                m_blk = jnp.max(s, axis=-1, keepdims=True)
            m_prev = m_scr[mp]
            m_new = jnp.maximum(m_prev, m_blk)
            alpha = jnp.exp2(m_prev - m_new)
            p = jnp.exp2(s - jnp.concatenate([m_new] * nlb, axis=1))
            psum = p[:, :LANES]
            for t in range(1, nlb):
                psum = psum + p[:, t * LANES:(t + 1) * LANES]
            l_scr[mp] = alpha * l_scr[mp] + psum
            acc_scr[mp] = (jnp.concatenate([alpha] * (2 * HEAD_DIM // LANES), axis=1) * acc_scr[mp]
                           + jnp.dot(p.astype(BF16), v, preferred_element_type=F32))
            m_scr[mp] = m_new

    n_far = jnp.maximum(qi - 1, 0)
    n_pairs = n_far // 2
    produce(0, 0, None)

    def pair_body(jj, carry):
        j = 2 * jj
        produce(j + 1, 1, None)
        consume(j, 0, None)
        produce(j + 2, 0, None)
        consume(j + 1, 1, None)
        return carry

    lax.fori_loop(0, n_pairs, pair_body, 0)
    j0 = 2 * n_pairs

    @pl.when(qi == 0)
    def _():
        consume(0, 0, 0)

    @pl.when(jnp.logical_and(qi >= 1, n_far == j0))
    def _():
        produce(j0 + 1, 1, 0)
        consume(j0, 0, 1)
        consume(j0 + 1, 1, None)

    @pl.when(n_far > j0)
    def _():
        produce(j0 + 1, 1, 1)
        consume(j0, 0, None)
        produce(j0 + 2, 0, 0)
        consume(j0 + 1, 1, None)
        consume(j0 + 2, 0, None)

    l0 = jnp.sum(l_scr[0], axis=-1, keepdims=True)
    l1 = jnp.sum(l_scr[1], axis=-1, keepdims=True)
    o = acc_scr[0] / l0 - lam * (acc_scr[1] / l1)
    ms = jnp.mean(o * o, axis=-1, keepdims=True)
    o = o * lax.rsqrt(ms + SUBLN_EPS) * sg_ref[...] * (1.0 - lam_init)
    o_ref[q_rows, :] = (o * z_ref[q_rows, :].astype(F32)).astype(o_ref.dtype)


def _attention(proj, bias_tiles, lamv, subln_g, bsz, seq, d_attn, lam_init):
    hw = 2 * HEAD_DIM
    n_heads = d_attn // hw
    blk = _tile(seq, ATT_BLOCK)
    qoff, koff, voff, zoff = (2 * d_attn // hw, 3 * d_attn // hw, 4 * d_attn // hw, 5 * d_attn // hw)
    return pl.pallas_call(
        functools.partial(_attn_kernel, lam_init=lam_init),
        grid=(bsz, n_heads),
        in_specs=[
            pl.BlockSpec((seq, hw), lambda b, h: (b, qoff + h)),
            pl.BlockSpec((seq, hw), lambda b, h: (b, koff + h)),
            pl.BlockSpec((seq, hw), lambda b, h: (b, voff + h)),
            pl.BlockSpec((seq, hw), lambda b, h: (b, zoff + h)),
            pl.BlockSpec((1, 2, blk, blk), lambda b, h: (h, 0, 0, 0)),
            pl.BlockSpec((4, HEAD_DIM), lambda b, h: (0, 0)),
            pl.BlockSpec((1, hw), lambda b, h: (0, 0)),
        ],
        out_specs=pl.BlockSpec((seq, hw), lambda b, h: (b, h)),
        out_shape=jax.ShapeDtypeStruct((bsz * seq, d_attn), BF16),
        scratch_shapes=[
            pltpu.VMEM((2, blk, LANES), F32),
            pltpu.VMEM((2, blk, LANES), F32),
            pltpu.VMEM((2, blk, hw), F32),
            pltpu.VMEM((2, 2, blk, blk), F32),
            pltpu.VMEM((2, 2, blk, LANES), F32),
        ],
        compiler_params=_params(("arbitrary", "arbitrary")),
        name="diff_attention",
    )(proj, proj, proj, proj, bias_tiles, lamv, subln_g.reshape(1, hw))


def _merge_kernel(ys_ref, ya_ref, ws_hbm, wa_hbm, gs_ref, ga_ref, o_ref, stage_s, stage_a, wsb_scr, wab_scr,
                  sem_s, sem_a, *, layer, n_col_tiles):
    first = pl.program_id(1) == 0
    _stage_weights(_weight_parts(ws_hbm, layer, wsb_scr, stage_s.shape[0]), stage_s, sem_s,
                   first, pl.program_id(0), n_col_tiles)
    _stage_weights(_weight_parts(wa_hbm, layer, wab_scr, stage_a.shape[0]), stage_a, sem_a,
                   first, pl.program_id(0), n_col_tiles)
    tm = ys_ref.shape[0]
    cm = tm // MERGE_ROW_CHUNKS
    for c in range(MERGE_ROW_CHUNKS):
        rows = slice(c * cm, (c + 1) * cm)
        ps = jnp.dot(ys_ref[rows, :], wsb_scr[...], preferred_element_type=F32)
        pa = jnp.dot(ya_ref[rows, :], wab_scr[...], preferred_element_type=F32)
        o_ref[rows, :] = (gs_ref[rows, :].astype(F32) * ps + ga_ref[rows, :].astype(F32) * pa).astype(o_ref.dtype)


def _out_merge(y_s, y_a, ws_all, wa_all, layer, proj, d_ssm, d_attn):
    m, ks = y_s.shape
    ka = y_a.shape[1]
    d = ws_all.shape[2]
    tm = _tile(m, 1024)
    tn = _tile(d, 512)
    gs_off = (2 * d_ssm + 4 * d_attn) // tn
    ga_off = gs_off + d // tn
    return pl.pallas_call(
        functools.partial(_merge_kernel, layer=layer, n_col_tiles=d // tn),
        grid=(d // tn, m // tm),
        in_specs=[
            pl.BlockSpec((tm, ks), lambda j, i: (i, 0)),
            pl.BlockSpec((tm, ka), lambda j, i: (i, 0)),
            pl.BlockSpec(memory_space=pl.ANY),
            pl.BlockSpec(memory_space=pl.ANY),
            pl.BlockSpec((tm, tn), lambda j, i: (i, gs_off + j)),
            pl.BlockSpec((tm, tn), lambda j, i: (i, ga_off + j)),
        ],
        out_specs=pl.BlockSpec((tm, tn), lambda j, i: (i, j)),
        out_shape=jax.ShapeDtypeStruct((m, d), BF16),
        scratch_shapes=[pltpu.VMEM((ks, tn), F32), pltpu.VMEM((ka, tn), F32),
                        pltpu.VMEM((ks, tn), BF16), pltpu.VMEM((ka, tn), BF16),
                        pltpu.SemaphoreType.DMA(()), pltpu.SemaphoreType.DMA(())],
        compiler_params=_params(("arbitrary", "arbitrary")),
        name="out_merge",
    )(y_s, y_a, ws_all, wa_all, proj, proj)


def _resid_kernel(a_ref, w_hbm, h_ref, m_ref, o_ref, stage, wb_scr, sem, *, layer, n_col_tiles):
    first = jnp.logical_and(pl.program_id(1) == 0, pl.program_id(2) == 0)
    _stage_weights(_weight_parts(w_hbm, layer, wb_scr, stage.shape[0]), stage, sem,
                   first, pl.program_id(0), n_col_tiles)
    acc = jnp.dot(a_ref[...], wb_scr[...], preferred_element_type=F32)
    o_ref[...] = h_ref[...] + m_ref[0, 2:3, :] * acc


def _resid(merged, w_all, layer, h_res2, mod3, bsz, in_place):
    m, k = merged.shape
    d = w_all.shape[2]
    seq = m // bsz
    tm = _tile(seq, 1024)
    tn = _tile(d, 512)
    ns = seq // tm
    stage_rows = k // RESID_STAGE_PARTS
    return pl.pallas_call(
        functools.partial(_resid_kernel, layer=layer, n_col_tiles=d // tn),
        grid=(d // tn, bsz, ns),
        in_specs=[
            pl.BlockSpec((tm, k), lambda j, b, i: (b * ns + i, 0)),
            pl.BlockSpec(memory_space=pl.ANY),
            pl.BlockSpec((tm, tn), lambda j, b, i: (b * ns + i, j)),
            pl.BlockSpec((1, 3, tn), lambda j, b, i: (b, 0, j)),
        ],
        out_specs=pl.BlockSpec((tm, tn), lambda j, b, i: (b * ns + i, j)),
        out_shape=jax.ShapeDtypeStruct((m, d), F32),
        input_output_aliases={2: 0} if in_place else {},
        scratch_shapes=[pltpu.VMEM((stage_rows, tn), F32), pltpu.VMEM((k, tn), BF16), pltpu.SemaphoreType.DMA(())],
        compiler_params=_params(("arbitrary", "arbitrary", "arbitrary")),
        name="resid",
    )(merged, w_all, h_res2, mod3)


def _final_norm_kernel(x_ref, g_ref, o_ref):
    x = x_ref[...]
    ms = jnp.mean(x * x, axis=-1, keepdims=True)
    o_ref[...] = (x * lax.rsqrt(ms + EPS) * g_ref[...]).astype(o_ref.dtype)


def _final_norm(h2, g, out_dtype):
    m, d = h2.shape
    tm = _tile(m, 512)
    return pl.pallas_call(
        _final_norm_kernel,
        grid=(m // tm,),
        in_specs=[pl.BlockSpec((tm, d), lambda i: (i, 0)), pl.BlockSpec((1, d), lambda i: (0, 0))],
        out_specs=pl.BlockSpec((tm, d), lambda i: (i, 0)),
        out_shape=jax.ShapeDtypeStruct((m, d), out_dtype),
        compiler_params=_params(("arbitrary",)),
        name="final_norm",
    )(h2, g.reshape(1, d))


def kernel(x, c, norm_g, w_ada, b_ada, w_in, ssm_lambda_re, ssm_lambda_im, ssm_log_dt, ssm_b_re, ssm_b_im, ssm_c_re, ssm_c_im, ssm_d, w_glu, lambda_q1, lambda_k1, lambda_q2, lambda_k2, subln_g, w_out_ssm, w_out_attn, w_o, rel_bias, final_g):
    out_dtype = x.dtype
    bsz, seq, d = x.shape
    depth = w_in.shape[0]
    d_ssm = w_glu.shape[1]
    d_attn = w_out_attn.shape[1]
    n_heads = d_attn // (2 * HEAD_DIM)
    npb = seq // CHUNK
    nst = npb.bit_length() - 1
    assert seq % CHUNK == 0 and (1 << nst) == npb, "sequence must be CHUNK * 2^k"
    assert d_ssm % LANES == 0 and npb % SUBLANES == 0

    c_pad = jnp.zeros((8, d), F32).at[:bsz].set(c.astype(F32))
    mod = _ada_mod(c_pad, w_ada.astype(F32), b_ada.astype(F32))
    bias_tiles = _bias_tiles(rel_bias, n_heads, _tile(seq, ATT_BLOCK))

    h_res = x.astype(F32)
    for l in range(depth):
        mod3 = mod[l, :bsz].reshape(bsz, 3, d)
        hn = _prenorm(h_res, norm_g[l].astype(F32), mod3)
        proj = _in_proj(hn.reshape(bsz * seq, d), w_in.astype(F32), l, d_ssm)

        g_act = _s5(proj, ssm_lambda_re[l].astype(F32), ssm_lambda_im[l].astype(F32), ssm_log_dt[l].astype(F32),
                    ssm_b_re[l].astype(F32), ssm_b_im[l].astype(F32), ssm_c_re[l].astype(F32),
                    ssm_c_im[l].astype(F32), ssm_d[l].astype(F32), nst, npb, d_ssm)
        y_s = _glu(g_act, w_glu.astype(F32), l, proj, d_ssm)

        lam_init = 0.8 - 0.6 * math.exp(-0.3 * l)
        lamv = jnp.stack([lambda_q1[l], lambda_k1[l], lambda_q2[l], lambda_k2[l]]).astype(F32)
        y_a = _attention(proj, bias_tiles, lamv, subln_g[l].astype(F32), bsz, seq, d_attn, lam_init)

        merged = _out_merge(y_s, y_a, w_out_ssm.astype(F32), w_out_attn.astype(F32), l, proj, d_ssm, d_attn)
        h_res = _resid(merged, w_o.astype(F32), l, h_res.reshape(bsz * seq, d), mod3, bsz,
                       in_place=l > 0).reshape(bsz, seq, d)

    out = _final_norm(h_res.reshape(bsz * seq, d), final_g.astype(F32), out_dtype)
    return out.reshape(bsz, seq, d)
```

```python
import functools
import math

import jax
import jax.numpy as jnp
from jax import lax
from jax.experimental import pallas as pl
from jax.experimental.pallas import tpu as pltpu

F32 = jnp.float32
BF16 = jnp.bfloat16

SSM_GROUP = 16
STATE = 64
HEAD_DIM = 128
N_BUCKETS = 32
MAX_DISTANCE = 128
EPS = 1e-6
SUBLN_EPS = 1e-5
NEG_INF = -1e30

LANES = 128
SUBLANES = 8
T_COL_BLOCK = 256
CHUNK = 16
SW = 2 * STATE
GROUPS_PER_STEP = LANES // SSM_GROUP
XW = CHUNK * LANES
HW = GROUPS_PER_STEP * SW
ATT_BLOCK = 512
MERGE_ROW_CHUNKS = 2
RESID_ROW_CHUNKS = 2
RESID_STAGE_PARTS = 1
INPROJ_ROW_CHUNKS = 2
LOG2E = math.log2(math.e)
V7X_VMEM_LIMIT = 56 * 1024 * 1024


def _sigmoid(x):
    return 0.5 * jnp.tanh(0.5 * x) + 0.5


def _silu(x):
    return x * _sigmoid(x)


def _gelu_tanh(x):
    return 0.5 * x * (1.0 + jnp.tanh(math.sqrt(2.0 / math.pi) * (x + 0.044715 * (x * x * x))))


def _tile(dim, pref):
    t = min(dim, pref)
    assert dim % t == 0, (dim, pref)
    return t


def _params(sem):
    return pltpu.CompilerParams(dimension_semantics=sem, vmem_limit_bytes=V7X_VMEM_LIMIT)


def _ada_kernel(c_ref, w_ref, b_ref, o_ref):
    c = c_ref[...]
    ca = _silu(c).astype(BF16)
    o_ref[0] = jnp.dot(ca, w_ref[0].astype(BF16), preferred_element_type=F32) + b_ref[0]


def _ada_mod(c_pad, w_ada, b_ada):
    n_layers, d, n3 = w_ada.shape
    rows = c_pad.shape[0]
    tn = _tile(n3, 1024)
    return pl.pallas_call(
        _ada_kernel,
        grid=(n_layers, n3 // tn),
        in_specs=[
            pl.BlockSpec((rows, d), lambda l, j: (0, 0)),
            pl.BlockSpec((1, d, tn), lambda l, j: (l, 0, j)),
            pl.BlockSpec((1, 1, tn), lambda l, j: (l, 0, j)),
        ],
        out_specs=pl.BlockSpec((1, rows, tn), lambda l, j: (l, 0, j)),
        out_shape=jax.ShapeDtypeStruct((n_layers, rows, n3), F32),
        compiler_params=_params(("arbitrary", "arbitrary")),
        name="ada_mod",
    )(c_pad, w_ada, b_ada.reshape(n_layers, 1, n3))


def _prenorm_kernel(x_ref, g_ref, m_ref, o_ref):
    x = x_ref[0]
    ms = jnp.mean(x * x, axis=-1, keepdims=True)
    shift = m_ref[0, 0:1, :]
    scale = m_ref[0, 1:2, :]
    h = x * lax.rsqrt(ms + EPS) * g_ref[...]
    o_ref[0] = (h * (1.0 + scale) + shift).astype(o_ref.dtype)


def _prenorm(h_res, g, mod3):
    b, s, d = h_res.shape
    ts = _tile(s, 512)
    return pl.pallas_call(
        _prenorm_kernel,
        grid=(b, s // ts),
        in_specs=[
            pl.BlockSpec((1, ts, d), lambda i, j: (i, j, 0)),
            pl.BlockSpec((1, d), lambda i, j: (0, 0)),
            pl.BlockSpec((1, 3, d), lambda i, j: (i, 0, 0)),
        ],
        out_specs=pl.BlockSpec((1, ts, d), lambda i, j: (i, j, 0)),
        out_shape=jax.ShapeDtypeStruct((b, s, d), BF16),
        compiler_params=_params(("arbitrary", "arbitrary")),
        name="prenorm",
    )(h_res, g.reshape(1, d), mod3)


def _stage_weights(parts, stage, sem, first, col_tile, n_col_tiles):
    def copy(part, tile):
        return pltpu.make_async_copy(parts[part][0](tile), stage, sem)

    @pl.when(first)
    def _():
        @pl.when(col_tile == 0)
        def _():
            copy(0, 0).start()

        for part in range(len(parts)):
            copy(part, col_tile).wait()
            parts[part][1][...] = stage[...].astype(BF16)
            if part + 1 < len(parts):
                copy(part + 1, col_tile).start()
            else:
                @pl.when(col_tile + 1 < n_col_tiles)
                def _():
                    copy(0, col_tile + 1).start()


def _weight_parts(w_hbm, layer, wb_scr, stage_rows):
    k, tn = wb_scr.shape
    parts = []
    for r0 in range(0, k, stage_rows):
        rows = pl.ds(r0, stage_rows)
        parts.append((lambda tile, rows=rows: w_hbm.at[layer, rows, pl.ds(pl.multiple_of(tile * tn, tn), tn)],
                      wb_scr.at[rows, :]))
    return parts


def _inproj_kernel(a_ref, w_hbm, o_ref, stage, wb_scr, sem, *, seg_tiles, layer, n_col_tiles):
    j = pl.program_id(0)
    _stage_weights(_weight_parts(w_hbm, layer, wb_scr, stage.shape[0]), stage, sem,
                   pl.program_id(1) == 0, j, n_col_tiles)

    st = seg_tiles
    is_silu = jnp.logical_or(jnp.logical_and(j >= st, j < 2 * st), jnp.logical_and(j >= 5 * st, j < 6 * st))
    is_q = jnp.logical_and(j >= 2 * st, j < 3 * st)
    is_gate = jnp.logical_or(is_silu, j >= 6 * st)

    tm = a_ref.shape[0]
    cm = tm // INPROJ_ROW_CHUNKS

    def run(epilogue):
        for c in range(INPROJ_ROW_CHUNKS):
            acc = jnp.dot(a_ref[c * cm:(c + 1) * cm, :], wb_scr[...], preferred_element_type=F32)
            o_ref[c * cm:(c + 1) * cm, :] = epilogue(acc).astype(o_ref.dtype)

    @pl.when(jnp.logical_not(is_gate))
    def _():
        scale = jnp.where(is_q, HEAD_DIM ** -0.5 * LOG2E, 1.0).astype(F32)
        run(lambda acc: acc * scale)

    @pl.when(is_gate)
    def _():
        run(lambda acc: _sigmoid(acc) * jnp.where(is_silu, acc, 1.0))


def _in_proj(a, w_all, layer, d_ssm):
    m, k = a.shape
    n = w_all.shape[2]
    tm = _tile(m, 1024)
    tn = _tile(d_ssm, 1024)
    return pl.pallas_call(
        functools.partial(_inproj_kernel, seg_tiles=d_ssm // tn, layer=layer, n_col_tiles=n // tn),
        grid=(n // tn, m // tm),
        in_specs=[
            pl.BlockSpec((tm, k), lambda j, i: (i, 0)),
            pl.BlockSpec(memory_space=pl.ANY),
        ],
        out_specs=pl.BlockSpec((tm, tn), lambda j, i: (i, j)),
        out_shape=jax.ShapeDtypeStruct((m, n), BF16),
        scratch_shapes=[pltpu.VMEM((k, tn), F32), pltpu.VMEM((k, tn), BF16), pltpu.SemaphoreType.DMA(())],
        compiler_params=_params(("arbitrary", "arbitrary")),
        name="in_proj",
    )(a, w_all)


def _s5_kernel(x_ref, lr_ref, li_ref, ldt_ref, ba_ref, bb_ref, ca_ref, cb_ref, d_ref, o_ref,
               t_scr, r_scr, ot_scr, xs_scr, os_scr, ha_scr, hb_scr, *, nst, npb):
    nc = x_ref.shape[0] // CHUNK
    gps = GROUPS_PER_STEP

    @pl.when(pl.program_id(0) == 0)
    def _():
        t_scr[...] = jnp.zeros(t_scr.shape, t_scr.dtype)

    lr = lr_ref[...]
    li = li_ref[...]
    dt = jnp.exp(ldt_ref[...])
    mag = jnp.exp(lr * dt)
    are = mag * jnp.cos(li * dt)
    aim = mag * jnp.sin(li * dt)
    den = lr * lr + li * li
    nre = are - 1.0
    cre = (nre * lr + aim * li) / den
    cim = (aim * lr - nre * li) / den
    b_a = ba_ref[...]
    b_b = bb_ref[...]
    c_a = ca_ref[...]
    c_b = cb_ref[...]
    cre3 = cre[:, None, :]
    cim3 = cim[:, None, :]
    bbv = cre3 * b_a + cim3 * b_b
    bbs = cre3 * b_b - cim3 * b_a

    pre = jnp.ones_like(are)
    pim = jnp.zeros_like(are)
    pows = [(pre, pim)]
    for _ in range(CHUNK):
        pre, pim = are * pre - aim * pim, are * pim + aim * pre
        pows.append((pre, pim))

    same_group = (lax.broadcasted_iota(jnp.int32, (LANES, LANES), 0) // SSM_GROUP
                  == lax.broadcasted_iota(jnp.int32, (LANES, LANES), 1) // SSM_GROUP)
    own_state = (lax.broadcasted_iota(jnp.int32, (LANES, HW), 0) // SSM_GROUP
                 == lax.broadcasted_iota(jnp.int32, (LANES, HW), 1) // SW)
    bbv2 = bbv.reshape(LANES, SW)
    for k in range(CHUNK + 1):
        pr, pi = pows[k]
        z_k = (pr[:, None, :] * c_a + pi[:, None, :] * c_b).reshape(LANES, SW)
        if k < CHUNK:
            kk = lax.dot_general(bbv2, z_k, (((1,), (1,)), ((), ())),
                                 precision=lax.Precision.HIGHEST, preferred_element_type=F32)
            kk = jnp.where(same_group, kk, 0.0).astype(t_scr.dtype)
            for s in range(CHUNK - k):
                t = s + k
                t_scr[s * LANES:(s + 1) * LANES, t * LANES:(t + 1) * LANES] = kk
        if k >= 1:
            ot_scr[(k - 1) * LANES:k * LANES, :] = jnp.where(
                own_state, jnp.concatenate([z_k] * gps, axis=1), 0.0).astype(ot_scr.dtype)
    for s in range(CHUNK):
        pr, pi = pows[CHUNK - 1 - s]
        r_s = (pr[:, None, :] * bbv + pi[:, None, :] * bbs).reshape(LANES, SW)
        r_scr[s * LANES:(s + 1) * LANES, :] = jnp.where(
            own_state, jnp.concatenate([r_s] * gps, axis=1), 0.0).astype(r_scr.dtype)

    first_half = lax.broadcasted_iota(jnp.int32, are.shape, 1) < STATE
    sre, sim = pows[CHUNK]
    steps = []
    for _ in range(nst):
        steps.append((sre, jnp.where(first_half, -sim, sim)))
        sre, sim = sre * sre - sim * sim, 2.0 * sre * sim
    low_steps = SUBLANES.bit_length() - 1
    ure, uim = pows[CHUNK]
    qre, qim = ure, uim
    rowp = []
    for _ in range(SUBLANES):
        rowp.append((qre, jnp.where(first_half, -qim, qim)))
        qre, qim = ure * qre - uim * qim, ure * qim + uim * qre

    xs_scr[...] = x_ref[...].astype(F32)
    xcat = jnp.concatenate([xs_scr[pl.ds(s, nc, stride=CHUNK), :].astype(BF16) for s in range(CHUNK)], axis=1)

    row = lax.broadcasted_iota(jnp.int32, (nc, SW), 0)
    row_in_batch = jnp.bitwise_and(row, npb - 1)
    row_in_tile = jnp.bitwise_and(row, SUBLANES - 1)
    n_tiles = nc // SUBLANES
    tile_in_batch = jnp.bitwise_and(lax.broadcasted_iota(jnp.int32, (n_tiles, SW), 0), npb // SUBLANES - 1)

    def cmul_acc(acc, coef, x):
        return acc + coef[0] * x + coef[1] * pltpu.roll(x, STATE, axis=1)

    r = jnp.dot(xcat, r_scr[...], preferred_element_type=F32)
    h_ins = []
    for g in range(gps):
        coef = [(ar[g:g + 1, :], ai[g:g + 1, :]) for ar, ai in steps]
        h = r[:, g * SW:(g + 1) * SW]
        for k in range(low_steps):
            d = 1 << k
            h = cmul_acc(h, coef[k], jnp.where(row_in_tile >= d, pltpu.roll(h, d, axis=0), 0.0))
        ha_scr[...] = h
        t = ha_scr[pl.ds(SUBLANES - 1, n_tiles, stride=SUBLANES), :]
        for k in range(low_steps, nst):
            d = 1 << (k - low_steps)
            t = cmul_acc(t, coef[k], jnp.where(tile_in_batch >= d, pltpu.roll(t, d, axis=0), 0.0))
        t_in = jnp.where(tile_in_batch >= 1, pltpu.roll(t, 1, axis=0), 0.0)
        zero = jnp.zeros_like(t_in)
        for i in range(SUBLANES):
            hb_scr[pl.ds(i, n_tiles, stride=SUBLANES), :] = cmul_acc(zero, (rowp[i][0][g:g + 1, :],
                                                                          rowp[i][1][g:g + 1, :]), t_in)
        h = h + hb_scr[...]
        h_in = jnp.where(row_in_batch >= 1, pltpu.roll(h, 1, axis=0), 0.0)
        h_ins.append(h_in.astype(BF16))
    hcat = jnp.concatenate(h_ins, axis=1)

    y_cols = []
    for c0 in range(0, XW, T_COL_BLOCK):
        c1 = c0 + T_COL_BLOCK
        y_cols.append(jnp.dot(xcat[:, :c1], t_scr[:c1, c0:c1], preferred_element_type=F32))
    y = jnp.concatenate(y_cols, axis=1)
    y = y + lax.dot_general(hcat, ot_scr[...], (((1,), (1,)), ((), ())), preferred_element_type=F32)
    y = y + d_ref[0] * xcat.astype(F32)
    gact = _gelu_tanh(y)
    for s in range(CHUNK):
        os_scr[pl.ds(s, nc, stride=CHUNK), :] = gact[:, s * LANES:(s + 1) * LANES]
    o_ref[...] = os_scr[...].astype(o_ref.dtype)


def _s5(proj, lam_re, lam_im, log_dt, b_re, b_im, c_re, c_im, d_skip, nst, npb, d_ssm):
    rows = proj.shape[0]
    g = d_ssm // SSM_GROUP
    gps = GROUPS_PER_STEP
    lr2 = jnp.concatenate([lam_re, lam_re], axis=-1)
    li2 = jnp.concatenate([lam_im, lam_im], axis=-1)
    ldt2 = jnp.broadcast_to(log_dt[:, None], (g, SW))
    brt = jnp.swapaxes(b_re, 1, 2)
    bit = jnp.swapaxes(b_im, 1, 2)
    b_a = jnp.concatenate([brt, bit], axis=-1)
    b_b = jnp.concatenate([-bit, brt], axis=-1)
    c_a = jnp.concatenate([c_re, -c_im], axis=-1)
    c_b = jnp.concatenate([-c_im, -c_re], axis=-1)
    d_lane = jnp.tile(d_skip.reshape(g // gps, 1, LANES), (1, 1, CHUNK))
    vec = pl.BlockSpec((gps, SW), lambda i: (i, 0))
    mat = pl.BlockSpec((gps, SSM_GROUP, SW), lambda i: (i, 0, 0))
    return pl.pallas_call(
        functools.partial(_s5_kernel, nst=nst, npb=npb),
        grid=(g // gps,),
        in_specs=[pl.BlockSpec((rows, LANES), lambda i: (0, i)), vec, vec, vec, mat, mat, mat, mat,
                  pl.BlockSpec((1, 1, XW), lambda i: (i, 0, 0))],
        out_specs=pl.BlockSpec((rows, LANES), lambda i: (0, i)),
        out_shape=jax.ShapeDtypeStruct((rows, d_ssm), BF16),
        scratch_shapes=[
            pltpu.VMEM((XW, XW), BF16),
            pltpu.VMEM((XW, HW), BF16),
            pltpu.VMEM((XW, HW), BF16),
            pltpu.VMEM((rows, LANES), F32),
            pltpu.VMEM((rows, LANES), F32),
            pltpu.VMEM((rows // CHUNK, SW), F32),
            pltpu.VMEM((rows // CHUNK, SW), F32),
        ],
        compiler_params=_params(("arbitrary",)),
        name="s5",
    )(proj, lr2, li2, ldt2, b_a, b_b, c_a, c_b, d_lane)


def _cast_weight_once(w_ref, wb_scr, first):
    @pl.when(first)
    def _():
        wb_scr[...] = w_ref[0].astype(BF16)


def _glu_kernel(a_ref, w_ref, z_ref, o_ref, wb_scr):
    _cast_weight_once(w_ref, wb_scr, pl.program_id(1) == 0)
    tn = o_ref.shape[1]
    acc = jnp.dot(a_ref[...], wb_scr[...], preferred_element_type=F32)
    g = a_ref[:, pl.ds(pl.multiple_of(pl.program_id(0) * tn, tn), tn)].astype(F32)
    o_ref[...] = (g * _sigmoid(acc) * z_ref[...].astype(F32)).astype(o_ref.dtype)


def _glu(g_act, w_all, layer, proj, d_ssm):
    m, k = g_act.shape
    tm = _tile(m, 1024)
    tn = _tile(d_ssm, 1024)
    zoff = d_ssm // tn
    return pl.pallas_call(
        _glu_kernel,
        grid=(d_ssm // tn, m // tm),
        in_specs=[
            pl.BlockSpec((tm, k), lambda j, i: (i, 0)),
            pl.BlockSpec((1, k, tn), lambda j, i: (layer, 0, j)),
            pl.BlockSpec((tm, tn), lambda j, i: (i, zoff + j)),
        ],
        out_specs=pl.BlockSpec((tm, tn), lambda j, i: (i, j)),
        out_shape=jax.ShapeDtypeStruct((m, d_ssm), BF16),
        scratch_shapes=[pltpu.VMEM((k, tn), BF16)],
        compiler_params=_params(("arbitrary", "arbitrary")),
        name="glu",
    )(g_act, w_all, proj)


def _bias_kernel(tab_ref, o_ref):
    h = pl.program_id(0)
    blk = o_ref.shape[-1]
    nb = blk // LANES
    r = lax.broadcasted_iota(jnp.int32, (LANES, LANES), 0)
    c = lax.broadcasted_iota(jnp.int32, (LANES, LANES), 1)
    max_exact = N_BUCKETS // 2
    far = tab_ref[h * N_BUCKETS + N_BUCKETS - 1]

    def band(offset):
        rel = r - c + offset
        n = jnp.maximum(rel, 0)
        nf = jnp.maximum(n, 1).astype(F32)
        large = max_exact + (jnp.log(nf / max_exact) / math.log(MAX_DISTANCE / max_exact)
                             * (N_BUCKETS - max_exact)).astype(jnp.int32)
        large = jnp.minimum(large, N_BUCKETS - 1)
        bucket = jnp.where(n < max_exact, n, large)
        val = jnp.zeros((LANES, LANES), F32)
        for b in range(N_BUCKETS):
            val = jnp.where(bucket == b, tab_ref[h * N_BUCKETS + b], val)
        return jnp.where(rel >= 0, (val - far) * LOG2E, NEG_INF)

    near = band(0)
    next_band = band(LANES)
    zeros = jnp.zeros((LANES, LANES), F32)
    masked = jnp.full((LANES, LANES), NEG_INF, F32)
    for a in range(nb):
        for b in range(nb):
            rows, cols = slice(a * LANES, (a + 1) * LANES), slice(b * LANES, (b + 1) * LANES)
            o_ref[0, 0, rows, cols] = near if b == a else next_band if b == a - 1 else zeros if b < a else masked
            o_ref[0, 1, rows, cols] = next_band if (a == 0 and b == nb - 1) else zeros


def _bias_tiles(rel_bias, n_heads, blk):
    assert MAX_DISTANCE <= LANES <= blk and blk % LANES == 0
    tab = jnp.transpose(rel_bias.astype(F32)).reshape(-1)
    return pl.pallas_call(
        _bias_kernel,
        grid=(n_heads,),
        in_specs=[pl.BlockSpec(memory_space=pltpu.SMEM)],
        out_specs=pl.BlockSpec((1, 2, blk, blk), lambda h: (h, 0, 0, 0)),
        out_shape=jax.ShapeDtypeStruct((n_heads, 2, blk, blk), F32),
        compiler_params=_params(("arbitrary",)),
        name="t5_bias_tiles",
    )(tab)


def _attn_kernel(q_ref, k_ref, v_ref, z_ref, bias_ref, lamv_ref, sg_ref, o_ref,
                 m_scr, l_scr, acc_scr, s_scr, mb_scr, *, lam_init):
    blk = s_scr.shape[-1]
    nlb = blk // LANES
    lv = lamv_ref[...]
    lam = (jnp.exp(jnp.sum(lv[0:1] * lv[1:2], axis=-1, keepdims=True))
           - jnp.exp(jnp.sum(lv[2:3] * lv[3:4], axis=-1, keepdims=True)) + lam_init)

    def q_block(qi, carry):
        _attn_q_block(qi, lam, q_ref, k_ref, v_ref, z_ref, bias_ref, sg_ref, o_ref,
                      m_scr, l_scr, acc_scr, s_scr, mb_scr, blk=blk, nlb=nlb, lam_init=lam_init)
        return carry

    lax.fori_loop(0, q_ref.shape[0] // blk, q_block, 0)


def _attn_q_block(qi, lam, q_ref, k_ref, v_ref, z_ref, bias_ref, sg_ref, o_ref,
                  m_scr, l_scr, acc_scr, s_scr, mb_scr, *, blk, nlb, lam_init):
    q_rows = pl.ds(pl.multiple_of(qi * blk, blk), blk)
    q = q_ref[q_rows, :]
    qs = (q[:, :HEAD_DIM], q[:, HEAD_DIM:])

    m_scr[...] = jnp.full(m_scr.shape, -jnp.inf, F32)
    l_scr[...] = jnp.zeros(l_scr.shape, F32)
    acc_scr[...] = jnp.zeros(acc_scr.shape, F32)

    def produce(j, slot, tile):
        off = pl.multiple_of(j * blk, blk)
        k = k_ref[pl.ds(off, blk), :]
        for mp in range(2):
            s = lax.dot_general(qs[mp], k[:, mp * HEAD_DIM:(mp + 1) * HEAD_DIM],
                                (((1,), (1,)), ((), ())), preferred_element_type=F32)
            if tile is not None:
                s = s + bias_ref[0, tile]
            s_scr[slot, mp] = s
            mb_scr[slot, mp] = jnp.broadcast_to(jnp.max(s, axis=-1, keepdims=True), (blk, LANES))

    def consume(j, slot, late_tile):
        off = pl.multiple_of(j * blk, blk)
        v = v_ref[pl.ds(off, blk), :]
        for mp in range(2):
            s = s_scr[slot, mp]
            if late_tile is None:
                m_blk = mb_scr[slot, mp]
            else:
                s = s + bias_ref[0, late_tile]
                m_blk = jnp.max(s, axis=-1, keepdims=True)
            m_prev = m_scr[mp]
            m_new = jnp.maximum(m_prev, m_blk)
            alpha = jnp.exp2(m_prev - m_new)
            p = jnp.exp2(s - jnp.concatenate([m_new] * nlb, axis=1))
            psum = p[:, :LANES]
            for t in range(1, nlb):
                psum = psum + p[:, t * LANES:(t + 1) * LANES]
            l_scr[mp] = alpha * l_scr[mp] + psum
            acc_scr[mp] = (jnp.concatenate([alpha] * (2 * HEAD_DIM // LANES), axis=1) * acc_scr[mp]
                           + jnp.dot(p.astype(BF16), v, preferred_element_type=F32))
            m_scr[mp] = m_new

    n_far = jnp.maximum(qi - 1, 0)
    n_pairs = n_far // 2
    produce(0, 0, None)

    def pair_body(jj, carry):
        j = 2 * jj
        produce(j + 1, 1, None)
        consume(j, 0, None)
        produce(j + 2, 0, None)
        consume(j + 1, 1, None)
        return carry

    lax.fori_loop(0, n_pairs, pair_body, 0)
    j0 = 2 * n_pairs

    @pl.when(qi == 0)
    def _():
        consume(0, 0, 0)

    @pl.when(jnp.logical_and(qi >= 1, n_far == j0))
    def _():
        produce(j0 + 1, 1, 0)
        consume(j0, 0, 1)
        consume(j0 + 1, 1, None)

    @pl.when(n_far > j0)
    def _():
        produce(j0 + 1, 1, 1)
        consume(j0, 0, None)
        produce(j0 + 2, 0, 0)
        consume(j0 + 1, 1, None)
        consume(j0 + 2, 0, None)

    l0 = jnp.sum(l_scr[0], axis=-1, keepdims=True)
    l1 = jnp.sum(l_scr[1], axis=-1, keepdims=True)
    o = acc_scr[0] / l0 - lam * (acc_scr[1] / l1)
    ms = jnp.mean(o * o, axis=-1, keepdims=True)
    o = o * lax.rsqrt(ms + SUBLN_EPS) * sg_ref[...] * (1.0 - lam_init)
    o_ref[q_rows, :] = (o * z_ref[q_rows, :].astype(F32)).astype(o_ref.dtype)


def _attention(proj, bias_tiles, lamv, subln_g, bsz, seq, d_attn, lam_init):
    hw = 2 * HEAD_DIM
    n_heads = d_attn // hw
    blk = _tile(seq, ATT_BLOCK)
    qoff, koff, voff, zoff = (2 * d_attn // hw, 3 * d_attn // hw, 4 * d_attn // hw, 5 * d_attn // hw)
    return pl.pallas_call(
        functools.partial(_attn_kernel, lam_init=lam_init),
        grid=(bsz, n_heads),
        in_specs=[
            pl.BlockSpec((seq, hw), lambda b, h: (b, qoff + h)),
            pl.BlockSpec((seq, hw), lambda b, h: (b, koff + h)),
            pl.BlockSpec((seq, hw), lambda b, h: (b, voff + h)),
            pl.BlockSpec((seq, hw), lambda b, h: (b, zoff + h)),
            pl.BlockSpec((1, 2, blk, blk), lambda b, h: (h, 0, 0, 0)),
            pl.BlockSpec((4, HEAD_DIM), lambda b, h: (0, 0)),
            pl.BlockSpec((1, hw), lambda b, h: (0, 0)),
        ],
        out_specs=pl.BlockSpec((seq, hw), lambda b, h: (b, h)),
        out_shape=jax.ShapeDtypeStruct((bsz * seq, d_attn), BF16),
        scratch_shapes=[
            pltpu.VMEM((2, blk, LANES), F32),
            pltpu.VMEM((2, blk, LANES), F32),
            pltpu.VMEM((2, blk, hw), F32),
            pltpu.VMEM((2, 2, blk, blk), F32),
            pltpu.VMEM((2, 2, blk, LANES), F32),
        ],
        compiler_params=_params(("arbitrary", "arbitrary")),
        name="diff_attention",
    )(proj, proj, proj, proj, bias_tiles, lamv, subln_g.reshape(1, hw))


def _merge_kernel(ys_ref, ya_ref, ws_hbm, wa_hbm, gs_ref, ga_ref, o_ref, stage_s, stage_a, wsb_scr, wab_scr,
                  sem_s, sem_a, *, layer, n_col_tiles):
    first = pl.program_id(1) == 0
    _stage_weights(_weight_parts(ws_hbm, layer, wsb_scr, stage_s.shape[0]), stage_s, sem_s,
                   first, pl.program_id(0), n_col_tiles)
    _stage_weights(_weight_parts(wa_hbm, layer, wab_scr, stage_a.shape[0]), stage_a, sem_a,
                   first, pl.program_id(0), n_col_tiles)
    tm = ys_ref.shape[0]
    cm = tm // MERGE_ROW_CHUNKS
    for c in range(MERGE_ROW_CHUNKS):
        rows = slice(c * cm, (c + 1) * cm)
        ps = jnp.dot(ys_ref[rows, :], wsb_scr[...], preferred_element_type=F32)
        pa = jnp.dot(ya_ref[rows, :], wab_scr[...], preferred_element_type=F32)
        o_ref[rows, :] = (gs_ref[rows, :].astype(F32) * ps + ga_ref[rows, :].astype(F32) * pa).astype(o_ref.dtype)


def _out_merge(y_s, y_a, ws_all, wa_all, layer, proj, d_ssm, d_attn):
    m, ks = y_s.shape
    ka = y_a.shape[1]
    d = ws_all.shape[2]
    tm = _tile(m, 1024)
    tn = _tile(d, 512)
    gs_off = (2 * d_ssm + 4 * d_attn) // tn
    ga_off = gs_off + d // tn
    return pl.pallas_call(
        functools.partial(_merge_kernel, layer=layer, n_col_tiles=d // tn),
        grid=(d // tn, m // tm),
        in_specs=[
            pl.BlockSpec((tm, ks), lambda j, i: (i, 0)),
            pl.BlockSpec((tm, ka), lambda j, i: (i, 0)),
            pl.BlockSpec(memory_space=pl.ANY),
            pl.BlockSpec(memory_space=pl.ANY),
            pl.BlockSpec((tm, tn), lambda j, i: (i, gs_off + j)),
            pl.BlockSpec((tm, tn), lambda j, i: (i, ga_off + j)),
        ],
        out_specs=pl.BlockSpec((tm, tn), lambda j, i: (i, j)),
        out_shape=jax.ShapeDtypeStruct((m, d), BF16),
        scratch_shapes=[pltpu.VMEM((ks, tn), F32), pltpu.VMEM((ka, tn), F32),
                        pltpu.VMEM((ks, tn), BF16), pltpu.VMEM((ka, tn), BF16),
                        pltpu.SemaphoreType.DMA(()), pltpu.SemaphoreType.DMA(())],
        compiler_params=_params(("arbitrary", "arbitrary")),
        name="out_merge",
    )(y_s, y_a, ws_all, wa_all, proj, proj)


def _resid_kernel(a_ref, w_hbm, h_ref, m_ref, o_ref, stage, wb_scr, sem, *, layer, n_col_tiles):
    first = jnp.logical_and(pl.program_id(1) == 0, pl.program_id(2) == 0)
    _stage_weights(_weight_parts(w_hbm, layer, wb_scr, stage.shape[0]), stage, sem,
                   first, pl.program_id(0), n_col_tiles)
    tm = a_ref.shape[0]
    cm = tm // RESID_ROW_CHUNKS
    gate = m_ref[0, 2:3, :]
    for c in range(RESID_ROW_CHUNKS):
        rows = slice(c * cm, (c + 1) * cm)
        acc = jnp.dot(a_ref[rows, :], wb_scr[...], preferred_element_type=F32)
        o_ref[rows, :] = h_ref[rows, :] + gate * acc


def _resid(merged, w_all, layer, h_res2, mod3, bsz, in_place):
    m, k = merged.shape
    d = w_all.shape[2]
    seq = m // bsz
    tm = _tile(seq, 1024)
    tn = _tile(d, 512)
    ns = seq // tm
    stage_rows = k // RESID_STAGE_PARTS
    return pl.pallas_call(
        functools.partial(_resid_kernel, layer=layer, n_col_tiles=d // tn),
        grid=(d // tn, bsz, ns),
        in_specs=[
            pl.BlockSpec((tm, k), lambda j, b, i: (b * ns + i, 0)),
            pl.BlockSpec(memory_space=pl.ANY),
            pl.BlockSpec((tm, tn), lambda j, b, i: (b * ns + i, j)),
            pl.BlockSpec((1, 3, tn), lambda j, b, i: (b, 0, j)),
        ],
        out_specs=pl.BlockSpec((tm, tn), lambda j, b, i: (b * ns + i, j)),
        out_shape=jax.ShapeDtypeStruct((m, d), F32),
        input_output_aliases={2: 0} if in_place else {},
        scratch_shapes=[pltpu.VMEM((stage_rows, tn), F32), pltpu.VMEM((k, tn), BF16), pltpu.SemaphoreType.DMA(())],
        compiler_params=_params(("arbitrary", "arbitrary", "arbitrary")),
        name="resid",
    )(merged, w_all, h_res2, mod3)


def _final_norm_kernel(x_ref, g_ref, o_ref):
    x = x_ref[...]
    ms = jnp.mean(x * x, axis=-1, keepdims=True)
    o_ref[...] = (x * lax.rsqrt(ms + EPS) * g_ref[...]).astype(o_ref.dtype)


def _final_norm(h2, g, out_dtype):
    m, d = h2.shape
    tm = _tile(m, 512)
    return pl.pallas_call(
        _final_norm_kernel,
        grid=(m // tm,),
        in_specs=[pl.BlockSpec((tm, d), lambda i: (i, 0)), pl.BlockSpec((1, d), lambda i: (0, 0))],
        out_specs=pl.BlockSpec((tm, d), lambda i: (i, 0)),
        out_shape=jax.ShapeDtypeStruct((m, d), out_dtype),
        compiler_params=_params(("arbitrary",)),
        name="final_norm",
    )(h2, g.reshape(1, d))


def kernel(x, c, norm_g, w_ada, b_ada, w_in, ssm_lambda_re, ssm_lambda_im, ssm_log_dt, ssm_b_re, ssm_b_im, ssm_c_re, ssm_c_im, ssm_d, w_glu, lambda_q1, lambda_k1, lambda_q2, lambda_k2, subln_g, w_out_ssm, w_out_attn, w_o, rel_bias, final_g):
    out_dtype = x.dtype
    bsz, seq, d = x.shape
    depth = w_in.shape[0]
    d_ssm = w_glu.shape[1]
    d_attn = w_out_attn.shape[1]
    n_heads = d_attn // (2 * HEAD_DIM)
    npb = seq // CHUNK
    nst = npb.bit_length() - 1
    assert seq % CHUNK == 0 and (1 << nst) == npb, "sequence must be CHUNK * 2^k"
    assert d_ssm % LANES == 0 and npb % SUBLANES == 0
    assert d_ssm == d_attn, "the in_proj segment map assumes equal branch widths"

    c_pad = jnp.zeros((8, d), F32).at[:bsz].set(c.astype(F32))
    mod = _ada_mod(c_pad, w_ada.astype(F32), b_ada.astype(F32))
    bias_tiles = _bias_tiles(rel_bias, n_heads, _tile(seq, ATT_BLOCK))

    h_res = x.astype(F32)
    for l in range(depth):
        mod3 = mod[l, :bsz].reshape(bsz, 3, d)
        hn = _prenorm(h_res, norm_g[l].astype(F32), mod3)
        proj = _in_proj(hn.reshape(bsz * seq, d), w_in.astype(F32), l, d_ssm)

        g_act = _s5(proj, ssm_lambda_re[l].astype(F32), ssm_lambda_im[l].astype(F32), ssm_log_dt[l].astype(F32),
                    ssm_b_re[l].astype(F32), ssm_b_im[l].astype(F32), ssm_c_re[l].astype(F32),
                    ssm_c_im[l].astype(F32), ssm_d[l].astype(F32), nst, npb, d_ssm)
        y_s = _glu(g_act, w_glu.astype(F32), l, proj, d_ssm)

        lam_init = 0.8 - 0.6 * math.exp(-0.3 * l)
        lamv = jnp.stack([lambda_q1[l], lambda_k1[l], lambda_q2[l], lambda_k2[l]]).astype(F32)
        y_a = _attention(proj, bias_tiles, lamv, subln_g[l].astype(F32), bsz, seq, d_attn, lam_init)

        merged = _out_merge(y_s, y_a, w_out_ssm.astype(F32), w_out_attn.astype(F32), l, proj, d_ssm, d_attn)
        h_res = _resid(merged, w_o.astype(F32), l, h_res.reshape(bsz * seq, d), mod3, bsz,
                       in_place=l > 0).reshape(bsz, seq, d)

    out = _final_norm(h_res.reshape(bsz * seq, d), final_g.astype(F32), out_dtype)
    return out.reshape(bsz, seq, d)
```

```python
import functools
import math

import jax
import jax.numpy as jnp
from jax import lax
from jax.experimental import pallas as pl
from jax.experimental.pallas import tpu as pltpu

F32 = jnp.float32
BF16 = jnp.bfloat16

SSM_GROUP = 16
STATE = 64
HEAD_DIM = 128
N_BUCKETS = 32
MAX_DISTANCE = 128
EPS = 1e-6
SUBLN_EPS = 1e-5
NEG_INF = -1e30

LANES = 128
SUBLANES = 8
T_COL_BLOCK = 256
CHUNK = 16
SW = 2 * STATE
GROUPS_PER_STEP = LANES // SSM_GROUP
XW = CHUNK * LANES
HW = GROUPS_PER_STEP * SW
ATT_BLOCK = 512
MERGE_ROW_CHUNKS = 2
ROW_STREAM_BUFFERS = 3
RESID_ROW_CHUNKS = 2
RESID_STAGE_PARTS = 1
INPROJ_ROW_CHUNKS = 2
LOG2E = math.log2(math.e)
V7X_VMEM_LIMIT = 56 * 1024 * 1024


def _sigmoid(x):
    return 0.5 * jnp.tanh(0.5 * x) + 0.5


def _silu(x):
    return x * _sigmoid(x)


def _gelu_tanh(x):
    return 0.5 * x * (1.0 + jnp.tanh(math.sqrt(2.0 / math.pi) * (x + 0.044715 * (x * x * x))))


def _tile(dim, pref):
    t = min(dim, pref)
    assert dim % t == 0, (dim, pref)
    return t


def _params(sem):
    return pltpu.CompilerParams(dimension_semantics=sem, vmem_limit_bytes=V7X_VMEM_LIMIT)


def _ada_kernel(c_ref, w_ref, b_ref, o_ref):
    c = c_ref[...]
    ca = _silu(c).astype(BF16)
    o_ref[0] = jnp.dot(ca, w_ref[0].astype(BF16), preferred_element_type=F32) + b_ref[0]


def _ada_mod(c_pad, w_ada, b_ada):
    n_layers, d, n3 = w_ada.shape
    rows = c_pad.shape[0]
    tn = _tile(n3, 1024)
    return pl.pallas_call(
        _ada_kernel,
        grid=(n_layers, n3 // tn),
        in_specs=[
            pl.BlockSpec((rows, d), lambda l, j: (0, 0)),
            pl.BlockSpec((1, d, tn), lambda l, j: (l, 0, j)),
            pl.BlockSpec((1, 1, tn), lambda l, j: (l, 0, j)),
        ],
        out_specs=pl.BlockSpec((1, rows, tn), lambda l, j: (l, 0, j)),
        out_shape=jax.ShapeDtypeStruct((n_layers, rows, n3), F32),
        compiler_params=_params(("arbitrary", "arbitrary")),
        name="ada_mod",
    )(c_pad, w_ada, b_ada.reshape(n_layers, 1, n3))


def _prenorm_kernel(x_ref, g_ref, m_ref, o_ref):
    x = x_ref[0]
    ms = jnp.mean(x * x, axis=-1, keepdims=True)
    shift = m_ref[0, 0:1, :]
    scale = m_ref[0, 1:2, :]
    h = x * lax.rsqrt(ms + EPS) * g_ref[...]
    o_ref[0] = (h * (1.0 + scale) + shift).astype(o_ref.dtype)


def _prenorm(h_res, g, mod3):
    b, s, d = h_res.shape
    ts = _tile(s, 512)
    return pl.pallas_call(
        _prenorm_kernel,
        grid=(b, s // ts),
        in_specs=[
            pl.BlockSpec((1, ts, d), lambda i, j: (i, j, 0)),
            pl.BlockSpec((1, d), lambda i, j: (0, 0)),
            pl.BlockSpec((1, 3, d), lambda i, j: (i, 0, 0)),
        ],
        out_specs=pl.BlockSpec((1, ts, d), lambda i, j: (i, j, 0)),
        out_shape=jax.ShapeDtypeStruct((b, s, d), BF16),
        compiler_params=_params(("arbitrary", "arbitrary")),
        name="prenorm",
    )(h_res, g.reshape(1, d), mod3)


def _stage_weights(parts, stage, sem, first, col_tile, n_col_tiles):
    def copy(part, tile):
        return pltpu.make_async_copy(parts[part][0](tile), stage, sem)

    @pl.when(first)
    def _():
        @pl.when(col_tile == 0)
        def _():
            copy(0, 0).start()

        for part in range(len(parts)):
            copy(part, col_tile).wait()
            parts[part][1][...] = stage[...].astype(BF16)
            if part + 1 < len(parts):
                copy(part + 1, col_tile).start()
            else:
                @pl.when(col_tile + 1 < n_col_tiles)
                def _():
                    copy(0, col_tile + 1).start()


def _weight_parts(w_hbm, layer, wb_scr, stage_rows):
    k, tn = wb_scr.shape
    parts = []
    for r0 in range(0, k, stage_rows):
        rows = pl.ds(r0, stage_rows)
        parts.append((lambda tile, rows=rows: w_hbm.at[layer, rows, pl.ds(pl.multiple_of(tile * tn, tn), tn)],
                      wb_scr.at[rows, :]))
    return parts


def _inproj_kernel(a_ref, w_hbm, o_ref, stage, wb_scr, sem, *, seg_tiles, layer, n_col_tiles):
    j = pl.program_id(0)
    _stage_weights(_weight_parts(w_hbm, layer, wb_scr, stage.shape[0]), stage, sem,
                   pl.program_id(1) == 0, j, n_col_tiles)

    st = seg_tiles
    is_silu = jnp.logical_or(jnp.logical_and(j >= st, j < 2 * st), jnp.logical_and(j >= 5 * st, j < 6 * st))
    is_q = jnp.logical_and(j >= 2 * st, j < 3 * st)
    is_gate = jnp.logical_or(is_silu, j >= 6 * st)

    tm = a_ref.shape[0]
    cm = tm // INPROJ_ROW_CHUNKS

    def run(epilogue):
        for c in range(INPROJ_ROW_CHUNKS):
            acc = jnp.dot(a_ref[c * cm:(c + 1) * cm, :], wb_scr[...], preferred_element_type=F32)
            o_ref[c * cm:(c + 1) * cm, :] = epilogue(acc).astype(o_ref.dtype)

    @pl.when(jnp.logical_not(is_gate))
    def _():
        scale = jnp.where(is_q, HEAD_DIM ** -0.5 * LOG2E, 1.0).astype(F32)
        run(lambda acc: acc * scale)

    @pl.when(is_gate)
    def _():
        run(lambda acc: _sigmoid(acc) * jnp.where(is_silu, acc, 1.0))


def _in_proj(a, w_all, layer, d_ssm):
    m, k = a.shape
    n = w_all.shape[2]
    tm = _tile(m, 1024)
    tn = _tile(d_ssm, 1024)
    return pl.pallas_call(
        functools.partial(_inproj_kernel, seg_tiles=d_ssm // tn, layer=layer, n_col_tiles=n // tn),
        grid=(n // tn, m // tm),
        in_specs=[
            pl.BlockSpec((tm, k), lambda j, i: (i, 0)),
            pl.BlockSpec(memory_space=pl.ANY),
        ],
        out_specs=pl.BlockSpec((tm, tn), lambda j, i: (i, j)),
        out_shape=jax.ShapeDtypeStruct((m, n), BF16),
        scratch_shapes=[pltpu.VMEM((k, tn), F32), pltpu.VMEM((k, tn), BF16), pltpu.SemaphoreType.DMA(())],
        compiler_params=_params(("arbitrary", "arbitrary")),
        name="in_proj",
    )(a, w_all)


def _s5_kernel(x_ref, lr_ref, li_ref, ldt_ref, ba_ref, bb_ref, ca_ref, cb_ref, d_ref, o_ref,
               t_scr, r_scr, ot_scr, xs_scr, os_scr, ha_scr, hb_scr, *, nst, npb):
    nc = x_ref.shape[0] // CHUNK
    gps = GROUPS_PER_STEP

    @pl.when(pl.program_id(0) == 0)
    def _():
        t_scr[...] = jnp.zeros(t_scr.shape, t_scr.dtype)

    lr = lr_ref[...]
    li = li_ref[...]
    dt = jnp.exp(ldt_ref[...])
    mag = jnp.exp(lr * dt)
    are = mag * jnp.cos(li * dt)
    aim = mag * jnp.sin(li * dt)
    den = lr * lr + li * li
    nre = are - 1.0
    cre = (nre * lr + aim * li) / den
    cim = (aim * lr - nre * li) / den
    b_a = ba_ref[...]
    b_b = bb_ref[...]
    c_a = ca_ref[...]
    c_b = cb_ref[...]
    cre3 = cre[:, None, :]
    cim3 = cim[:, None, :]
    bbv = cre3 * b_a + cim3 * b_b
    bbs = cre3 * b_b - cim3 * b_a

    pre = jnp.ones_like(are)
    pim = jnp.zeros_like(are)
    pows = [(pre, pim)]
    for _ in range(CHUNK):
        pre, pim = are * pre - aim * pim, are * pim + aim * pre
        pows.append((pre, pim))

    same_group = (lax.broadcasted_iota(jnp.int32, (LANES, LANES), 0) // SSM_GROUP
                  == lax.broadcasted_iota(jnp.int32, (LANES, LANES), 1) // SSM_GROUP)
    own_state = (lax.broadcasted_iota(jnp.int32, (LANES, HW), 0) // SSM_GROUP
                 == lax.broadcasted_iota(jnp.int32, (LANES, HW), 1) // SW)
    bbv2 = bbv.reshape(LANES, SW)
    for k in range(CHUNK + 1):
        pr, pi = pows[k]
        z_k = (pr[:, None, :] * c_a + pi[:, None, :] * c_b).reshape(LANES, SW)
        if k < CHUNK:
            kk = lax.dot_general(bbv2, z_k, (((1,), (1,)), ((), ())),
                                 precision=lax.Precision.HIGHEST, preferred_element_type=F32)
            kk = jnp.where(same_group, kk, 0.0).astype(t_scr.dtype)
            for s in range(CHUNK - k):
                t = s + k
                t_scr[s * LANES:(s + 1) * LANES, t * LANES:(t + 1) * LANES] = kk
        if k >= 1:
            ot_scr[(k - 1) * LANES:k * LANES, :] = jnp.where(
                own_state, jnp.concatenate([z_k] * gps, axis=1), 0.0).astype(ot_scr.dtype)
    for s in range(CHUNK):
        pr, pi = pows[CHUNK - 1 - s]
        r_s = (pr[:, None, :] * bbv + pi[:, None, :] * bbs).reshape(LANES, SW)
        r_scr[s * LANES:(s + 1) * LANES, :] = jnp.where(
            own_state, jnp.concatenate([r_s] * gps, axis=1), 0.0).astype(r_scr.dtype)

    first_half = lax.broadcasted_iota(jnp.int32, are.shape, 1) < STATE
    sre, sim = pows[CHUNK]
    steps = []
    for _ in range(nst):
        steps.append((sre, jnp.where(first_half, -sim, sim)))
        sre, sim = sre * sre - sim * sim, 2.0 * sre * sim
    low_steps = SUBLANES.bit_length() - 1
    ure, uim = pows[CHUNK]
    qre, qim = ure, uim
    rowp = []
    for _ in range(SUBLANES):
        rowp.append((qre, jnp.where(first_half, -qim, qim)))
        qre, qim = ure * qre - uim * qim, ure * qim + uim * qre

    xs_scr[...] = x_ref[...].astype(F32)
    xcat = jnp.concatenate([xs_scr[pl.ds(s, nc, stride=CHUNK), :].astype(BF16) for s in range(CHUNK)], axis=1)

    row = lax.broadcasted_iota(jnp.int32, (nc, SW), 0)
    row_in_batch = jnp.bitwise_and(row, npb - 1)
    row_in_tile = jnp.bitwise_and(row, SUBLANES - 1)
    n_tiles = nc // SUBLANES
    tile_in_batch = jnp.bitwise_and(lax.broadcasted_iota(jnp.int32, (n_tiles, SW), 0), npb // SUBLANES - 1)

    def cmul_acc(acc, coef, x):
        return acc + coef[0] * x + coef[1] * pltpu.roll(x, STATE, axis=1)

    r = jnp.dot(xcat, r_scr[...], preferred_element_type=F32)
    h_ins = []
    for g in range(gps):
        coef = [(ar[g:g + 1, :], ai[g:g + 1, :]) for ar, ai in steps]
        h = r[:, g * SW:(g + 1) * SW]
        for k in range(low_steps):
            d = 1 << k
            h = cmul_acc(h, coef[k], jnp.where(row_in_tile >= d, pltpu.roll(h, d, axis=0), 0.0))
        ha_scr[...] = h
        t = ha_scr[pl.ds(SUBLANES - 1, n_tiles, stride=SUBLANES), :]
        for k in range(low_steps, nst):
            d = 1 << (k - low_steps)
            t = cmul_acc(t, coef[k], jnp.where(tile_in_batch >= d, pltpu.roll(t, d, axis=0), 0.0))
        t_in = jnp.where(tile_in_batch >= 1, pltpu.roll(t, 1, axis=0), 0.0)
        zero = jnp.zeros_like(t_in)
        for i in range(SUBLANES):
            hb_scr[pl.ds(i, n_tiles, stride=SUBLANES), :] = cmul_acc(zero, (rowp[i][0][g:g + 1, :],
                                                                          rowp[i][1][g:g + 1, :]), t_in)
        h = h + hb_scr[...]
        h_in = jnp.where(row_in_batch >= 1, pltpu.roll(h, 1, axis=0), 0.0)
        h_ins.append(h_in.astype(BF16))
    hcat = jnp.concatenate(h_ins, axis=1)

    y_cols = []
    for c0 in range(0, XW, T_COL_BLOCK):
        c1 = c0 + T_COL_BLOCK
        y_cols.append(jnp.dot(xcat[:, :c1], t_scr[:c1, c0:c1], preferred_element_type=F32))
    y = jnp.concatenate(y_cols, axis=1)
    y = y + lax.dot_general(hcat, ot_scr[...], (((1,), (1,)), ((), ())), preferred_element_type=F32)
    y = y + d_ref[0] * xcat.astype(F32)
    gact = _gelu_tanh(y)
    for s in range(CHUNK):
        os_scr[pl.ds(s, nc, stride=CHUNK), :] = gact[:, s * LANES:(s + 1) * LANES]
    o_ref[...] = os_scr[...].astype(o_ref.dtype)


def _s5(proj, lam_re, lam_im, log_dt, b_re, b_im, c_re, c_im, d_skip, nst, npb, d_ssm):
    rows = proj.shape[0]
    g = d_ssm // SSM_GROUP
    gps = GROUPS_PER_STEP
    lr2 = jnp.concatenate([lam_re, lam_re], axis=-1)
    li2 = jnp.concatenate([lam_im, lam_im], axis=-1)
    ldt2 = jnp.broadcast_to(log_dt[:, None], (g, SW))
    brt = jnp.swapaxes(b_re, 1, 2)
    bit = jnp.swapaxes(b_im, 1, 2)
    b_a = jnp.concatenate([brt, bit], axis=-1)
    b_b = jnp.concatenate([-bit, brt], axis=-1)
    c_a = jnp.concatenate([c_re, -c_im], axis=-1)
    c_b = jnp.concatenate([-c_im, -c_re], axis=-1)
    d_lane = jnp.tile(d_skip.reshape(g // gps, 1, LANES), (1, 1, CHUNK))
    vec = pl.BlockSpec((gps, SW), lambda i: (i, 0))
    mat = pl.BlockSpec((gps, SSM_GROUP, SW), lambda i: (i, 0, 0))
    return pl.pallas_call(
        functools.partial(_s5_kernel, nst=nst, npb=npb),
        grid=(g // gps,),
        in_specs=[pl.BlockSpec((rows, LANES), lambda i: (0, i)), vec, vec, vec, mat, mat, mat, mat,
                  pl.BlockSpec((1, 1, XW), lambda i: (i, 0, 0))],
        out_specs=pl.BlockSpec((rows, LANES), lambda i: (0, i)),
        out_shape=jax.ShapeDtypeStruct((rows, d_ssm), BF16),
        scratch_shapes=[
            pltpu.VMEM((XW, XW), BF16),
            pltpu.VMEM((XW, HW), BF16),
            pltpu.VMEM((XW, HW), BF16),
            pltpu.VMEM((rows, LANES), F32),
            pltpu.VMEM((rows, LANES), F32),
            pltpu.VMEM((rows // CHUNK, SW), F32),
            pltpu.VMEM((rows // CHUNK, SW), F32),
        ],
        compiler_params=_params(("arbitrary",)),
        name="s5",
    )(proj, lr2, li2, ldt2, b_a, b_b, c_a, c_b, d_lane)


def _cast_weight_once(w_ref, wb_scr, first):
    @pl.when(first)
    def _():
        wb_scr[...] = w_ref[0].astype(BF16)


def _glu_kernel(a_ref, w_ref, z_ref, o_ref, wb_scr):
    _cast_weight_once(w_ref, wb_scr, pl.program_id(1) == 0)
    tn = o_ref.shape[1]
    acc = jnp.dot(a_ref[...], wb_scr[...], preferred_element_type=F32)
    g = a_ref[:, pl.ds(pl.multiple_of(pl.program_id(0) * tn, tn), tn)].astype(F32)
    o_ref[...] = (g * _sigmoid(acc) * z_ref[...].astype(F32)).astype(o_ref.dtype)


def _glu(g_act, w_all, layer, proj, d_ssm):
    m, k = g_act.shape
    tm = _tile(m, 1024)
    tn = _tile(d_ssm, 1024)
    zoff = d_ssm // tn
    return pl.pallas_call(
        _glu_kernel,
        grid=(d_ssm // tn, m // tm),
        in_specs=[
            pl.BlockSpec((tm, k), lambda j, i: (i, 0)),
            pl.BlockSpec((1, k, tn), lambda j, i: (layer, 0, j)),
            pl.BlockSpec((tm, tn), lambda j, i: (i, zoff + j)),
        ],
        out_specs=pl.BlockSpec((tm, tn), lambda j, i: (i, j)),
        out_shape=jax.ShapeDtypeStruct((m, d_ssm), BF16),
        scratch_shapes=[pltpu.VMEM((k, tn), BF16)],
        compiler_params=_params(("arbitrary", "arbitrary")),
        name="glu",
    )(g_act, w_all, proj)


def _bias_kernel(tab_ref, o_ref):
    h = pl.program_id(0)
    blk = o_ref.shape[-1]
    nb = blk // LANES
    r = lax.broadcasted_iota(jnp.int32, (LANES, LANES), 0)
    c = lax.broadcasted_iota(jnp.int32, (LANES, LANES), 1)
    max_exact = N_BUCKETS // 2
    far = tab_ref[h * N_BUCKETS + N_BUCKETS - 1]

    def band(offset):
        rel = r - c + offset
        n = jnp.maximum(rel, 0)
        nf = jnp.maximum(n, 1).astype(F32)
        large = max_exact + (jnp.log(nf / max_exact) / math.log(MAX_DISTANCE / max_exact)
                             * (N_BUCKETS - max_exact)).astype(jnp.int32)
        large = jnp.minimum(large, N_BUCKETS - 1)
        bucket = jnp.where(n < max_exact, n, large)
        val = jnp.zeros((LANES, LANES), F32)
        for b in range(N_BUCKETS):
            val = jnp.where(bucket == b, tab_ref[h * N_BUCKETS + b], val)
        return jnp.where(rel >= 0, (val - far) * LOG2E, NEG_INF)

    near = band(0)
    next_band = band(LANES)
    zeros = jnp.zeros((LANES, LANES), F32)
    masked = jnp.full((LANES, LANES), NEG_INF, F32)
    for a in range(nb):
        for b in range(nb):
            rows, cols = slice(a * LANES, (a + 1) * LANES), slice(b * LANES, (b + 1) * LANES)
            o_ref[0, 0, rows, cols] = near if b == a else next_band if b == a - 1 else zeros if b < a else masked
            o_ref[0, 1, rows, cols] = next_band if (a == 0 and b == nb - 1) else zeros


def _bias_tiles(rel_bias, n_heads, blk):
    assert MAX_DISTANCE <= LANES <= blk and blk % LANES == 0
    tab = jnp.transpose(rel_bias.astype(F32)).reshape(-1)
    return pl.pallas_call(
        _bias_kernel,
        grid=(n_heads,),
        in_specs=[pl.BlockSpec(memory_space=pltpu.SMEM)],
        out_specs=pl.BlockSpec((1, 2, blk, blk), lambda h: (h, 0, 0, 0)),
        out_shape=jax.ShapeDtypeStruct((n_heads, 2, blk, blk), F32),
        compiler_params=_params(("arbitrary",)),
        name="t5_bias_tiles",
    )(tab)


def _attn_kernel(q_ref, k_ref, v_ref, z_ref, bias_ref, lamv_ref, sg_ref, o_ref,
                 m_scr, l_scr, acc_scr, s_scr, mb_scr, *, lam_init):
    blk = s_scr.shape[-1]
    nlb = blk // LANES
    lv = lamv_ref[...]
    lam = (jnp.exp(jnp.sum(lv[0:1] * lv[1:2], axis=-1, keepdims=True))
           - jnp.exp(jnp.sum(lv[2:3] * lv[3:4], axis=-1, keepdims=True)) + lam_init)

    def q_block(qi, carry):
        _attn_q_block(qi, lam, q_ref, k_ref, v_ref, z_ref, bias_ref, sg_ref, o_ref,
                      m_scr, l_scr, acc_scr, s_scr, mb_scr, blk=blk, nlb=nlb, lam_init=lam_init)
        return carry

    lax.fori_loop(0, q_ref.shape[0] // blk, q_block, 0)


def _attn_q_block(qi, lam, q_ref, k_ref, v_ref, z_ref, bias_ref, sg_ref, o_ref,
                  m_scr, l_scr, acc_scr, s_scr, mb_scr, *, blk, nlb, lam_init):
    q_rows = pl.ds(pl.multiple_of(qi * blk, blk), blk)
    q = q_ref[q_rows, :]
    qs = (q[:, :HEAD_DIM], q[:, HEAD_DIM:])

    m_scr[...] = jnp.full(m_scr.shape, -jnp.inf, F32)
    l_scr[...] = jnp.zeros(l_scr.shape, F32)
    acc_scr[...] = jnp.zeros(acc_scr.shape, F32)

    def produce(j, slot, tile):
        off = pl.multiple_of(j * blk, blk)
        k = k_ref[pl.ds(off, blk), :]
        for mp in range(2):
            s = lax.dot_general(qs[mp], k[:, mp * HEAD_DIM:(mp + 1) * HEAD_DIM],
                                (((1,), (1,)), ((), ())), preferred_element_type=F32)
            if tile is not None:
                s = s + bias_ref[0, tile]
            s_scr[slot, mp] = s
            mb_scr[slot, mp] = jnp.broadcast_to(jnp.max(s, axis=-1, keepdims=True), (blk, LANES))

    def consume(j, slot, late_tile):
        off = pl.multiple_of(j * blk, blk)
        v = v_ref[pl.ds(off, blk), :]
        for mp in range(2):
            s = s_scr[slot, mp]
            if late_tile is None:
                m_blk = mb_scr[slot, mp]
            else:
                s = s + bias_ref[0, late_tile]
                m_blk = jnp.max(s, axis=-1, keepdims=True)
            m_prev = m_scr[mp]
            m_new = jnp.maximum(m_prev, m_blk)
            alpha = jnp.exp2(m_prev - m_new)
            p = jnp.exp2(s - jnp.concatenate([m_new] * nlb, axis=1))
            psum = p[:, :LANES]
            for t in range(1, nlb):
                psum = psum + p[:, t * LANES:(t + 1) * LANES]
            l_scr[mp] = alpha * l_scr[mp] + psum
            acc_scr[mp] = (jnp.concatenate([alpha] * (2 * HEAD_DIM // LANES), axis=1) * acc_scr[mp]
                           + jnp.dot(p.astype(BF16), v, preferred_element_type=F32))
            m_scr[mp] = m_new

    n_far = jnp.maximum(qi - 1, 0)
    n_pairs = n_far // 2
    produce(0, 0, None)

    def pair_body(jj, carry):
        j = 2 * jj
        produce(j + 1, 1, None)
        consume(j, 0, None)
        produce(j + 2, 0, None)
        consume(j + 1, 1, None)
        return carry

    lax.fori_loop(0, n_pairs, pair_body, 0)
    j0 = 2 * n_pairs

    @pl.when(qi == 0)
    def _():
        consume(0, 0, 0)

    @pl.when(jnp.logical_and(qi >= 1, n_far == j0))
    def _():
        produce(j0 + 1, 1, 0)
        consume(j0, 0, 1)
        consume(j0 + 1, 1, None)

    @pl.when(n_far > j0)
    def _():
        produce(j0 + 1, 1, 1)
        consume(j0, 0, None)
        produce(j0 + 2, 0, 0)
        consume(j0 + 1, 1, None)
        consume(j0 + 2, 0, None)

    l0 = jnp.sum(l_scr[0], axis=-1, keepdims=True)
    l1 = jnp.sum(l_scr[1], axis=-1, keepdims=True)
    o = acc_scr[0] / l0 - lam * (acc_scr[1] / l1)
    ms = jnp.mean(o * o, axis=-1, keepdims=True)
    o = o * lax.rsqrt(ms + SUBLN_EPS) * sg_ref[...] * (1.0 - lam_init)
    o_ref[q_rows, :] = (o * z_ref[q_rows, :].astype(F32)).astype(o_ref.dtype)


def _attention(proj, bias_tiles, lamv, subln_g, bsz, seq, d_attn, lam_init):
    hw = 2 * HEAD_DIM
    n_heads = d_attn // hw
    blk = _tile(seq, ATT_BLOCK)
    qoff, koff, voff, zoff = (2 * d_attn // hw, 3 * d_attn // hw, 4 * d_attn // hw, 5 * d_attn // hw)
    return pl.pallas_call(
        functools.partial(_attn_kernel, lam_init=lam_init),
        grid=(bsz, n_heads),
        in_specs=[
            pl.BlockSpec((seq, hw), lambda b, h: (b, qoff + h)),
            pl.BlockSpec((seq, hw), lambda b, h: (b, koff + h)),
            pl.BlockSpec((seq, hw), lambda b, h: (b, voff + h)),
            pl.BlockSpec((seq, hw), lambda b, h: (b, zoff + h)),
            pl.BlockSpec((1, 2, blk, blk), lambda b, h: (h, 0, 0, 0)),
            pl.BlockSpec((4, HEAD_DIM), lambda b, h: (0, 0)),
            pl.BlockSpec((1, hw), lambda b, h: (0, 0)),
        ],
        out_specs=pl.BlockSpec((seq, hw), lambda b, h: (b, h)),
        out_shape=jax.ShapeDtypeStruct((bsz * seq, d_attn), BF16),
        scratch_shapes=[
            pltpu.VMEM((2, blk, LANES), F32),
            pltpu.VMEM((2, blk, LANES), F32),
            pltpu.VMEM((2, blk, hw), F32),
            pltpu.VMEM((2, 2, blk, blk), F32),
            pltpu.VMEM((2, 2, blk, LANES), F32),
        ],
        compiler_params=_params(("arbitrary", "arbitrary")),
        name="diff_attention",
    )(proj, proj, proj, proj, bias_tiles, lamv, subln_g.reshape(1, hw))


def _merge_kernel(ys_ref, ya_ref, ws_hbm, wa_hbm, gs_ref, ga_ref, o_ref, stage_s, stage_a, wsb_scr, wab_scr,
                  sem_s, sem_a, *, layer, n_col_tiles):
    first = pl.program_id(1) == 0
    _stage_weights(_weight_parts(ws_hbm, layer, wsb_scr, stage_s.shape[0]), stage_s, sem_s,
                   first, pl.program_id(0), n_col_tiles)
    _stage_weights(_weight_parts(wa_hbm, layer, wab_scr, stage_a.shape[0]), stage_a, sem_a,
                   first, pl.program_id(0), n_col_tiles)
    tm = ys_ref.shape[0]
    cm = tm // MERGE_ROW_CHUNKS
    for c in range(MERGE_ROW_CHUNKS):
        rows = slice(c * cm, (c + 1) * cm)
        ps = jnp.dot(ys_ref[rows, :], wsb_scr[...], preferred_element_type=F32)
        pa = jnp.dot(ya_ref[rows, :], wab_scr[...], preferred_element_type=F32)
        o_ref[rows, :] = (gs_ref[rows, :].astype(F32) * ps + ga_ref[rows, :].astype(F32) * pa).astype(o_ref.dtype)


def _out_merge(y_s, y_a, ws_all, wa_all, layer, proj, d_ssm, d_attn):
    m, ks = y_s.shape
    ka = y_a.shape[1]
    d = ws_all.shape[2]
    tm = _tile(m, 1024)
    tn = _tile(d, 512)
    gs_off = (2 * d_ssm + 4 * d_attn) // tn
    ga_off = gs_off + d // tn
    return pl.pallas_call(
        functools.partial(_merge_kernel, layer=layer, n_col_tiles=d // tn),
        grid=(d // tn, m // tm),
        in_specs=[
            pl.BlockSpec((tm, ks), lambda j, i: (i, 0)),
            pl.BlockSpec((tm, ka), lambda j, i: (i, 0)),
            pl.BlockSpec(memory_space=pl.ANY),
            pl.BlockSpec(memory_space=pl.ANY),
            pl.BlockSpec((tm, tn), lambda j, i: (i, gs_off + j)),
            pl.BlockSpec((tm, tn), lambda j, i: (i, ga_off + j)),
        ],
        out_specs=pl.BlockSpec((tm, tn), lambda j, i: (i, j)),
        out_shape=jax.ShapeDtypeStruct((m, d), BF16),
        scratch_shapes=[pltpu.VMEM((ks, tn), F32), pltpu.VMEM((ka, tn), F32),
                        pltpu.VMEM((ks, tn), BF16), pltpu.VMEM((ka, tn), BF16),
                        pltpu.SemaphoreType.DMA(()), pltpu.SemaphoreType.DMA(())],
        compiler_params=_params(("arbitrary", "arbitrary")),
        name="out_merge",
    )(y_s, y_a, ws_all, wa_all, proj, proj)


def _resid_kernel(a_hbm, w_hbm, h_ref, m_ref, o_ref, a_buf, a_sem, stage, wb_scr, sem,
                  *, layer, n_col_tiles, n_row_tiles):
    first = jnp.logical_and(pl.program_id(1) == 0, pl.program_id(2) == 0)
    _stage_weights(_weight_parts(w_hbm, layer, wb_scr, stage.shape[0]), stage, sem,
                   first, pl.program_id(0), n_col_tiles)
    tm = a_buf.shape[1]
    row_tile = pl.program_id(1) * pl.num_programs(2) + pl.program_id(2)
    t = pl.program_id(0) * n_row_tiles + row_tile
    n_steps = n_col_tiles * n_row_tiles
    ahead = ROW_STREAM_BUFFERS - 1

    def a_copy(step):
        rows = pl.ds(pl.multiple_of(lax.rem(step, n_row_tiles) * tm, tm), tm)
        slot = lax.rem(step, ROW_STREAM_BUFFERS)
        return pltpu.make_async_copy(a_hbm.at[rows, :], a_buf.at[slot], a_sem.at[slot])

    @pl.when(t == 0)
    def _():
        for s0 in range(ahead):
            a_copy(s0).start()

    @pl.when(t + ahead < n_steps)
    def _():
        a_copy(t + ahead).start()

    a_copy(t).wait()
    slot = lax.rem(t, ROW_STREAM_BUFFERS)
    cm = tm // RESID_ROW_CHUNKS
    gate = m_ref[0, 2:3, :]
    for c in range(RESID_ROW_CHUNKS):
        rows = slice(c * cm, (c + 1) * cm)
        acc = jnp.dot(a_buf[slot, rows, :], wb_scr[...], preferred_element_type=F32)
        o_ref[rows, :] = h_ref[rows, :] + gate * acc


def _resid(merged, w_all, layer, h_res2, mod3, bsz, in_place):
    m, k = merged.shape
    d = w_all.shape[2]
    seq = m // bsz
    tm = _tile(seq, 1024)
    tn = _tile(d, 512)
    ns = seq // tm
    stage_rows = k // RESID_STAGE_PARTS
    assert (d // tn) * bsz * ns >= ROW_STREAM_BUFFERS
    return pl.pallas_call(
        functools.partial(_resid_kernel, layer=layer, n_col_tiles=d // tn, n_row_tiles=bsz * ns),
        grid=(d // tn, bsz, ns),
        in_specs=[
            pl.BlockSpec(memory_space=pl.ANY),
            pl.BlockSpec(memory_space=pl.ANY),
            pl.BlockSpec((tm, tn), lambda j, b, i: (b * ns + i, j)),
            pl.BlockSpec((1, 3, tn), lambda j, b, i: (b, 0, j)),
        ],
        out_specs=pl.BlockSpec((tm, tn), lambda j, b, i: (b * ns + i, j)),
        out_shape=jax.ShapeDtypeStruct((m, d), F32),
        input_output_aliases={2: 0} if in_place else {},
        scratch_shapes=[pltpu.VMEM((ROW_STREAM_BUFFERS, tm, k), BF16), pltpu.SemaphoreType.DMA((ROW_STREAM_BUFFERS,)),
                        pltpu.VMEM((stage_rows, tn), F32), pltpu.VMEM((k, tn), BF16), pltpu.SemaphoreType.DMA(())],
        compiler_params=_params(("arbitrary", "arbitrary", "arbitrary")),
        name="resid",
    )(merged, w_all, h_res2, mod3)


def _final_norm_kernel(x_ref, g_ref, o_ref):
    x = x_ref[...]
    ms = jnp.mean(x * x, axis=-1, keepdims=True)
    o_ref[...] = (x * lax.rsqrt(ms + EPS) * g_ref[...]).astype(o_ref.dtype)


def _final_norm(h2, g, out_dtype):
    m, d = h2.shape
    tm = _tile(m, 512)
    return pl.pallas_call(
        _final_norm_kernel,
        grid=(m // tm,),
        in_specs=[pl.BlockSpec((tm, d), lambda i: (i, 0)), pl.BlockSpec((1, d), lambda i: (0, 0))],
        out_specs=pl.BlockSpec((tm, d), lambda i: (i, 0)),
        out_shape=jax.ShapeDtypeStruct((m, d), out_dtype),
        compiler_params=_params(("arbitrary",)),
        name="final_norm",
    )(h2, g.reshape(1, d))


def kernel(x, c, norm_g, w_ada, b_ada, w_in, ssm_lambda_re, ssm_lambda_im, ssm_log_dt, ssm_b_re, ssm_b_im, ssm_c_re, ssm_c_im, ssm_d, w_glu, lambda_q1, lambda_k1, lambda_q2, lambda_k2, subln_g, w_out_ssm, w_out_attn, w_o, rel_bias, final_g):
    out_dtype = x.dtype
    bsz, seq, d = x.shape
    depth = w_in.shape[0]
    d_ssm = w_glu.shape[1]
    d_attn = w_out_attn.shape[1]
    n_heads = d_attn // (2 * HEAD_DIM)
    npb = seq // CHUNK
    nst = npb.bit_length() - 1
    assert seq % CHUNK == 0 and (1 << nst) == npb, "sequence must be CHUNK * 2^k"
    assert d_ssm % LANES == 0 and npb % SUBLANES == 0
    assert d_ssm == d_attn, "the in_proj segment map assumes equal branch widths"

    c_pad = jnp.zeros((8, d), F32).at[:bsz].set(c.astype(F32))
    mod = _ada_mod(c_pad, w_ada.astype(F32), b_ada.astype(F32))
    bias_tiles = _bias_tiles(rel_bias, n_heads, _tile(seq, ATT_BLOCK))

    h_res = x.astype(F32)
    for l in range(depth):
        mod3 = mod[l, :bsz].reshape(bsz, 3, d)
        hn = _prenorm(h_res, norm_g[l].astype(F32), mod3)
        proj = _in_proj(hn.reshape(bsz * seq, d), w_in.astype(F32), l, d_ssm)

        g_act = _s5(proj, ssm_lambda_re[l].astype(F32), ssm_lambda_im[l].astype(F32), ssm_log_dt[l].astype(F32),
                    ssm_b_re[l].astype(F32), ssm_b_im[l].astype(F32), ssm_c_re[l].astype(F32),
                    ssm_c_im[l].astype(F32), ssm_d[l].astype(F32), nst, npb, d_ssm)
        y_s = _glu(g_act, w_glu.astype(F32), l, proj, d_ssm)

        lam_init = 0.8 - 0.6 * math.exp(-0.3 * l)
        lamv = jnp.stack([lambda_q1[l], lambda_k1[l], lambda_q2[l], lambda_k2[l]]).astype(F32)
        y_a = _attention(proj, bias_tiles, lamv, subln_g[l].astype(F32), bsz, seq, d_attn, lam_init)

        merged = _out_merge(y_s, y_a, w_out_ssm.astype(F32), w_out_attn.astype(F32), l, proj, d_ssm, d_attn)
        h_res = _resid(merged, w_o.astype(F32), l, h_res.reshape(bsz * seq, d), mod3, bsz,
                       in_place=l > 0).reshape(bsz, seq, d)

    out = _final_norm(h_res.reshape(bsz * seq, d), final_g.astype(F32), out_dtype)
    return out.reshape(bsz, seq, d)
```

```python
import functools
import math

import jax
import jax.numpy as jnp
from jax import lax
from jax.experimental import pallas as pl
from jax.experimental.pallas import tpu as pltpu

F32 = jnp.float32
BF16 = jnp.bfloat16

SSM_GROUP = 16
STATE = 64
HEAD_DIM = 128
N_BUCKETS = 32
MAX_DISTANCE = 128
EPS = 1e-6
SUBLN_EPS = 1e-5
NEG_INF = -1e30

LANES = 128
SUBLANES = 8
T_COL_BLOCK = 256
CHUNK = 16
SW = 2 * STATE
GROUPS_PER_STEP = LANES // SSM_GROUP
XW = CHUNK * LANES
HW = GROUPS_PER_STEP * SW
ATT_BLOCK = 512
MERGE_ROW_CHUNKS = 2
PREFETCH_DMA_PRIORITY = 1
RESID_ROW_CHUNKS = 2
RESID_STAGE_PARTS = 1
INPROJ_ROW_CHUNKS = 2
LOG2E = math.log2(math.e)
V7X_VMEM_LIMIT = 56 * 1024 * 1024


def _sigmoid(x):
    return 0.5 * jnp.tanh(0.5 * x) + 0.5


def _silu(x):
    return x * _sigmoid(x)


def _gelu_tanh(x):
    return 0.5 * x * (1.0 + jnp.tanh(math.sqrt(2.0 / math.pi) * (x + 0.044715 * (x * x * x))))


def _tile(dim, pref):
    t = min(dim, pref)
    assert dim % t == 0, (dim, pref)
    return t


def _params(sem):
    return pltpu.CompilerParams(dimension_semantics=sem, vmem_limit_bytes=V7X_VMEM_LIMIT)


def _ada_kernel(c_ref, w_ref, b_ref, o_ref):
    c = c_ref[...]
    ca = _silu(c).astype(BF16)
    o_ref[0] = jnp.dot(ca, w_ref[0].astype(BF16), preferred_element_type=F32) + b_ref[0]


def _ada_mod(c_pad, w_ada, b_ada):
    n_layers, d, n3 = w_ada.shape
    rows = c_pad.shape[0]
    tn = _tile(n3, 1024)
    return pl.pallas_call(
        _ada_kernel,
        grid=(n_layers, n3 // tn),
        in_specs=[
            pl.BlockSpec((rows, d), lambda l, j: (0, 0)),
            pl.BlockSpec((1, d, tn), lambda l, j: (l, 0, j)),
            pl.BlockSpec((1, 1, tn), lambda l, j: (l, 0, j)),
        ],
        out_specs=pl.BlockSpec((1, rows, tn), lambda l, j: (l, 0, j)),
        out_shape=jax.ShapeDtypeStruct((n_layers, rows, n3), F32),
        compiler_params=_params(("arbitrary", "arbitrary")),
        name="ada_mod",
    )(c_pad, w_ada, b_ada.reshape(n_layers, 1, n3))


def _prenorm_kernel(x_ref, g_ref, m_ref, o_ref):
    x = x_ref[0]
    ms = jnp.mean(x * x, axis=-1, keepdims=True)
    shift = m_ref[0, 0:1, :]
    scale = m_ref[0, 1:2, :]
    h = x * lax.rsqrt(ms + EPS) * g_ref[...]
    o_ref[0] = (h * (1.0 + scale) + shift).astype(o_ref.dtype)


def _prenorm(h_res, g, mod3):
    b, s, d = h_res.shape
    ts = _tile(s, 512)
    return pl.pallas_call(
        _prenorm_kernel,
        grid=(b, s // ts),
        in_specs=[
            pl.BlockSpec((1, ts, d), lambda i, j: (i, j, 0)),
            pl.BlockSpec((1, d), lambda i, j: (0, 0)),
            pl.BlockSpec((1, 3, d), lambda i, j: (i, 0, 0)),
        ],
        out_specs=pl.BlockSpec((1, ts, d), lambda i, j: (i, j, 0)),
        out_shape=jax.ShapeDtypeStruct((b, s, d), BF16),
        compiler_params=_params(("arbitrary", "arbitrary")),
        name="prenorm",
    )(h_res, g.reshape(1, d), mod3)


def _stage_weights(parts, stage, sem, first, col_tile, n_col_tiles):
    def copy(part, tile):
        return pltpu.make_async_copy(parts[part][0](tile), stage, sem)

    @pl.when(first)
    def _():
        @pl.when(col_tile == 0)
        def _():
            copy(0, 0).start()

        for part in range(len(parts)):
            copy(part, col_tile).wait()
            parts[part][1][...] = stage[...].astype(BF16)
            if part + 1 < len(parts):
                copy(part + 1, col_tile).start()
            else:
                @pl.when(col_tile + 1 < n_col_tiles)
                def _():
                    copy(0, col_tile + 1).start(priority=PREFETCH_DMA_PRIORITY)


def _weight_parts(w_hbm, layer, wb_scr, stage_rows):
    k, tn = wb_scr.shape
    parts = []
    for r0 in range(0, k, stage_rows):
        rows = pl.ds(r0, stage_rows)
        parts.append((lambda tile, rows=rows: w_hbm.at[layer, rows, pl.ds(pl.multiple_of(tile * tn, tn), tn)],
                      wb_scr.at[rows, :]))
    return parts


def _inproj_kernel(a_ref, w_hbm, o_ref, stage, wb_scr, sem, *, seg_tiles, layer, n_col_tiles):
    j = pl.program_id(0)
    _stage_weights(_weight_parts(w_hbm, layer, wb_scr, stage.shape[0]), stage, sem,
                   pl.program_id(1) == 0, j, n_col_tiles)

    st = seg_tiles
    is_silu = jnp.logical_or(jnp.logical_and(j >= st, j < 2 * st), jnp.logical_and(j >= 5 * st, j < 6 * st))
    is_q = jnp.logical_and(j >= 2 * st, j < 3 * st)
    is_gate = jnp.logical_or(is_silu, j >= 6 * st)

    tm = a_ref.shape[0]
    cm = tm // INPROJ_ROW_CHUNKS

    def run(epilogue):
        for c in range(INPROJ_ROW_CHUNKS):
            acc = jnp.dot(a_ref[c * cm:(c + 1) * cm, :], wb_scr[...], preferred_element_type=F32)
            o_ref[c * cm:(c + 1) * cm, :] = epilogue(acc).astype(o_ref.dtype)

    @pl.when(jnp.logical_not(is_gate))
    def _():
        scale = jnp.where(is_q, HEAD_DIM ** -0.5 * LOG2E, 1.0).astype(F32)
        run(lambda acc: acc * scale)

    @pl.when(is_gate)
    def _():
        run(lambda acc: _sigmoid(acc) * jnp.where(is_silu, acc, 1.0))


def _in_proj(a, w_all, layer, d_ssm):
    m, k = a.shape
    n = w_all.shape[2]
    tm = _tile(m, 1024)
    tn = _tile(d_ssm, 1024)
    return pl.pallas_call(
        functools.partial(_inproj_kernel, seg_tiles=d_ssm // tn, layer=layer, n_col_tiles=n // tn),
        grid=(n // tn, m // tm),
        in_specs=[
            pl.BlockSpec((tm, k), lambda j, i: (i, 0)),
            pl.BlockSpec(memory_space=pl.ANY),
        ],
        out_specs=pl.BlockSpec((tm, tn), lambda j, i: (i, j)),
        out_shape=jax.ShapeDtypeStruct((m, n), BF16),
        scratch_shapes=[pltpu.VMEM((k, tn), F32), pltpu.VMEM((k, tn), BF16), pltpu.SemaphoreType.DMA(())],
        compiler_params=_params(("arbitrary", "arbitrary")),
        name="in_proj",
    )(a, w_all)


def _s5_kernel(x_ref, lr_ref, li_ref, ldt_ref, ba_ref, bb_ref, ca_ref, cb_ref, d_ref, o_ref,
               t_scr, r_scr, ot_scr, xs_scr, os_scr, ha_scr, hb_scr, *, nst, npb):
    nc = x_ref.shape[0] // CHUNK
    gps = GROUPS_PER_STEP

    @pl.when(pl.program_id(0) == 0)
    def _():
        t_scr[...] = jnp.zeros(t_scr.shape, t_scr.dtype)

    lr = lr_ref[...]
    li = li_ref[...]
    dt = jnp.exp(ldt_ref[...])
    mag = jnp.exp(lr * dt)
    are = mag * jnp.cos(li * dt)
    aim = mag * jnp.sin(li * dt)
    den = lr * lr + li * li
    nre = are - 1.0
    cre = (nre * lr + aim * li) / den
    cim = (aim * lr - nre * li) / den
    b_a = ba_ref[...]
    b_b = bb_ref[...]
    c_a = ca_ref[...]
    c_b = cb_ref[...]
    cre3 = cre[:, None, :]
    cim3 = cim[:, None, :]
    bbv = cre3 * b_a + cim3 * b_b
    bbs = cre3 * b_b - cim3 * b_a

    pre = jnp.ones_like(are)
    pim = jnp.zeros_like(are)
    pows = [(pre, pim)]
    for _ in range(CHUNK):
        pre, pim = are * pre - aim * pim, are * pim + aim * pre
        pows.append((pre, pim))

    same_group = (lax.broadcasted_iota(jnp.int32, (LANES, LANES), 0) // SSM_GROUP
                  == lax.broadcasted_iota(jnp.int32, (LANES, LANES), 1) // SSM_GROUP)
    own_state = (lax.broadcasted_iota(jnp.int32, (LANES, HW), 0) // SSM_GROUP
                 == lax.broadcasted_iota(jnp.int32, (LANES, HW), 1) // SW)
    bbv2 = bbv.reshape(LANES, SW)
    for k in range(CHUNK + 1):
        pr, pi = pows[k]
        z_k = (pr[:, None, :] * c_a + pi[:, None, :] * c_b).reshape(LANES, SW)
        if k < CHUNK:
            kk = lax.dot_general(bbv2, z_k, (((1,), (1,)), ((), ())),
                                 precision=lax.Precision.HIGHEST, preferred_element_type=F32)
            kk = jnp.where(same_group, kk, 0.0).astype(t_scr.dtype)
            for s in range(CHUNK - k):
                t = s + k
                t_scr[s * LANES:(s + 1) * LANES, t * LANES:(t + 1) * LANES] = kk
        if k >= 1:
            ot_scr[(k - 1) * LANES:k * LANES, :] = jnp.where(
                own_state, jnp.concatenate([z_k] * gps, axis=1), 0.0).astype(ot_scr.dtype)
    for s in range(CHUNK):
        pr, pi = pows[CHUNK - 1 - s]
        r_s = (pr[:, None, :] * bbv + pi[:, None, :] * bbs).reshape(LANES, SW)
        r_scr[s * LANES:(s + 1) * LANES, :] = jnp.where(
            own_state, jnp.concatenate([r_s] * gps, axis=1), 0.0).astype(r_scr.dtype)

    first_half = lax.broadcasted_iota(jnp.int32, are.shape, 1) < STATE
    sre, sim = pows[CHUNK]
    steps = []
    for _ in range(nst):
        steps.append((sre, jnp.where(first_half, -sim, sim)))
        sre, sim = sre * sre - sim * sim, 2.0 * sre * sim
    low_steps = SUBLANES.bit_length() - 1
    ure, uim = pows[CHUNK]
    qre, qim = ure, uim
    rowp = []
    for _ in range(SUBLANES):
        rowp.append((qre, jnp.where(first_half, -qim, qim)))
        qre, qim = ure * qre - uim * qim, ure * qim + uim * qre

    xs_scr[...] = x_ref[...].astype(F32)
    xcat = jnp.concatenate([xs_scr[pl.ds(s, nc, stride=CHUNK), :].astype(BF16) for s in range(CHUNK)], axis=1)

    row = lax.broadcasted_iota(jnp.int32, (nc, SW), 0)
    row_in_batch = jnp.bitwise_and(row, npb - 1)
    row_in_tile = jnp.bitwise_and(row, SUBLANES - 1)
    n_tiles = nc // SUBLANES
    tile_in_batch = jnp.bitwise_and(lax.broadcasted_iota(jnp.int32, (n_tiles, SW), 0), npb // SUBLANES - 1)

    def cmul_acc(acc, coef, x):
        return acc + coef[0] * x + coef[1] * pltpu.roll(x, STATE, axis=1)

    r = jnp.dot(xcat, r_scr[...], preferred_element_type=F32)
    h_ins = []
    for g in range(gps):
        coef = [(ar[g:g + 1, :], ai[g:g + 1, :]) for ar, ai in steps]
        h = r[:, g * SW:(g + 1) * SW]
        for k in range(low_steps):
            d = 1 << k
            h = cmul_acc(h, coef[k], jnp.where(row_in_tile >= d, pltpu.roll(h, d, axis=0), 0.0))
        ha_scr[...] = h
        t = ha_scr[pl.ds(SUBLANES - 1, n_tiles, stride=SUBLANES), :]
        for k in range(low_steps, nst):
            d = 1 << (k - low_steps)
            t = cmul_acc(t, coef[k], jnp.where(tile_in_batch >= d, pltpu.roll(t, d, axis=0), 0.0))
        t_in = jnp.where(tile_in_batch >= 1, pltpu.roll(t, 1, axis=0), 0.0)
        zero = jnp.zeros_like(t_in)
        for i in range(SUBLANES):
            hb_scr[pl.ds(i, n_tiles, stride=SUBLANES), :] = cmul_acc(zero, (rowp[i][0][g:g + 1, :],
                                                                          rowp[i][1][g:g + 1, :]), t_in)
        h = h + hb_scr[...]
        h_in = jnp.where(row_in_batch >= 1, pltpu.roll(h, 1, axis=0), 0.0)
        h_ins.append(h_in.astype(BF16))
    hcat = jnp.concatenate(h_ins, axis=1)

    y_cols = []
    for c0 in range(0, XW, T_COL_BLOCK):
        c1 = c0 + T_COL_BLOCK
        y_cols.append(jnp.dot(xcat[:, :c1], t_scr[:c1, c0:c1], preferred_element_type=F32))
    y = jnp.concatenate(y_cols, axis=1)
    y = y + lax.dot_general(hcat, ot_scr[...], (((1,), (1,)), ((), ())), preferred_element_type=F32)
    y = y + d_ref[0] * xcat.astype(F32)
    gact = _gelu_tanh(y)
    for s in range(CHUNK):
        os_scr[pl.ds(s, nc, stride=CHUNK), :] = gact[:, s * LANES:(s + 1) * LANES]
    o_ref[...] = os_scr[...].astype(o_ref.dtype)


def _s5(proj, lam_re, lam_im, log_dt, b_re, b_im, c_re, c_im, d_skip, nst, npb, d_ssm):
    rows = proj.shape[0]
    g = d_ssm // SSM_GROUP
    gps = GROUPS_PER_STEP
    lr2 = jnp.concatenate([lam_re, lam_re], axis=-1)
    li2 = jnp.concatenate([lam_im, lam_im], axis=-1)
    ldt2 = jnp.broadcast_to(log_dt[:, None], (g, SW))
    brt = jnp.swapaxes(b_re, 1, 2)
    bit = jnp.swapaxes(b_im, 1, 2)
    b_a = jnp.concatenate([brt, bit], axis=-1)
    b_b = jnp.concatenate([-bit, brt], axis=-1)
    c_a = jnp.concatenate([c_re, -c_im], axis=-1)
    c_b = jnp.concatenate([-c_im, -c_re], axis=-1)
    d_lane = jnp.tile(d_skip.reshape(g // gps, 1, LANES), (1, 1, CHUNK))
    vec = pl.BlockSpec((gps, SW), lambda i: (i, 0))
    mat = pl.BlockSpec((gps, SSM_GROUP, SW), lambda i: (i, 0, 0))
    return pl.pallas_call(
        functools.partial(_s5_kernel, nst=nst, npb=npb),
        grid=(g // gps,),
        in_specs=[pl.BlockSpec((rows, LANES), lambda i: (0, i)), vec, vec, vec, mat, mat, mat, mat,
                  pl.BlockSpec((1, 1, XW), lambda i: (i, 0, 0))],
        out_specs=pl.BlockSpec((rows, LANES), lambda i: (0, i)),
        out_shape=jax.ShapeDtypeStruct((rows, d_ssm), BF16),
        scratch_shapes=[
            pltpu.VMEM((XW, XW), BF16),
            pltpu.VMEM((XW, HW), BF16),
            pltpu.VMEM((XW, HW), BF16),
            pltpu.VMEM((rows, LANES), F32),
            pltpu.VMEM((rows, LANES), F32),
            pltpu.VMEM((rows // CHUNK, SW), F32),
            pltpu.VMEM((rows // CHUNK, SW), F32),
        ],
        compiler_params=_params(("arbitrary",)),
        name="s5",
    )(proj, lr2, li2, ldt2, b_a, b_b, c_a, c_b, d_lane)


def _cast_weight_once(w_ref, wb_scr, first):
    @pl.when(first)
    def _():
        wb_scr[...] = w_ref[0].astype(BF16)


def _glu_kernel(a_ref, w_ref, z_ref, o_ref, wb_scr):
    _cast_weight_once(w_ref, wb_scr, pl.program_id(1) == 0)
    tn = o_ref.shape[1]
    acc = jnp.dot(a_ref[...], wb_scr[...], preferred_element_type=F32)
    g = a_ref[:, pl.ds(pl.multiple_of(pl.program_id(0) * tn, tn), tn)].astype(F32)
    o_ref[...] = (g * _sigmoid(acc) * z_ref[...].astype(F32)).astype(o_ref.dtype)


def _glu(g_act, w_all, layer, proj, d_ssm):
    m, k = g_act.shape
    tm = _tile(m, 1024)
    tn = _tile(d_ssm, 1024)
    zoff = d_ssm // tn
    return pl.pallas_call(
        _glu_kernel,
        grid=(d_ssm // tn, m // tm),
        in_specs=[
            pl.BlockSpec((tm, k), lambda j, i: (i, 0)),
            pl.BlockSpec((1, k, tn), lambda j, i: (layer, 0, j)),
            pl.BlockSpec((tm, tn), lambda j, i: (i, zoff + j)),
        ],
        out_specs=pl.BlockSpec((tm, tn), lambda j, i: (i, j)),
        out_shape=jax.ShapeDtypeStruct((m, d_ssm), BF16),
        scratch_shapes=[pltpu.VMEM((k, tn), BF16)],
        compiler_params=_params(("arbitrary", "arbitrary")),
        name="glu",
    )(g_act, w_all, proj)


def _bias_kernel(tab_ref, o_ref):
    h = pl.program_id(0)
    blk = o_ref.shape[-1]
    nb = blk // LANES
    r = lax.broadcasted_iota(jnp.int32, (LANES, LANES), 0)
    c = lax.broadcasted_iota(jnp.int32, (LANES, LANES), 1)
    max_exact = N_BUCKETS // 2
    far = tab_ref[h * N_BUCKETS + N_BUCKETS - 1]

    def band(offset):
        rel = r - c + offset
        n = jnp.maximum(rel, 0)
        nf = jnp.maximum(n, 1).astype(F32)
        large = max_exact + (jnp.log(nf / max_exact) / math.log(MAX_DISTANCE / max_exact)
                             * (N_BUCKETS - max_exact)).astype(jnp.int32)
        large = jnp.minimum(large, N_BUCKETS - 1)
        bucket = jnp.where(n < max_exact, n, large)
        val = jnp.zeros((LANES, LANES), F32)
        for b in range(N_BUCKETS):
            val = jnp.where(bucket == b, tab_ref[h * N_BUCKETS + b], val)
        return jnp.where(rel >= 0, (val - far) * LOG2E, NEG_INF)

    near = band(0)
    next_band = band(LANES)
    zeros = jnp.zeros((LANES, LANES), F32)
    masked = jnp.full((LANES, LANES), NEG_INF, F32)
    for a in range(nb):
        for b in range(nb):
            rows, cols = slice(a * LANES, (a + 1) * LANES), slice(b * LANES, (b + 1) * LANES)
            o_ref[0, 0, rows, cols] = near if b == a else next_band if b == a - 1 else zeros if b < a else masked
            o_ref[0, 1, rows, cols] = next_band if (a == 0 and b == nb - 1) else zeros


def _bias_tiles(rel_bias, n_heads, blk):
    assert MAX_DISTANCE <= LANES <= blk and blk % LANES == 0
    tab = jnp.transpose(rel_bias.astype(F32)).reshape(-1)
    return pl.pallas_call(
        _bias_kernel,
        grid=(n_heads,),
        in_specs=[pl.BlockSpec(memory_space=pltpu.SMEM)],
        out_specs=pl.BlockSpec((1, 2, blk, blk), lambda h: (h, 0, 0, 0)),
        out_shape=jax.ShapeDtypeStruct((n_heads, 2, blk, blk), F32),
        compiler_params=_params(("arbitrary",)),
        name="t5_bias_tiles",
    )(tab)


def _attn_kernel(q_ref, k_ref, v_ref, z_ref, bias_ref, lamv_ref, sg_ref, o_ref,
                 m_scr, l_scr, acc_scr, s_scr, mb_scr, *, lam_init):
    blk = s_scr.shape[-1]
    nlb = blk // LANES
    lv = lamv_ref[...]
    lam = (jnp.exp(jnp.sum(lv[0:1] * lv[1:2], axis=-1, keepdims=True))
           - jnp.exp(jnp.sum(lv[2:3] * lv[3:4], axis=-1, keepdims=True)) + lam_init)

    def q_block(qi, carry):
        _attn_q_block(qi, lam, q_ref, k_ref, v_ref, z_ref, bias_ref, sg_ref, o_ref,
                      m_scr, l_scr, acc_scr, s_scr, mb_scr, blk=blk, nlb=nlb, lam_init=lam_init)
        return carry

    lax.fori_loop(0, q_ref.shape[0] // blk, q_block, 0)


def _attn_q_block(qi, lam, q_ref, k_ref, v_ref, z_ref, bias_ref, sg_ref, o_ref,
                  m_scr, l_scr, acc_scr, s_scr, mb_scr, *, blk, nlb, lam_init):
    q_rows = pl.ds(pl.multiple_of(qi * blk, blk), blk)
    q = q_ref[q_rows, :]
    qs = (q[:, :HEAD_DIM], q[:, HEAD_DIM:])

    m_scr[...] = jnp.full(m_scr.shape, -jnp.inf, F32)
    l_scr[...] = jnp.zeros(l_scr.shape, F32)
    acc_scr[...] = jnp.zeros(acc_scr.shape, F32)

    def produce(j, slot, tile):
        off = pl.multiple_of(j * blk, blk)
        k = k_ref[pl.ds(off, blk), :]
        for mp in range(2):
            s = lax.dot_general(qs[mp], k[:, mp * HEAD_DIM:(mp + 1) * HEAD_DIM],
                                (((1,), (1,)), ((), ())), preferred_element_type=F32)
            if tile is not None:
                s = s + bias_ref[0, tile]
            s_scr[slot, mp] = s
            mb_scr[slot, mp] = jnp.broadcast_to(jnp.max(s, axis=-1, keepdims=True), (blk, LANES))

    def consume(j, slot, late_tile):
        off = pl.multiple_of(j * blk, blk)
        v = v_ref[pl.ds(off, blk), :]
        for mp in range(2):
            s = s_scr[slot, mp]
            if late_tile is None:
                m_blk = mb_scr[slot, mp]
            else:
                s = s + bias_ref[0, late_tile]
                m_blk = jnp.max(s, axis=-1, keepdims=True)
            m_prev = m_scr[mp]
            m_new = jnp.maximum(m_prev, m_blk)
            alpha = jnp.exp2(m_prev - m_new)
            p = jnp.exp2(s - jnp.concatenate([m_new] * nlb, axis=1))
            psum = p[:, :LANES]
            for t in range(1, nlb):
                psum = psum + p[:, t * LANES:(t + 1) * LANES]
            l_scr[mp] = alpha * l_scr[mp] + psum
            acc_scr[mp] = (jnp.concatenate([alpha] * (2 * HEAD_DIM // LANES), axis=1) * acc_scr[mp]
                           + jnp.dot(p.astype(BF16), v, preferred_element_type=F32))
            m_scr[mp] = m_new

    n_far = jnp.maximum(qi - 1, 0)
    n_pairs = n_far // 2
    produce(0, 0, None)

    def pair_body(jj, carry):
        j = 2 * jj
        produce(j + 1, 1, None)
        consume(j, 0, None)
        produce(j + 2, 0, None)
        consume(j + 1, 1, None)
        return carry

    lax.fori_loop(0, n_pairs, pair_body, 0)
    j0 = 2 * n_pairs

    @pl.when(qi == 0)
    def _():
        consume(0, 0, 0)

    @pl.when(jnp.logical_and(qi >= 1, n_far == j0))
    def _():
        produce(j0 + 1, 1, 0)
        consume(j0, 0, 1)
        consume(j0 + 1, 1, None)

    @pl.when(n_far > j0)
    def _():
        produce(j0 + 1, 1, 1)
        consume(j0, 0, None)
        produce(j0 + 2, 0, 0)
        consume(j0 + 1, 1, None)
        consume(j0 + 2, 0, None)

    l0 = jnp.sum(l_scr[0], axis=-1, keepdims=True)
    l1 = jnp.sum(l_scr[1], axis=-1, keepdims=True)
    o = acc_scr[0] / l0 - lam * (acc_scr[1] / l1)
    ms = jnp.mean(o * o, axis=-1, keepdims=True)
    o = o * lax.rsqrt(ms + SUBLN_EPS) * sg_ref[...] * (1.0 - lam_init)
    o_ref[q_rows, :] = (o * z_ref[q_rows, :].astype(F32)).astype(o_ref.dtype)


def _attention(proj, bias_tiles, lamv, subln_g, bsz, seq, d_attn, lam_init):
    hw = 2 * HEAD_DIM
    n_heads = d_attn // hw
    blk = _tile(seq, ATT_BLOCK)
    qoff, koff, voff, zoff = (2 * d_attn // hw, 3 * d_attn // hw, 4 * d_attn // hw, 5 * d_attn // hw)
    return pl.pallas_call(
        functools.partial(_attn_kernel, lam_init=lam_init),
        grid=(bsz, n_heads),
        in_specs=[
            pl.BlockSpec((seq, hw), lambda b, h: (b, qoff + h)),
            pl.BlockSpec((seq, hw), lambda b, h: (b, koff + h)),
            pl.BlockSpec((seq, hw), lambda b, h: (b, voff + h)),
            pl.BlockSpec((seq, hw), lambda b, h: (b, zoff + h)),
            pl.BlockSpec((1, 2, blk, blk), lambda b, h: (h, 0, 0, 0)),
            pl.BlockSpec((4, HEAD_DIM), lambda b, h: (0, 0)),
            pl.BlockSpec((1, hw), lambda b, h: (0, 0)),
        ],
        out_specs=pl.BlockSpec((seq, hw), lambda b, h: (b, h)),
        out_shape=jax.ShapeDtypeStruct((bsz * seq, d_attn), BF16),
        scratch_shapes=[
            pltpu.VMEM((2, blk, LANES), F32),
            pltpu.VMEM((2, blk, LANES), F32),
            pltpu.VMEM((2, blk, hw), F32),
            pltpu.VMEM((2, 2, blk, blk), F32),
            pltpu.VMEM((2, 2, blk, LANES), F32),
        ],
        compiler_params=_params(("arbitrary", "arbitrary")),
        name="diff_attention",
    )(proj, proj, proj, proj, bias_tiles, lamv, subln_g.reshape(1, hw))


def _merge_kernel(ys_ref, ya_ref, ws_hbm, wa_hbm, gs_ref, ga_ref, o_ref, stage_s, stage_a, wsb_scr, wab_scr,
                  sem_s, sem_a, *, layer, n_col_tiles):
    first = pl.program_id(1) == 0
    _stage_weights(_weight_parts(ws_hbm, layer, wsb_scr, stage_s.shape[0]), stage_s, sem_s,
                   first, pl.program_id(0), n_col_tiles)
    _stage_weights(_weight_parts(wa_hbm, layer, wab_scr, stage_a.shape[0]), stage_a, sem_a,
                   first, pl.program_id(0), n_col_tiles)
    tm = ys_ref.shape[0]
    cm = tm // MERGE_ROW_CHUNKS
    for c in range(MERGE_ROW_CHUNKS):
        rows = slice(c * cm, (c + 1) * cm)
        ps = jnp.dot(ys_ref[rows, :], wsb_scr[...], preferred_element_type=F32)
        pa = jnp.dot(ya_ref[rows, :], wab_scr[...], preferred_element_type=F32)
        o_ref[rows, :] = (gs_ref[rows, :].astype(F32) * ps + ga_ref[rows, :].astype(F32) * pa).astype(o_ref.dtype)


def _out_merge(y_s, y_a, ws_all, wa_all, layer, proj, d_ssm, d_attn):
    m, ks = y_s.shape
    ka = y_a.shape[1]
    d = ws_all.shape[2]
    tm = _tile(m, 1024)
    tn = _tile(d, 512)
    gs_off = (2 * d_ssm + 4 * d_attn) // tn
    ga_off = gs_off + d // tn
    return pl.pallas_call(
        functools.partial(_merge_kernel, layer=layer, n_col_tiles=d // tn),
        grid=(d // tn, m // tm),
        in_specs=[
            pl.BlockSpec((tm, ks), lambda j, i: (i, 0)),
            pl.BlockSpec((tm, ka), lambda j, i: (i, 0)),
            pl.BlockSpec(memory_space=pl.ANY),
            pl.BlockSpec(memory_space=pl.ANY),
            pl.BlockSpec((tm, tn), lambda j, i: (i, gs_off + j)),
            pl.BlockSpec((tm, tn), lambda j, i: (i, ga_off + j)),
        ],
        out_specs=pl.BlockSpec((tm, tn), lambda j, i: (i, j)),
        out_shape=jax.ShapeDtypeStruct((m, d), BF16),
        scratch_shapes=[pltpu.VMEM((ks, tn), F32), pltpu.VMEM((ka, tn), F32),
                        pltpu.VMEM((ks, tn), BF16), pltpu.VMEM((ka, tn), BF16),
                        pltpu.SemaphoreType.DMA(()), pltpu.SemaphoreType.DMA(())],
        compiler_params=_params(("arbitrary", "arbitrary")),
        name="out_merge",
    )(y_s, y_a, ws_all, wa_all, proj, proj)


def _resid_kernel(a_ref, w_hbm, h_ref, m_ref, o_ref, stage, wb_scr, sem, *, layer, n_col_tiles):
    first = jnp.logical_and(pl.program_id(1) == 0, pl.program_id(2) == 0)
    _stage_weights(_weight_parts(w_hbm, layer, wb_scr, stage.shape[0]), stage, sem,
                   first, pl.program_id(0), n_col_tiles)
    tm = a_ref.shape[0]
    cm = tm // RESID_ROW_CHUNKS
    gate = m_ref[0, 2:3, :]
    for c in range(RESID_ROW_CHUNKS):
        rows = slice(c * cm, (c + 1) * cm)
        acc = jnp.dot(a_ref[rows, :], wb_scr[...], preferred_element_type=F32)
        o_ref[rows, :] = h_ref[rows, :] + gate * acc


def _resid(merged, w_all, layer, h_res2, mod3, bsz, in_place):
    m, k = merged.shape
    d = w_all.shape[2]
    seq = m // bsz
    tm = _tile(seq, 1024)
    tn = _tile(d, 512)
    ns = seq // tm
    stage_rows = k // RESID_STAGE_PARTS
    return pl.pallas_call(
        functools.partial(_resid_kernel, layer=layer, n_col_tiles=d // tn),
        grid=(d // tn, bsz, ns),
        in_specs=[
            pl.BlockSpec((tm, k), lambda j, b, i: (b * ns + i, 0)),
            pl.BlockSpec(memory_space=pl.ANY),
            pl.BlockSpec((tm, tn), lambda j, b, i: (b * ns + i, j)),
            pl.BlockSpec((1, 3, tn), lambda j, b, i: (b, 0, j)),
        ],
        out_specs=pl.BlockSpec((tm, tn), lambda j, b, i: (b * ns + i, j)),
        out_shape=jax.ShapeDtypeStruct((m, d), F32),
        input_output_aliases={2: 0} if in_place else {},
        scratch_shapes=[pltpu.VMEM((stage_rows, tn), F32), pltpu.VMEM((k, tn), BF16), pltpu.SemaphoreType.DMA(())],
        compiler_params=_params(("arbitrary", "arbitrary", "arbitrary")),
        name="resid",
    )(merged, w_all, h_res2, mod3)


def _final_norm_kernel(x_ref, g_ref, o_ref):
    x = x_ref[...]
    ms = jnp.mean(x * x, axis=-1, keepdims=True)
    o_ref[...] = (x * lax.rsqrt(ms + EPS) * g_ref[...]).astype(o_ref.dtype)


def _final_norm(h2, g, out_dtype):
    m, d = h2.shape
    tm = _tile(m, 512)
    return pl.pallas_call(
        _final_norm_kernel,
        grid=(m // tm,),
        in_specs=[pl.BlockSpec((tm, d), lambda i: (i, 0)), pl.BlockSpec((1, d), lambda i: (0, 0))],
        out_specs=pl.BlockSpec((tm, d), lambda i: (i, 0)),
        out_shape=jax.ShapeDtypeStruct((m, d), out_dtype),
        compiler_params=_params(("arbitrary",)),
        name="final_norm",
    )(h2, g.reshape(1, d))


def kernel(x, c, norm_g, w_ada, b_ada, w_in, ssm_lambda_re, ssm_lambda_im, ssm_log_dt, ssm_b_re, ssm_b_im, ssm_c_re, ssm_c_im, ssm_d, w_glu, lambda_q1, lambda_k1, lambda_q2, lambda_k2, subln_g, w_out_ssm, w_out_attn, w_o, rel_bias, final_g):
    out_dtype = x.dtype
    bsz, seq, d = x.shape
    depth = w_in.shape[0]
    d_ssm = w_glu.shape[1]
    d_attn = w_out_attn.shape[1]
    n_heads = d_attn // (2 * HEAD_DIM)
    npb = seq // CHUNK
    nst = npb.bit_length() - 1
    assert seq % CHUNK == 0 and (1 << nst) == npb, "sequence must be CHUNK * 2^k"
    assert d_ssm % LANES == 0 and npb % SUBLANES == 0
    assert d_ssm == d_attn, "the in_proj segment map assumes equal branch widths"

    c_pad = jnp.zeros((8, d), F32).at[:bsz].set(c.astype(F32))
    mod = _ada_mod(c_pad, w_ada.astype(F32), b_ada.astype(F32))
    bias_tiles = _bias_tiles(rel_bias, n_heads, _tile(seq, ATT_BLOCK))

    h_res = x.astype(F32)
    for l in range(depth):
        mod3 = mod[l, :bsz].reshape(bsz, 3, d)
        hn = _prenorm(h_res, norm_g[l].astype(F32), mod3)
        proj = _in_proj(hn.reshape(bsz * seq, d), w_in.astype(F32), l, d_ssm)

        g_act = _s5(proj, ssm_lambda_re[l].astype(F32), ssm_lambda_im[l].astype(F32), ssm_log_dt[l].astype(F32),
                    ssm_b_re[l].astype(F32), ssm_b_im[l].astype(F32), ssm_c_re[l].astype(F32),
                    ssm_c_im[l].astype(F32), ssm_d[l].astype(F32), nst, npb, d_ssm)
        y_s = _glu(g_act, w_glu.astype(F32), l, proj, d_ssm)

        lam_init = 0.8 - 0.6 * math.exp(-0.3 * l)
        lamv = jnp.stack([lambda_q1[l], lambda_k1[l], lambda_q2[l], lambda_k2[l]]).astype(F32)
        y_a = _attention(proj, bias_tiles, lamv, subln_g[l].astype(F32), bsz, seq, d_attn, lam_init)

        merged = _out_merge(y_s, y_a, w_out_ssm.astype(F32), w_out_attn.astype(F32), l, proj, d_ssm, d_attn)
        h_res = _resid(merged, w_o.astype(F32), l, h_res.reshape(bsz * seq, d), mod3, bsz,
                       in_place=l > 0).reshape(bsz, seq, d)

    out = _final_norm(h_res.reshape(bsz * seq, d), final_g.astype(F32), out_dtype)
    return out.reshape(bsz, seq, d)
```
